```python
import jax, jax.numpy as jnp
from jax import lax
import numpy as np

D_MODEL = 1024
BATCH = 8
SEQ = 4096
DEPTH = 2

POOL_GROUPS = 4
POOL_GROUP_DIM = D_MODEL // 16
POOL_WIDTH = POOL_GROUPS * POOL_GROUP_DIM
POOL_WINDOWS = (2, 4, 8, 16)
MLSTM_HEADS = 4
MLSTM_HEAD_DIM = D_MODEL // 16
MLSTM_WIDTH = MLSTM_HEADS * MLSTM_HEAD_DIM
MLSTM_CONV = 5
MLSTM_CHUNK = 64
MLSTM_N_GATES = 4 * MLSTM_HEADS
MLA_HEADS = 8
MLA_NOPE = D_MODEL // 16
MLA_ROPE = D_MODEL // 32
MLA_V = D_MODEL // 16
MLA_Q_LORA = D_MODEL // 4
MLA_KV_LORA = D_MODEL // 8
MLA_WIDTH = MLA_HEADS * MLA_V
ROPE_THETA = 10000.0
ATTN_BLOCK = 128
MIX_WIDTH = POOL_WIDTH + MLSTM_WIDTH + MLA_WIDTH
IN_SPLITS = (POOL_WIDTH, MLSTM_WIDTH, MLSTM_WIDTH, MLSTM_WIDTH, MLSTM_WIDTH,
             MLSTM_N_GATES, MLA_Q_LORA, MLA_KV_LORA, MLA_ROPE)
IN_WIDTH = sum(IN_SPLITS)
N_EXPERTS = 256
TOP_K = 8
N_GROUPS = 8
TOPK_GROUPS = 4
D_EXPERT = D_MODEL // 4
ROUTED_SCALE = 2.5
DISPATCH_BLOCK = 128
DEEPNORM_ALPHA = (2 * DEPTH) ** 0.25
DEEPNORM_BETA = (8 * DEPTH) ** -0.25
LN_EPS = 1e-5
RMS_EPS = 1e-6
NEG_BIG = -1e30

kernel_name = 'hybrid_pool_mlstm_mla_moe_encoder'


def layer_norm(x):
    x32 = x.astype(jnp.float32)
    mu = x32.mean(-1, keepdims=True)
    var = jnp.square(x32 - mu).mean(-1, keepdims=True)
    return ((x32 - mu) * lax.rsqrt(var + LN_EPS)).astype(x.dtype)


def rms_norm(x, g):
    x32 = x.astype(jnp.float32)
    y = x32 * lax.rsqrt(jnp.mean(jnp.square(x32), -1, keepdims=True) + RMS_EPS)
    return y.astype(x.dtype) * g


def rope(x, cos, sin):
    x1, x2 = jnp.split(x, 2, axis=-1)
    return jnp.concatenate([x1 * cos - x2 * sin, x2 * cos + x1 * sin], axis=-1)


def split_cols(u):
    return jnp.split(u, np.cumsum(IN_SPLITS)[:-1].tolist(), axis=-1)


def pool_mixer(u, w_pool, s_pool):
    B, S, _ = u.shape
    ug = u.reshape(B, S, POOL_GROUPS, POOL_GROUP_DIM).astype(jnp.float32)
    csum = jnp.concatenate([jnp.zeros((B, 1, POOL_GROUPS, POOL_GROUP_DIM), jnp.float32),
                            jnp.cumsum(ug, axis=1)], axis=1)
    t = jnp.arange(S)
    pooled = []
    for g, w in enumerate(POOL_WINDOWS):
        lo = jnp.clip(t - w // 2, 0, S)
        hi = jnp.clip(t + w // 2, 0, S)
        win_sum = csum[:, hi, g] - csum[:, lo, g]
        pooled.append(win_sum / (hi - lo).astype(jnp.float32)[:, None])
    pooled = jnp.stack(pooled, axis=2)
    d = (pooled - ug).astype(u.dtype)
    y = jnp.einsum('bsgc,gcd->bsgd', d, w_pool)
    return y.reshape(B, S, POOL_WIDTH) * s_pool


def depthwise_conv(x, w, b):
    y = lax.conv_general_dilated(x, w[:, None, :], window_strides=(1,),
                                 padding=[(MLSTM_CONV // 2, MLSTM_CONV // 2)],
                                 dimension_numbers=('NWC', 'WIO', 'NWC'),
                                 feature_group_count=x.shape[-1])
    return y + b


def mlstm_scan(q, k, v, i_pre, f_pre):
    B, H, S, dk = q.shape
    dv = v.shape[-1]
    nc = S // MLSTM_CHUNK

    def chunks(a):
        return jnp.moveaxis(a.reshape(B, H, nc, MLSTM_CHUNK, *a.shape[3:]), 2, 0)

    lower = jnp.tril(jnp.ones((MLSTM_CHUNK, MLSTM_CHUNK), bool))

    def step(carry, inp):
        C, n, m = carry
        qc, kc, vc, ic, fc = inp
        b = jnp.cumsum(jax.nn.log_sigmoid(fc), axis=-1)
        logw = jnp.where(lower, b[..., :, None] - b[..., None, :] + ic[..., None, :], -jnp.inf)
        m_inter = b + m[..., None]
        m_t = jnp.maximum(m_inter, logw.max(-1))
        w_intra = jnp.exp(logw - m_t[..., None])
        w_inter = jnp.exp(m_inter - m_t)
        qk = jnp.einsum('bhtd,bhsd->bhts', qc, kc) * w_intra
        num = jnp.einsum('bhts,bhse->bhte', qk, vc) + \
            w_inter[..., None] * jnp.einsum('bhtd,bhde->bhte', qc, C)
        den = qk.sum(-1) + w_inter * jnp.einsum('bhtd,bhd->bht', qc, n)
        h = num / jnp.maximum(jnp.abs(den), jnp.exp(-m_t))[..., None]
        b_last = b[..., -1]
        g = b_last[..., None] - b + ic
        m_new = jnp.maximum(b_last + m, g.max(-1))
        w_k = jnp.exp(g - m_new[..., None])
        decay = jnp.exp(b_last + m - m_new)
        C_new = decay[..., None, None] * C + jnp.einsum('bhs,bhsd,bhse->bhde', w_k, kc, vc)
        n_new = decay[..., None] * n + jnp.einsum('bhs,bhsd->bhd', w_k, kc)
        return (C_new, n_new, m_new), h

    init = (jnp.zeros((B, H, dk, dv), jnp.float32), jnp.zeros((B, H, dk), jnp.float32),
            jnp.full((B, H), NEG_BIG, jnp.float32))
    _, h = lax.scan(step, init, (chunks(q), chunks(k), chunks(v), chunks(i_pre), chunks(f_pre)))
    return jnp.moveaxis(h, 0, 2).reshape(B, H, S, dv)


def mlstm_mixer(u_q, u_k, u_v, u_o, u_gate, conv_w, conv_b, gate_b, gn_w):
    B, S, _ = u_q.shape
    qk = jax.nn.silu(depthwise_conv(jnp.concatenate([u_q, u_k], axis=-1), conv_w, conv_b))
    q, k = jnp.split(qk, 2, axis=-1)

    def heads(a):
        return a.reshape(B, S, MLSTM_HEADS, MLSTM_HEAD_DIM).transpose(0, 2, 1, 3).astype(jnp.float32)

    q, k, v = heads(q), heads(k) * MLSTM_HEAD_DIM ** -0.5, heads(u_v)
    gates = (u_gate + gate_b).astype(jnp.float32).reshape(B, S, 4, MLSTM_HEADS).transpose(2, 0, 3, 1)
    h_fwd = mlstm_scan(q, k, v, gates[0], gates[1])
    flip = lambda a: jnp.flip(a, axis=2)
    h_bwd = flip(mlstm_scan(flip(q), flip(k), flip(v), flip(gates[2]), flip(gates[3])))
    h = h_fwd + h_bwd
    mu = h.mean(-1, keepdims=True)
    var = jnp.square(h - mu).mean(-1, keepdims=True)
    h = ((h - mu) * lax.rsqrt(var + LN_EPS)).transpose(0, 2, 1, 3).reshape(B, S, MLSTM_WIDTH)
    return jax.nn.sigmoid(u_o) * (h.astype(u_q.dtype) * gn_w)


def mla_mixer(u_dq, u_dkv, u_kr, g_q, g_kv, w_uq, w_uk, w_uv, cos, sin):
    B, S, _ = u_dq.shape
    q = (rms_norm(u_dq, g_q) @ w_uq).reshape(B, S, MLA_HEADS, MLA_NOPE + MLA_ROPE)
    q_nope = q[..., :MLA_NOPE]
    q_rope = rope(q[..., MLA_NOPE:], cos[:, :, None], sin[:, :, None])
    c_kv = rms_norm(u_dkv, g_kv)
    k_nope = (c_kv @ w_uk).reshape(B, S, MLA_HEADS, MLA_NOPE)
    v = (c_kv @ w_uv).reshape(B, S, MLA_HEADS, MLA_V)
    k_rope = rope(u_kr, cos, sin)
    nb = S // ATTN_BLOCK
    scale = (MLA_NOPE + MLA_ROPE) ** -0.5

    def blocks(a):
        return jnp.moveaxis(a.reshape(B, nb, ATTN_BLOCK, *a.shape[2:]), 1, 0)

    def attend(blk):
        qn, qr = blk
        s = jnp.einsum('bqhd,bkhd->bhqk', qn, k_nope) + jnp.einsum('bqhr,bkr->bhqk', qr, k_rope)
        p = jax.nn.softmax(s.astype(jnp.float32) * scale, axis=-1).astype(v.dtype)
        return jnp.einsum('bhqk,bkhd->bqhd', p, v)

    o = lax.map(attend, (blocks(q_nope), blocks(q_rope)))
    return jnp.moveaxis(o, 0, 1).reshape(B, S, MLA_WIDTH)


def route(xf, w_router, e_bias):
    T = xf.shape[0]
    scores = jax.nn.sigmoid((xf @ w_router).astype(jnp.float32))
    sel = scores + e_bias.astype(jnp.float32)
    grp = sel.reshape(T, N_GROUPS, N_EXPERTS // N_GROUPS)
    gscore = lax.top_k(grp, 2)[0].sum(-1)
    _, gidx = lax.top_k(gscore, TOPK_GROUPS)
    gmask = jnp.any(gidx[:, :, None] == jnp.arange(N_GROUPS), axis=1)
    sel = jnp.where(jnp.repeat(gmask, N_EXPERTS // N_GROUPS, axis=1), sel, -jnp.inf)
    _, idx = lax.top_k(sel, TOP_K)
    w = jnp.take_along_axis(scores, idx, axis=-1)
    w = w / w.sum(-1, keepdims=True) * ROUTED_SCALE
    return idx, w


def routed_experts(xf, idx, wts, w1, w3, w2):
    T, D = xf.shape
    A = T * TOP_K
    e_flat = idx.reshape(-1)
    tok_flat = jnp.arange(A, dtype=jnp.int32) // TOP_K
    w_flat = wts.reshape(-1)
    order = jnp.argsort(e_flat)
    e_s, tok_s, w_s = e_flat[order], tok_flat[order], w_flat[order]
    counts = jnp.bincount(e_flat, length=N_EXPERTS)
    padded = (counts + DISPATCH_BLOCK - 1) // DISPATCH_BLOCK * DISPATCH_BLOCK
    start = jnp.cumsum(counts) - counts
    pend = jnp.cumsum(padded)
    pstart = pend - padded
    pos = pstart[e_s] + (jnp.arange(A, dtype=jnp.int32) - start[e_s])
    n_blocks = -(-A // DISPATCH_BLOCK) + N_EXPERTS
    P = n_blocks * DISPATCH_BLOCK
    tok_pad = jnp.full((P,), T, jnp.int32).at[pos].set(tok_s)
    w_pad = jnp.zeros((P,), w_s.dtype).at[pos].set(w_s)
    blk_start = jnp.arange(n_blocks, dtype=jnp.int32) * DISPATCH_BLOCK
    blk_e = jnp.minimum(jnp.searchsorted(pend, blk_start, side='right'), N_EXPERTS - 1)
    x_pad = jnp.concatenate([xf, jnp.zeros((1, D), xf.dtype)], axis=0)

    def step(acc, inp):
        e, tk, wv = inp
        xb = x_pad[tk]
        hb = jax.nn.silu(xb @ w1[e]) * (xb @ w3[e])
        yb = (hb @ w2[e]) * wv[:, None].astype(xf.dtype)
        return acc.at[tk].add(yb.astype(acc.dtype)), None

    acc, _ = lax.scan(step, jnp.zeros((T + 1, D), xf.dtype),
                      (blk_e, tok_pad.reshape(n_blocks, DISPATCH_BLOCK),
                       w_pad.reshape(n_blocks, DISPATCH_BLOCK)))
    return acc[:T]


def moe_ffn(h, w_router, e_bias, w1, w3, w2, ws1, ws3, ws2):
    B, S, D = h.shape
    xf = h.reshape(B * S, D)
    idx, wts = route(xf, w_router, e_bias)
    shared = (jax.nn.silu(xf @ ws1) * (xf @ ws3)) @ ws2
    return (shared + routed_experts(xf, idx, wts, w1, w3, w2)).reshape(B, S, D)


def setup_inputs(seed: int = 0) -> dict:
    key = jax.random.key(seed)
    ks = iter(jax.random.split(key, 40))
    f32 = jnp.float32
    L, D, H = DEPTH, D_MODEL, MLSTM_HEADS

    def nrm(shape, scale):
        return scale * jax.random.normal(next(ks), shape, f32)

    def gain(shape):
        return 1.0 + nrm(shape, 0.1)

    x = jax.random.normal(next(ks), (BATCH, SEQ, D), f32)
    c = jax.random.normal(next(ks), (BATCH, D), f32)
    positions = (jnp.arange(SEQ, dtype=jnp.int32)[None, :] +
                 jax.random.randint(next(ks), (BATCH, 1), 0, 1024, jnp.int32))
    forget_off = jnp.linspace(3.0, 6.0, H, dtype=f32)
    gate_off = jnp.concatenate([jnp.zeros((H,), f32), forget_off, jnp.zeros((H,), f32), forget_off])
    return {
        'x': x,
        'c': c,
        'positions': positions,
        'w_ada': nrm((L, D, 6 * D), 0.5 * D ** -0.5),
        'b_ada': nrm((L, 6 * D), 0.02),
        'w_in': nrm((L, D, IN_WIDTH), D ** -0.5),
        'w_pool': nrm((L, POOL_GROUPS, POOL_GROUP_DIM, POOL_GROUP_DIM), POOL_GROUP_DIM ** -0.5),
        's_pool': gain((L, POOL_WIDTH)),
        'conv_w': nrm((L, MLSTM_CONV, 2 * MLSTM_WIDTH), MLSTM_CONV ** -0.5),
        'conv_b': nrm((L, 2 * MLSTM_WIDTH), 0.02),
        'gate_b': gate_off[None, :] + nrm((L, MLSTM_N_GATES), 0.1),
        'gn_w': gain((L, MLSTM_WIDTH)),
        'g_q': gain((L, MLA_Q_LORA)),
        'g_kv': gain((L, MLA_KV_LORA)),
        'w_uq': nrm((L, MLA_Q_LORA, MLA_HEADS * (MLA_NOPE + MLA_ROPE)), MLA_Q_LORA ** -0.5),
        'w_uk': nrm((L, MLA_KV_LORA, MLA_HEADS * MLA_NOPE), MLA_KV_LORA ** -0.5),
        'w_uv': nrm((L, MLA_KV_LORA, MLA_HEADS * MLA_V), DEEPNORM_BETA * MLA_KV_LORA ** -0.5),
        'w_out': nrm((L, MIX_WIDTH, D), DEEPNORM_BETA * MIX_WIDTH ** -0.5),
        'ln1_g': gain((L, D)),
        'ln1_b': nrm((L, D), 0.02),
        'w_router': nrm((L, D, N_EXPERTS), D ** -0.5),
        'e_bias': nrm((L, N_EXPERTS), 0.01),
        'w1': nrm((L, N_EXPERTS, D, D_EXPERT), D ** -0.5),
        'w3': nrm((L, N_EXPERTS, D, D_EXPERT), D ** -0.5),
        'w2': nrm((L, N_EXPERTS, D_EXPERT, D), DEEPNORM_BETA * D_EXPERT ** -0.5),
        'ws1': nrm((L, D, D_EXPERT), D ** -0.5),
        'ws3': nrm((L, D, D_EXPERT), D ** -0.5),
        'ws2': nrm((L, D_EXPERT, D), DEEPNORM_BETA * D_EXPERT ** -0.5),
        'ln2_g': gain((L, D)),
        'ln2_b': nrm((L, D), 0.02),
    }


def reference(x, c, positions, w_ada, b_ada, w_in, w_pool, s_pool, conv_w, conv_b, gate_b,
              gn_w, g_q, g_kv, w_uq, w_uk, w_uv, w_out, ln1_g, ln1_b, w_router, e_bias,
              w1, w3, w2, ws1, ws3, ws2, ln2_g, ln2_b):
    inv_freq = ROPE_THETA ** (-jnp.arange(0, MLA_ROPE, 2, dtype=jnp.float32) / MLA_ROPE)
    ang = positions.astype(jnp.float32)[..., None] * inv_freq
    cos, sin = jnp.cos(ang).astype(x.dtype), jnp.sin(ang).astype(x.dtype)
    c_act = jax.nn.silu(c)
    for l in range(DEPTH):
        ada = c_act @ w_ada[l] + b_ada[l]
        sh1, sc1, g1, sh2, sc2, g2 = [a[:, None, :] for a in jnp.split(ada, 6, axis=-1)]
        h = layer_norm(x) * (1.0 + sc1) + sh1
        u = h @ w_in[l]
        u_pool, u_q, u_k, u_v, u_o, u_gate, u_dq, u_dkv, u_kr = split_cols(u)
        y_pool = pool_mixer(u_pool, w_pool[l], s_pool[l])
        y_mlstm = mlstm_mixer(u_q, u_k, u_v, u_o, u_gate, conv_w[l], conv_b[l], gate_b[l], gn_w[l])
        y_mla = mla_mixer(u_dq, u_dkv, u_kr, g_q[l], g_kv[l], w_uq[l], w_uk[l], w_uv[l], cos, sin)
        mix = jnp.concatenate([y_pool, y_mlstm, y_mla], axis=-1) @ w_out[l]
        x = layer_norm(DEEPNORM_ALPHA * x + g1 * mix) * ln1_g[l] + ln1_b[l]
        h = layer_norm(x) * (1.0 + sc2) + sh2
        ffn = moe_ffn(h, w_router[l], e_bias[l], w1[l], w3[l], w2[l], ws1[l], ws3[l], ws2[l])
        x = layer_norm(DEEPNORM_ALPHA * x + g2 * ffn) * ln2_g[l] + ln2_b[l]
    return x
```

```python
import functools

import jax
import jax.numpy as jnp
import numpy as np
from jax import lax
from jax.experimental import pallas as pl
from jax.experimental.pallas import tpu as pltpu

F32 = jnp.float32
BF16 = jnp.bfloat16
I32 = jnp.int32

D_MODEL = 1024
POOL_WINDOWS = (2, 4, 8, 16)
POOL_GROUP_DIM = 64
POOL_WIDTH = 256
ML_HEADS = 4
ML_DH = 64
ML_WIDTH = 256
ML_CONV = 5
MLA_HEADS = 8
MLA_NOPE = 64
MLA_ROPE = 32
MLA_V = 64
MLA_Q_LORA = 256
MLA_KV_LORA = 128
MLA_HEAD_PAD = 128
ROPE_THETA = 10000.0
N_EXPERTS = 256
TOP_K = 8
N_GROUPS = 8
TOPK_GROUPS = 4
GROUP_SIZE = N_EXPERTS // N_GROUPS
D_EXPERT = 256
ROUTED_SCALE = 2.5
LN_EPS = 1e-5
RMS_EPS = 1e-6
NEG_BIG = -1e30
NEG_INF = float("-inf")

EXPERT_BLOCK = 256
VMEM_LIMIT = 56 * 1024 * 1024


def _cp(sem, vmem=None):
    return pltpu.CompilerParams(dimension_semantics=sem, vmem_limit_bytes=vmem)


def _ln(x):
    mu = jnp.mean(x, axis=-1, keepdims=True)
    xc = x - mu
    var = jnp.mean(xc * xc, axis=-1, keepdims=True)
    return xc * lax.rsqrt(var + LN_EPS)


def _silu(x):
    return x * jax.nn.sigmoid(x)


def _rms(x, g):
    return x * lax.rsqrt(jnp.mean(x * x, axis=-1, keepdims=True) + RMS_EPS) * g


def _dot(a, b):
    return jnp.dot(a, b, preferred_element_type=F32)


def _dot_nt(a, b):
    return lax.dot_general(a, b, (((1,), (1,)), ((), ())), preferred_element_type=F32)


def _dot_tn(a, b):
    return lax.dot_general(a, b, (((0,), (0,)), ((), ())), preferred_element_type=F32)


def _ada_kernel(c_ref, w_ref, b_ref, o_ref):
    ca = _silu(c_ref[...]).astype(BF16)
    o_ref[...] = _dot(ca, w_ref[...].astype(BF16)) + b_ref[...]


def _ada(c, w_ada, b_ada):
    L, D, N = w_ada.shape
    B = c.shape[0]
    tn = 1536
    return pl.pallas_call(
        _ada_kernel,
        grid=(L, N // tn),
        in_specs=[pl.BlockSpec((B, D), lambda l, j: (0, 0)),
                  pl.BlockSpec((None, D, tn), lambda l, j: (l, 0, j)),
                  pl.BlockSpec((None, 1, tn), lambda l, j: (l, 0, j))],
        out_specs=pl.BlockSpec((None, B, tn), lambda l, j: (l, 0, j)),
        out_shape=jax.ShapeDtypeStruct((L, B, N), F32),
        compiler_params=_cp(("parallel", "parallel"), VMEM_LIMIT),
        name="ada",
    )(c, w_ada, b_ada.reshape(L, 1, N))


def _inproj_kernel(x_ref, sc_ref, sh_ref, w_ref, pool_ref, qk_ref, v_ref, o_ref, dq_ref,
                   kv_ref, gate_ref):
    h = (_ln(x_ref[...]) * (1.0 + sc_ref[...]) + sh_ref[...]).astype(BF16)
    pool_ref[...] = _dot(h, w_ref[:, 0:256])
    qk_ref[...] = _dot(h, w_ref[:, 256:768])
    v_ref[...] = _dot(h, w_ref[:, 768:1024]).astype(BF16)
    o_ref[...] = _dot(h, w_ref[:, 1024:1280])
    dq_ref[...] = _dot(h, w_ref[:, 1280:1536])
    kv_ref[...] = _dot(h, w_ref[:, 1536:1792])
    gate_ref[...] = _dot(h, w_ref[:, 1792:1920])


def _inproj(x, sc, sh, w):
    B, S, D = x.shape
    tm = min(512, S)
    widths = (256, 512, 256, 256, 256, 256, 128)
    dtypes = (F32, F32, BF16, F32, F32, F32, F32)
    row = lambda n: pl.BlockSpec((None, tm, n), lambda b, i: (b, i, 0))
    return pl.pallas_call(
        _inproj_kernel,
        grid=(B, S // tm),
        in_specs=[row(D),
                  pl.BlockSpec((None, 1, D), lambda b, i: (b, 0, 0)),
                  pl.BlockSpec((None, 1, D), lambda b, i: (b, 0, 0)),
                  pl.BlockSpec(w.shape, lambda b, i: (0, 0))],
        out_specs=[row(n) for n in widths],
        out_shape=[jax.ShapeDtypeStruct((B, S, n), dt) for n, dt in zip(widths, dtypes)],
        compiler_params=_cp(("parallel", "parallel"), VMEM_LIMIT),
        name="inproj",
    )(x, sc, sh, w)


def _shift_dn(a, k, row):
    return jnp.where(row >= k, pltpu.roll(a, k, 0), 0.0)


def _shift_up(a, k, row):
    n = a.shape[0]
    return jnp.where(row < n - k, pltpu.roll(a, n - k, 0), 0.0)


def _pool_kernel(u_ref, w_ref, s_ref, y_ref):
    x = u_ref[...]
    S, C = x.shape
    row = lax.broadcasted_iota(I32, (S, C), 0)
    lane = lax.broadcasted_iota(I32, (S, C), 1)
    rowf = row.astype(F32)
    p1, f1 = x, x
    p2 = p1 + _shift_dn(p1, 1, row)
    f2 = f1 + _shift_up(f1, 1, row)
    p4 = p2 + _shift_dn(p2, 2, row)
    f4 = f2 + _shift_up(f2, 2, row)
    p8 = p4 + _shift_dn(p4, 4, row)
    f8 = f4 + _shift_up(f4, 4, row)
    wins = []
    for half, p, f in ((1, p1, f1), (2, p2, f2), (4, p4, f4), (8, p8, f8)):
        total = _shift_dn(p, 1, row) + f
        cnt = jnp.minimum(rowf + half, float(S)) - jnp.maximum(rowf - half, 0.0)
        wins.append(total / cnt)
    pooled = jnp.where(lane < 64, wins[0],
                       jnp.where(lane < 128, wins[1], jnp.where(lane < 192, wins[2], wins[3])))
    d = (pooled - x).astype(BF16)
    y_ref[...] = (_dot(d, w_ref[...]) * s_ref[...]).astype(BF16)


def _pool(u_pool, w_bd, s_pool):
    B, S, C = u_pool.shape
    return pl.pallas_call(
        _pool_kernel,
        grid=(B,),
        in_specs=[pl.BlockSpec((None, S, C), lambda b: (b, 0, 0)),
                  pl.BlockSpec((C, C), lambda b: (0, 0)),
                  pl.BlockSpec((1, C), lambda b: (0, 0))],
        out_specs=pl.BlockSpec((None, S, C), lambda b: (b, 0, 0)),
        out_shape=jax.ShapeDtypeStruct((B, S, C), BF16),
        compiler_params=_cp(("parallel",), VMEM_LIMIT),
        name="pool",
    )(u_pool, w_bd, s_pool)


def _conv_kernel(u_ref, w_ref, b_ref, scale_ref, y_ref):
    x = u_ref[...]
    S, C = x.shape
    row = lax.broadcasted_iota(I32, (S, C), 0)
    y = (_shift_dn(x, 2, row) * w_ref[0:1, :] + _shift_dn(x, 1, row) * w_ref[1:2, :]
         + x * w_ref[2:3, :] + _shift_up(x, 1, row) * w_ref[3:4, :]
         + _shift_up(x, 2, row) * w_ref[4:5, :] + b_ref[...])
    y_ref[...] = (_silu(y) * scale_ref[...]).astype(BF16)


def _conv(u_qk, conv_w, conv_b, scale):
    B, S, C = u_qk.shape
    return pl.pallas_call(
        _conv_kernel,
        grid=(B,),
        in_specs=[pl.BlockSpec((None, S, C), lambda b: (b, 0, 0)),
                  pl.BlockSpec((ML_CONV, C), lambda b: (0, 0)),
                  pl.BlockSpec((1, C), lambda b: (0, 0)),
                  pl.BlockSpec((1, C), lambda b: (0, 0))],
        out_specs=pl.BlockSpec((None, S, C), lambda b: (b, 0, 0)),
        out_shape=jax.ShapeDtypeStruct((B, S, C), BF16),
        compiler_params=_cp(("parallel",), VMEM_LIMIT),
        name="mlstm_conv",
    )(u_qk, conv_w, conv_b, scale)


def _log_sigmoid(x):
    return jnp.minimum(x, 0.0) - jnp.log(1.0 + jnp.exp(-jnp.abs(x)))


def _split3(x):
    a = x.astype(BF16)
    r = x - a.astype(F32)
    b = r.astype(BF16)
    c = (r - b.astype(F32)).astype(BF16)
    return a, b, c


def _mlstm_chunk(q, k, vext, b_col, b_row, i_col, i_row, b_last, allowed, c_ext, m_prev):
    logw = jnp.where(allowed, b_col - b_row + i_row, NEG_INF)
    m_inter = b_col + m_prev
    m_t = jnp.maximum(m_inter, jnp.max(logw, axis=1, keepdims=True))
    w_intra = jnp.exp(logw - m_t)
    w_inter = jnp.exp(m_inter - m_t)
    qk = (_dot_nt(q, k) * w_intra).astype(BF16)
    nd = _dot(qk, vext) + w_inter * _dot(q, c_ext.astype(BF16))
    den = jnp.maximum(jnp.abs(nd[:, ML_DH:ML_DH + 1]), jnp.exp(-m_t))
    h = nd[:, :ML_DH] / den
    m_new = jnp.maximum(b_last + m_prev, jnp.max(b_last - b_row + i_row, axis=1, keepdims=True))
    w_k = jnp.exp(b_last - b_col + i_col - m_new)
    decay = jnp.exp(b_last + m_prev - m_new)
    c_new = decay * c_ext + _dot_tn(k, (w_k * vext.astype(F32)).astype(BF16))
    return h, c_new, m_new


def _mlstm_kernel(q_ref, k_ref, v_ref, uo_ref, gc_ref, gr_ref, bc_ref, br_ref, gn_ref, y_ref,
                  hf_ref, hb_ref, *, chunk):
    S = q_ref.shape[0]
    L = chunk
    nc = S // L
    ri = lax.broadcasted_iota(I32, (L, L), 0)
    ci = lax.broadcasted_iota(I32, (L, L), 1)
    lower = ci <= ri
    upper = ci >= ri
    tril = jnp.where(lower, 1.0, 0.0).astype(BF16)
    triu = jnp.where(upper, 1.0, 0.0).astype(BF16)
    bias_c = bc_ref[...]
    bias_r = br_ref[...]

    def cums(r0):
        gc = gc_ref[pl.ds(r0, L), :] + bias_c
        gr = gr_ref[:, pl.ds(r0, L)] + bias_r
        c1, c2, c3 = _split3(_log_sigmoid(gc))
        r1, r2, r3 = _split3(_log_sigmoid(gr))
        pre_c = _dot(tril, c1) + _dot(tril, c2) + _dot(tril, c3)
        suf_c = _dot(triu, c1) + _dot(triu, c2) + _dot(triu, c3)
        pre_r = _dot(r1, triu) + _dot(r2, triu) + _dot(r3, triu)
        suf_r = _dot(r1, tril) + _dot(r2, tril) + _dot(r3, tril)
        return gc, gr, pre_c, suf_c, pre_r, suf_r

    def step(i, carry):
        cf, mf, cb, mb = carry
        rf = pl.multiple_of(i * L, L)
        rb = pl.multiple_of((nc - 1 - i) * L, L)
        gc, gr, pre_c, _, pre_r, _ = cums(rf)
        h, cf, mf = _mlstm_chunk(
            q_ref[pl.ds(rf, L), :], k_ref[pl.ds(rf, L), :], v_ref[pl.ds(rf, L), :],
            pre_c[:, 1:2], pre_r[1:2, :], gc[:, 0:1], gr[0:1, :], pre_r[1:2, L - 1:L],
            lower, cf, mf)
        hf_ref[pl.ds(rf, L), :] = h
        gc, gr, _, suf_c, _, suf_r = cums(rb)
        h, cb, mb = _mlstm_chunk(
            q_ref[pl.ds(rb, L), :], k_ref[pl.ds(rb, L), :], v_ref[pl.ds(rb, L), :],
            suf_c[:, 3:4], suf_r[3:4, :], gc[:, 2:3], gr[2:3, :], suf_r[3:4, 0:1],
            upper, cb, mb)
        hb_ref[pl.ds(rb, L), :] = h
        return cf, mf, cb, mb

    c0 = jnp.zeros((ML_DH, 128), F32)
    m0 = jnp.full((1, 1), NEG_BIG, F32)
    lax.fori_loop(0, nc, step, (c0, m0, c0, m0))
    h = hf_ref[...] + hb_ref[...]
    y_ref[...] = (jax.nn.sigmoid(uo_ref[...]) * (_ln(h) * gn_ref[...])).astype(BF16)


def _mlstm(qk_heads, v_ext, uo_heads, g_col, g_row, b_col, b_row, gn_w):
    B, _, S, dh = qk_heads.shape
    H = ML_HEADS
    L = min(256, S)
    seq = lambda n, off: pl.BlockSpec((None, None, S, n), lambda b, h: (b, h + off, 0, 0))
    return pl.pallas_call(
        functools.partial(_mlstm_kernel, chunk=L),
        grid=(B, H),
        in_specs=[seq(dh, 0), seq(dh, H), seq(128, 0), seq(dh, 0), seq(4, 0),
                  pl.BlockSpec((None, None, 4, S), lambda b, h: (b, h, 0, 0)),
                  pl.BlockSpec((None, 1, 4), lambda b, h: (h, 0, 0)),
                  pl.BlockSpec((None, 4, 1), lambda b, h: (h, 0, 0)),
                  pl.BlockSpec((None, 1, dh), lambda b, h: (h, 0, 0))],
        out_specs=seq(dh, 0),
        out_shape=jax.ShapeDtypeStruct((B, H, S, dh), BF16),
        scratch_shapes=[pltpu.VMEM((S, dh), F32), pltpu.VMEM((S, dh), F32)],
        compiler_params=_cp(("parallel", "parallel"), VMEM_LIMIT),
        name="mlstm_scan",
    )(qk_heads, qk_heads, v_ext, uo_heads, g_col, g_row, b_col, b_row, gn_w)


def _mla_proj_kernel(dq_ref, kv_ref, pos_ref, frq_ref, sgn_ref, gq_ref, gkv_ref, wq_ref, wqs_ref,
                     wk_ref, wks_ref, wv_ref, q_ref, k_ref, v_ref, *, scale):
    ang = pos_ref[...].astype(F32) * frq_ref[...]
    cos = jnp.cos(ang)
    sin = jnp.sin(ang) * sgn_ref[...]
    qn = _rms(dq_ref[...], gq_ref[...]).astype(BF16)
    a = _dot(qn, wq_ref[...])
    a_sw = _dot(qn, wqs_ref[...])
    kv = kv_ref[...]
    ckv = _rms(kv[:, :MLA_KV_LORA], gkv_ref[...])
    kin = jnp.concatenate([ckv, kv[:, MLA_KV_LORA:]], axis=1).astype(BF16)
    ak = _dot(kin, wk_ref[...])
    ak_sw = _dot(kin, wks_ref[...])
    for h in range(MLA_HEADS):
        sl = slice(h * MLA_HEAD_PAD, (h + 1) * MLA_HEAD_PAD)
        q_ref[:, sl] = ((a[:, sl] * cos + a_sw[:, sl] * sin) * scale).astype(BF16)
        k_ref[:, sl] = (ak[:, sl] * cos + ak_sw[:, sl] * sin).astype(BF16)
    v_ref[...] = _dot(ckv.astype(BF16), wv_ref[...]).astype(BF16)


def _mla_proj(u_dq, u_kv, pos, frq, sgn, g_q, g_kv, wq, wqs, wk, wks, wv):
    B, S, _ = u_dq.shape
    tm = min(512, S)
    hw = MLA_HEADS * MLA_HEAD_PAD
    vw = MLA_HEADS * MLA_V
    row = lambda n: pl.BlockSpec((None, tm, n), lambda b, i: (b, i, 0))
    full = lambda a: pl.BlockSpec(a.shape, lambda b, i: (0,) * a.ndim)
    scale = (MLA_NOPE + MLA_ROPE) ** -0.5
    return pl.pallas_call(
        functools.partial(_mla_proj_kernel, scale=scale),
        grid=(B, S // tm),
        in_specs=[row(MLA_Q_LORA), row(256), row(1), full(frq), full(sgn), full(g_q), full(g_kv),
                  full(wq), full(wqs), full(wk), full(wks), full(wv)],
        out_specs=[row(hw), row(hw), row(vw)],
        out_shape=[jax.ShapeDtypeStruct((B, S, hw), BF16),
                   jax.ShapeDtypeStruct((B, S, hw), BF16),
                   jax.ShapeDtypeStruct((B, S, vw), BF16)],
        compiler_params=_cp(("parallel", "parallel"), VMEM_LIMIT),
        name="mla_proj",
    )(u_dq, u_kv, pos, frq, sgn, g_q, g_kv, wq, wqs, wk, wks, wv)


def _attn_kernel(q_ref, k_ref, v_ref, o_ref, *, tk):
    tq = q_ref.shape[0]
    S = k_ref.shape[0]
    nk = S // tk
    outs = []
    for h in range(2):
        q = q_ref[:, h * MLA_HEAD_PAD:(h + 1) * MLA_HEAD_PAD]

        def step(j, carry, q=q, h=h):
            m, l, acc = carry
            r0 = pl.multiple_of(j * tk, tk)
            k = k_ref[pl.ds(r0, tk), h * MLA_HEAD_PAD:(h + 1) * MLA_HEAD_PAD]
            s = _dot_nt(q, k)
            m_new = jnp.maximum(m, jnp.max(s, axis=1, keepdims=True))
            alpha = jnp.exp(m - m_new)
            p = jnp.exp(s - m_new)
            l = alpha * l + jnp.sum(p, axis=1, keepdims=True)
            acc = alpha * acc + _dot(p.astype(BF16), v_ref[pl.ds(r0, tk), :])
            return m_new, l, acc

        m0 = jnp.full((tq, 1), NEG_INF, F32)
        l0 = jnp.zeros((tq, 1), F32)
        a0 = jnp.zeros((tq, 2 * MLA_V), F32)
        _, l, acc = lax.fori_loop(0, nk, step, (m0, l0, a0))
        outs.append(acc / l)
    lane = lax.broadcasted_iota(I32, (tq, 2 * MLA_V), 1)
    o_ref[...] = jnp.where(lane < MLA_V, outs[0], outs[1]).astype(BF16)


def _attn(qh, kh, vh):
    B, S, _ = qh.shape
    tq = min(512, S)
    tk = min(512, S)
    pw = 2 * MLA_HEAD_PAD
    return pl.pallas_call(
        functools.partial(_attn_kernel, tk=tk),
        grid=(B, MLA_HEADS // 2, S // tq),
        in_specs=[pl.BlockSpec((None, tq, pw), lambda b, p, i: (b, i, p)),
                  pl.BlockSpec((None, S, pw), lambda b, p, i: (b, 0, p)),
                  pl.BlockSpec((None, S, 2 * MLA_V), lambda b, p, i: (b, 0, p))],
        out_specs=pl.BlockSpec((None, tq, 2 * MLA_V), lambda b, p, i: (b, i, p)),
        out_shape=jax.ShapeDtypeStruct((B, S, MLA_HEADS * MLA_V), BF16),
        compiler_params=_cp(("parallel", "parallel", "parallel"), VMEM_LIMIT),
        name="mla_attn",
    )(qh, kh, vh)


def _outproj_kernel(yp_ref, ym_ref, ya_ref, x_ref, g1_ref, lg_ref, lb_ref, sc_ref, sh_ref, wo_ref,
                    wr_ref, x1_ref, h2_ref, lt_ref, *, alpha):
    mix = (_dot(yp_ref[...], wo_ref[0:256, :]) + _dot(ym_ref[...], wo_ref[256:512, :])
           + _dot(ya_ref[...], wo_ref[512:1024, :]))
    x1 = _ln(alpha * x_ref[...] + g1_ref[...] * mix) * lg_ref[...] + lb_ref[...]
    x1_ref[...] = x1
    h2 = _ln(x1) * (1.0 + sc_ref[...]) + sh_ref[...]
    h2_ref[...] = h2
    lt_ref[...] = _dot_nt(wr_ref[...], h2.astype(BF16))


def _outproj(y_pool, y_ml, y_mla, x, g1, ln_g, ln_b, sc2, sh2, w_out, w_router_t, alpha):
    B, S, D = x.shape
    tm = min(512, S)
    row = lambda n: pl.BlockSpec((None, tm, n), lambda b, i: (b, i, 0))
    per_b = pl.BlockSpec((None, 1, D), lambda b, i: (b, 0, 0))
    vec = pl.BlockSpec((1, D), lambda b, i: (0, 0))
    return pl.pallas_call(
        functools.partial(_outproj_kernel, alpha=alpha),
        grid=(B, S // tm),
        in_specs=[row(256), row(256), row(512), row(D), per_b, vec, vec, per_b, per_b,
                  pl.BlockSpec((D, D), lambda b, i: (0, 0)),
                  pl.BlockSpec((N_EXPERTS, D), lambda b, i: (0, 0))],
        out_specs=[row(D), row(D), pl.BlockSpec((None, N_EXPERTS, tm), lambda b, i: (b, 0, i))],
        out_shape=[jax.ShapeDtypeStruct((B, S, D), F32), jax.ShapeDtypeStruct((B, S, D), F32),
                   jax.ShapeDtypeStruct((B, N_EXPERTS, S), F32)],
        compiler_params=_cp(("parallel", "parallel"), VMEM_LIMIT),
        name="outproj",
    )(y_pool, y_ml, y_mla, x, g1, ln_g, ln_b, sc2, sh2, w_out, w_router_t)


def _first_max(v, iota, n):
    m = jnp.max(v, axis=0, keepdims=True)
    first = jnp.min(jnp.where(v == m, iota, n), axis=0, keepdims=True)
    return m, first


def _route_kernel(lt_ref, eb_ref, idx_ref, w_ref):
    scores = jax.nn.sigmoid(lt_ref[...])
    sel = scores + eb_ref[...]
    E, tt = sel.shape
    gi = lax.broadcasted_iota(I32, (GROUP_SIZE, tt), 0)
    groups = [sel[g * GROUP_SIZE:(g + 1) * GROUP_SIZE, :] for g in range(N_GROUPS)]
    gs = []
    for grp in groups:
        m1, f1 = _first_max(grp, gi, GROUP_SIZE)
        m2 = jnp.max(jnp.where(gi == f1, NEG_INF, grp), axis=0, keepdims=True)
        gs.append(m1 + m2)
    chosen = [jnp.zeros((1, tt), F32) for _ in range(N_GROUPS)]
    for _ in range(TOPK_GROUPS):
        m = functools.reduce(jnp.maximum, gs)
        first = functools.reduce(
            jnp.minimum, [jnp.where(gs[g] == m, g, N_GROUPS) for g in range(N_GROUPS)])
        for g in range(N_GROUPS):
            hit = first == g
            chosen[g] = jnp.where(hit, 1.0, chosen[g])
            gs[g] = jnp.where(hit, NEG_INF, gs[g])
    cand = jnp.concatenate(
        [jnp.where(chosen[g] > 0.5, groups[g], NEG_INF) for g in range(N_GROUPS)], axis=0)
    ei = lax.broadcasted_iota(I32, (E, tt), 0)
    ws = []
    for k in range(TOP_K):
        _, f = _first_max(cand, ei, E)
        hit = ei == f
        idx_ref[k:k + 1, :] = f
        ws.append(jnp.sum(jnp.where(hit, scores, 0.0), axis=0, keepdims=True))
        cand = jnp.where(hit, NEG_INF, cand)
    norm = ROUTED_SCALE / functools.reduce(jnp.add, ws)
    for k in range(TOP_K):
        w_ref[k:k + 1, :] = ws[k] * norm


def _route(logits_t, e_bias):
    B, E, S = logits_t.shape
    tt = min(512, S)
    return pl.pallas_call(
        _route_kernel,
        grid=(B, S // tt),
        in_specs=[pl.BlockSpec((None, E, tt), lambda b, i: (b, 0, i)),
                  pl.BlockSpec((E, 1), lambda b, i: (0, 0))],
        out_specs=[pl.BlockSpec((None, TOP_K, tt), lambda b, i: (b, 0, i)),
                   pl.BlockSpec((None, TOP_K, tt), lambda b, i: (b, 0, i))],
        out_shape=[jax.ShapeDtypeStruct((B, TOP_K, S), I32),
                   jax.ShapeDtypeStruct((B, TOP_K, S), F32)],
        compiler_params=_cp(("parallel", "parallel"), VMEM_LIMIT),
        name="route",
    )(logits_t, e_bias)


def _rank_kernel(idx_ref, rank_ref, cnt_ref, carry_ref):
    first = jnp.logical_and(pl.program_id(0) == 0, pl.program_id(1) == 0)

    @pl.when(first)
    def _():
        carry_ref[...] = jnp.zeros_like(carry_ref)

    K, tt = idx_ref.shape
    E = carry_ref.shape[0]
    ei = lax.broadcasted_iota(I32, (E, tt), 0)
    ri = lax.broadcasted_iota(I32, (tt, tt), 0)
    ci = lax.broadcasted_iota(I32, (tt, tt), 1)
    before = jnp.where(ri < ci, 1.0, 0.0).astype(BF16)
    base = carry_ref[...]
    for k in range(K):
        hit = ei == idx_ref[k:k + 1, :]
        onehot = jnp.where(hit, 1.0, 0.0)
        prefix = _dot(onehot.astype(BF16), before)
        rank = jnp.sum(jnp.where(hit, base + prefix, 0.0), axis=0, keepdims=True)
        rank_ref[k:k + 1, :] = rank.astype(I32)
        base = base + jnp.sum(onehot, axis=1, keepdims=True)
    carry_ref[...] = base
    cnt_ref[...] = base


def _rank(idx_t):
    B, K, S = idx_t.shape
    tt = min(256, S)
    return pl.pallas_call(
        _rank_kernel,
        grid=(B, S // tt),
        in_specs=[pl.BlockSpec((None, K, tt), lambda b, i: (b, 0, i))],
        out_specs=[pl.BlockSpec((None, K, tt), lambda b, i: (b, 0, i)),
                   pl.BlockSpec((N_EXPERTS, 1), lambda b, i: (0, 0))],
        out_shape=[jax.ShapeDtypeStruct((B, K, S), I32),
                   jax.ShapeDtypeStruct((N_EXPERTS, 1), F32)],
        scratch_shapes=[pltpu.VMEM((N_EXPERTS, 1), F32)],
        compiler_params=_cp(("arbitrary", "arbitrary"), VMEM_LIMIT),
        name="rank",
    )(idx_t)


def _dispatch_kernel(pos_ref, h_ref, xs_ref, sem, *, tt):
    base = pl.program_id(0) * tt

    def copy(t, k):
        return pltpu.make_async_copy(h_ref.at[pl.ds(base + t, 1), :],
                                     xs_ref.at[pl.ds(pos_ref[k, t], 1), :], sem)

    def issue(t, c):
        for k in range(TOP_K):
            copy(t, k).start()
        return c

    def drain(t, c):
        for k in range(TOP_K):
            copy(t, k).wait()
        return c

    lax.fori_loop(0, tt, issue, 0)
    lax.fori_loop(0, tt, drain, 0)


def _dispatch(pos_kt, h2_flat, n_rows):
    T, D = h2_flat.shape
    tt = 128
    return pl.pallas_call(
        functools.partial(_dispatch_kernel, tt=tt),
        grid=(T // tt,),
        in_specs=[pl.BlockSpec((TOP_K, tt), lambda i: (0, i), memory_space=pltpu.SMEM),
                  pl.BlockSpec(memory_space=pl.ANY)],
        out_specs=pl.BlockSpec(memory_space=pl.ANY),
        out_shape=jax.ShapeDtypeStruct((n_rows, D), F32),
        scratch_shapes=[pltpu.SemaphoreType.DMA(())],
        compiler_params=pltpu.CompilerParams(dimension_semantics=("arbitrary",),
                                             has_side_effects=True),
        name="dispatch",
    )(pos_kt, h2_flat)


def _expert_kernel(be_ref, nu_ref, xs_ref, w1_ref, w3_ref, w2_ref, ys_ref, w1b, w3b, w2b):
    b = pl.program_id(0)
    prev = be_ref[jnp.maximum(b - 1, 0)]
    fresh = jnp.logical_or(b == 0, be_ref[b] != prev)

    @pl.when(jnp.logical_and(fresh, b < nu_ref[0]))
    def _():
        w1b[...] = w1_ref[...].astype(BF16)
        w3b[...] = w3_ref[...].astype(BF16)
        w2b[...] = w2_ref[...].astype(BF16)

    @pl.when(b < nu_ref[0])
    def _():
        x = xs_ref[...].astype(BF16)
        hb = (_silu(_dot(x, w1b[...])) * _dot(x, w3b[...])).astype(BF16)
        ys_ref[...] = _dot(hb, w2b[...])


def _experts(blk_expert, n_used, xs, w1, w3, w2, layer):
    P, D = xs.shape
    nb = P // EXPERT_BLOCK
    last = lambda b, be, nu: jnp.minimum(b, nu[0] - 1)
    wspec = lambda r, c: pl.BlockSpec((None, None, r, c),
                                      lambda b, be, nu: (layer, be[last(b, be, nu)], 0, 0))
    grid_spec = pltpu.PrefetchScalarGridSpec(
        num_scalar_prefetch=2,
        grid=(nb,),
        in_specs=[pl.BlockSpec((EXPERT_BLOCK, D), lambda b, be, nu: (last(b, be, nu), 0)),
                  wspec(D, D_EXPERT), wspec(D, D_EXPERT), wspec(D_EXPERT, D)],
        out_specs=pl.BlockSpec((EXPERT_BLOCK, D), lambda b, be, nu: (last(b, be, nu), 0)),
        scratch_shapes=[pltpu.VMEM((D, D_EXPERT), BF16), pltpu.VMEM((D, D_EXPERT), BF16),
                        pltpu.VMEM((D_EXPERT, D), BF16)],
    )
    return pl.pallas_call(
        _expert_kernel,
        grid_spec=grid_spec,
        out_shape=jax.ShapeDtypeStruct((P, D), F32),
        compiler_params=_cp(("arbitrary",), VMEM_LIMIT),
        name="experts",
    )(blk_expert, n_used, xs, w1, w3, w2)


def _combine_kernel(pos_ref, ys_ref, wt_ref, h2_ref, x1_ref, g2_ref, lg_ref, lb_ref, ws1_ref,
                    ws3_ref, ws2_ref, o_ref, buf, sem, *, tt, alpha):
    def copy(t, k):
        return pltpu.make_async_copy(ys_ref.at[pl.ds(pos_ref[k, t], 1), :],
                                     buf.at[k, pl.ds(t, 1), :], sem)

    def issue(t, c):
        for k in range(TOP_K):
            copy(t, k).start()
        return c

    def drain(t, c):
        for k in range(TOP_K):
            copy(t, k).wait()
        return c

    lax.fori_loop(0, tt, issue, 0)
    h = h2_ref[...].astype(BF16)
    ffn = _dot((_silu(_dot(h, ws1_ref[...])) * _dot(h, ws3_ref[...])).astype(BF16), ws2_ref[...])
    lax.fori_loop(0, tt, drain, 0)
    wt = wt_ref[...]
    for k in range(TOP_K):
        ffn = ffn + wt[:, k:k + 1] * buf[k]
    o_ref[...] = _ln(alpha * x1_ref[...] + g2_ref[...] * ffn) * lg_ref[...] + lb_ref[...]


def _combine(pos_kt, ys, w_tk, h2, x1, g2, ln_g, ln_b, ws1, ws3, ws2, alpha):
    B, S, D = x1.shape
    tt = 128
    nt = S // tt
    row = lambda n: pl.BlockSpec((None, tt, n), lambda b, i: (b, i, 0))
    full = lambda a: pl.BlockSpec(a.shape, lambda b, i: (0,) * a.ndim)
    return pl.pallas_call(
        functools.partial(_combine_kernel, tt=tt, alpha=alpha),
        grid=(B, nt),
        in_specs=[pl.BlockSpec((TOP_K, tt), lambda b, i: (0, b * nt + i), memory_space=pltpu.SMEM),
                  pl.BlockSpec(memory_space=pl.ANY),
                  row(TOP_K), row(D), row(D),
                  pl.BlockSpec((None, 1, D), lambda b, i: (b, 0, 0)),
                  full(ln_g), full(ln_b), full(ws1), full(ws3), full(ws2)],
        out_specs=row(D),
        out_shape=jax.ShapeDtypeStruct((B, S, D), F32),
        scratch_shapes=[pltpu.VMEM((TOP_K, tt, D), F32), pltpu.SemaphoreType.DMA(())],
        compiler_params=_cp(("arbitrary", "arbitrary"), VMEM_LIMIT),
        name="combine",
    )(pos_kt, ys, w_tk, h2, x1, g2, ln_g, ln_b, ws1, ws3, ws2)


def _arrange_w_in(w):
    z = lambda n: jnp.zeros((w.shape[0], n), w.dtype)
    return jnp.concatenate([w[:, 0:1280], w[:, 1296:1552], w[:, 1552:1680], w[:, 1680:1712],
                            z(96), w[:, 1280:1296], z(112)], axis=1).astype(BF16)


def _head_layout(nope, rope_a, rope_b):
    r = nope.shape[0]
    z = jnp.zeros((r, MLA_HEADS, MLA_HEAD_PAD - MLA_NOPE - MLA_ROPE), nope.dtype)
    return jnp.concatenate([nope, rope_a, rope_b, z], axis=2).reshape(r, MLA_HEADS * MLA_HEAD_PAD)


def _arrange_mla(w_uq, w_uk):
    half = MLA_ROPE // 2
    q = w_uq.reshape(MLA_Q_LORA, MLA_HEADS, MLA_NOPE + MLA_ROPE)
    qn, q1, q2 = q[..., :MLA_NOPE], q[..., MLA_NOPE:MLA_NOPE + half], q[..., MLA_NOPE + half:]
    wq = _head_layout(qn, q1, q2)
    wq_sw = _head_layout(jnp.zeros_like(qn), q2, q1)
    kn = w_uk.reshape(MLA_KV_LORA, MLA_HEADS, MLA_NOPE)
    zk = jnp.zeros((MLA_KV_LORA, MLA_HEADS, half), w_uk.dtype)
    eye = jnp.eye(MLA_ROPE, dtype=w_uk.dtype)
    e1 = jnp.broadcast_to(eye[:, None, :half], (MLA_ROPE, MLA_HEADS, half))
    e2 = jnp.broadcast_to(eye[:, None, half:], (MLA_ROPE, MLA_HEADS, half))
    zn = jnp.zeros((MLA_ROPE, MLA_HEADS, MLA_NOPE), w_uk.dtype)
    pad = jnp.zeros((128 - MLA_ROPE, MLA_HEADS * MLA_HEAD_PAD), w_uk.dtype)
    wk = jnp.concatenate([_head_layout(kn, zk, zk), _head_layout(zn, e1, e2), pad], axis=0)
    wk_sw = jnp.concatenate([_head_layout(jnp.zeros_like(kn), zk, zk), _head_layout(zn, e2, e1), pad],
                            axis=0)
    return wq.astype(BF16), wq_sw.astype(BF16), wk.astype(BF16), wk_sw.astype(BF16)


def _rope_rows():
    half = MLA_ROPE // 2
    inv_freq = ROPE_THETA ** (-jnp.arange(0, MLA_ROPE, 2, dtype=F32) / MLA_ROPE)
    z = lambda n: jnp.zeros((n,), F32)
    frq = jnp.concatenate([z(MLA_NOPE), inv_freq, inv_freq, z(MLA_HEAD_PAD - MLA_NOPE - MLA_ROPE)])
    sgn = jnp.concatenate([z(MLA_NOPE), -jnp.ones((half,), F32), jnp.ones((half,), F32),
                           z(MLA_HEAD_PAD - MLA_NOPE - MLA_ROPE)])
    return frq[None, :], sgn[None, :]


def _block_diag_pool(w_pool):
    G, C, _ = w_pool.shape
    out = jnp.zeros((G * C, G * C), w_pool.dtype)
    for g in range(G):
        out = out.at[g * C:(g + 1) * C, g * C:(g + 1) * C].set(w_pool[g])
    return out.astype(BF16)


def kernel(x, c, positions, w_ada, b_ada, w_in, w_pool, s_pool, conv_w, conv_b, gate_b, gn_w, g_q,
           g_kv, w_uq, w_uk, w_uv, w_out, ln1_g, ln1_b, w_router, e_bias, w1, w3, w2, ws1, ws3, ws2,
           ln2_g, ln2_b):
    B, S, D = x.shape
    depth = w_in.shape[0]
    T = B * S
    H = ML_HEADS
    alpha = float((2 * depth) ** 0.25)
    n_assign = T * TOP_K
    n_blocks = n_assign // EXPERT_BLOCK + N_EXPERTS
    n_rows = n_blocks * EXPERT_BLOCK

    ada = _ada(c, w_ada, b_ada)
    frq, sgn = _rope_rows()
    pos3 = positions.reshape(B, S, 1)
    qk_scale = jnp.concatenate([jnp.ones((1, ML_WIDTH), F32),
                                jnp.full((1, ML_WIDTH), ML_DH ** -0.5, F32)], axis=1)
    ones_col = jnp.concatenate([jnp.ones((B, H, S, 1), BF16),
                                jnp.zeros((B, H, S, 128 - ML_DH - 1), BF16)], axis=-1)

    for l in range(depth):
        sh1, sc1, g1, sh2, sc2, g2 = [a.reshape(B, 1, D) for a in jnp.split(ada[l], 6, axis=-1)]
        u_pool, u_qk, u_v, u_o, u_dq, u_kv, u_gate = _inproj(x, sc1, sh1, _arrange_w_in(w_in[l]))
        y_pool = _pool(u_pool, _block_diag_pool(w_pool[l]), s_pool[l][None, :])

        qk_act = _conv(u_qk, conv_w[l], conv_b[l][None, :], qk_scale)
        heads = lambda a, n: a.reshape(B, S, n, ML_DH).transpose(0, 2, 1, 3)
        gates = u_gate[:, :, :4 * H].reshape(B, S, 4, H)
        y_ml = _mlstm(heads(qk_act, 2 * H),
                      jnp.concatenate([heads(u_v, H), ones_col], axis=-1),
                      heads(u_o, H),
                      gates.transpose(0, 3, 1, 2),
                      gates.transpose(0, 3, 2, 1),
                      gate_b[l].reshape(4, H).T.reshape(H, 1, 4),
                      gate_b[l].reshape(4, H).T.reshape(H, 4, 1),
                      gn_w[l].reshape(H, 1, ML_DH))
        y_ml = y_ml.transpose(0, 2, 1, 3).reshape(B, S, ML_WIDTH)

        wq, wq_sw, wk, wk_sw = _arrange_mla(w_uq[l], w_uk[l])
        qh, kh, vh = _mla_proj(u_dq, u_kv, pos3, frq, sgn, g_q[l][None, :], g_kv[l][None, :],
                               wq, wq_sw, wk, wk_sw, w_uv[l].astype(BF16))
        y_mla = _attn(qh, kh, vh)

        x1, h2, logits_t = _outproj(y_pool, y_ml, y_mla, x, g1, ln1_g[l][None, :], ln1_b[l][None, :],
                                    sc2, sh2, w_out[l].astype(BF16), w_router[l].T.astype(BF16), alpha)
        idx_t, w_t = _route(logits_t, e_bias[l][:, None])
        rank_t, counts = _rank(idx_t)
        counts = counts[:, 0].astype(I32)
        padded = (counts + EXPERT_BLOCK - 1) // EXPERT_BLOCK * EXPERT_BLOCK
        pend = jnp.cumsum(padded)
        pstart = pend - padded
        n_used = (pend[-1:] // EXPERT_BLOCK).astype(I32)
        blk_expert = jnp.minimum(
            jnp.searchsorted(pend, jnp.arange(n_blocks, dtype=I32) * EXPERT_BLOCK, side="right"),
            N_EXPERTS - 1).astype(I32)
        pos_kt = (pstart[idx_t] + rank_t).transpose(1, 0, 2).reshape(TOP_K, T)
        xs = _dispatch(pos_kt, h2.reshape(T, D), n_rows)
        ys = _experts(blk_expert, n_used, xs, w1, w3, w2, l)
        x = _combine(pos_kt, ys, w_t.transpose(0, 2, 1), h2, x1, g2, ln2_g[l][None, :],
                     ln2_b[l][None, :], ws1[l].astype(BF16), ws3[l].astype(BF16),
                     ws2[l].astype(BF16), alpha)
    return x
```

```python
import functools

import jax
import jax.numpy as jnp
import numpy as np
from jax import lax
from jax.experimental import pallas as pl
from jax.experimental.pallas import tpu as pltpu

F32 = jnp.float32
BF16 = jnp.bfloat16
I32 = jnp.int32

D_MODEL = 1024
POOL_WINDOWS = (2, 4, 8, 16)
POOL_GROUP_DIM = 64
POOL_WIDTH = 256
ML_HEADS = 4
ML_DH = 64
ML_WIDTH = 256
ML_CONV = 5
MLA_HEADS = 8
MLA_NOPE = 64
MLA_ROPE = 32
MLA_V = 64
MLA_Q_LORA = 256
MLA_KV_LORA = 128
MLA_HEAD_PAD = 128
ROPE_THETA = 10000.0
N_EXPERTS = 256
TOP_K = 8
N_GROUPS = 8
TOPK_GROUPS = 4
GROUP_SIZE = N_EXPERTS // N_GROUPS
D_EXPERT = 256
ROUTED_SCALE = 2.5
LN_EPS = 1e-5
RMS_EPS = 1e-6
NEG_BIG = -1e30
NEG_INF = float("-inf")

EXPERT_BLOCK = 256
VMEM_LIMIT = 56 * 1024 * 1024


def _cp(sem, vmem=None):
    return pltpu.CompilerParams(dimension_semantics=sem, vmem_limit_bytes=vmem)


def _ln(x):
    mu = jnp.mean(x, axis=-1, keepdims=True)
    xc = x - mu
    var = jnp.mean(xc * xc, axis=-1, keepdims=True)
    return xc * lax.rsqrt(var + LN_EPS)


def _silu(x):
    return x * jax.nn.sigmoid(x)


def _rms(x, g):
    return x * lax.rsqrt(jnp.mean(x * x, axis=-1, keepdims=True) + RMS_EPS) * g


def _dot(a, b):
    return jnp.dot(a, b, preferred_element_type=F32)


def _dot_nt(a, b):
    return lax.dot_general(a, b, (((1,), (1,)), ((), ())), preferred_element_type=F32)


def _dot_tn(a, b):
    return lax.dot_general(a, b, (((0,), (0,)), ((), ())), preferred_element_type=F32)


def _ada_kernel(c_ref, w_ref, b_ref, o_ref):
    ca = _silu(c_ref[...]).astype(BF16)
    o_ref[...] = _dot(ca, w_ref[...].astype(BF16)) + b_ref[...]


def _ada(c, w_ada, b_ada):
    L, D, N = w_ada.shape
    B = c.shape[0]
    tn = 1536
    return pl.pallas_call(
        _ada_kernel,
        grid=(L, N // tn),
        in_specs=[pl.BlockSpec((B, D), lambda l, j: (0, 0)),
                  pl.BlockSpec((None, D, tn), lambda l, j: (l, 0, j)),
                  pl.BlockSpec((None, 1, tn), lambda l, j: (l, 0, j))],
        out_specs=pl.BlockSpec((None, B, tn), lambda l, j: (l, 0, j)),
        out_shape=jax.ShapeDtypeStruct((L, B, N), F32),
        compiler_params=_cp(("parallel", "parallel"), VMEM_LIMIT),
        name="ada",
    )(c, w_ada, b_ada.reshape(L, 1, N))


def _inproj_kernel(x_ref, sc_ref, sh_ref, w_ref, pool_ref, qk_ref, v_ref, o_ref, dq_ref,
                   kv_ref, gate_ref):
    h = (_ln(x_ref[...]) * (1.0 + sc_ref[...]) + sh_ref[...]).astype(BF16)
    pool_ref[...] = _dot(h, w_ref[:, 0:256])
    qk_ref[...] = _dot(h, w_ref[:, 256:768])
    v_ref[...] = _dot(h, w_ref[:, 768:1024]).astype(BF16)
    o_ref[...] = _dot(h, w_ref[:, 1024:1280])
    dq_ref[...] = _dot(h, w_ref[:, 1280:1536])
    kv_ref[...] = _dot(h, w_ref[:, 1536:1792])
    gate_ref[...] = _dot(h, w_ref[:, 1792:1920])


def _inproj(x, sc, sh, w):
    B, S, D = x.shape
    tm = min(512, S)
    widths = (256, 512, 256, 256, 256, 256, 128)
    dtypes = (F32, F32, BF16, F32, F32, F32, F32)
    row = lambda n: pl.BlockSpec((None, tm, n), lambda b, i: (b, i, 0))
    return pl.pallas_call(
        _inproj_kernel,
        grid=(B, S // tm),
        in_specs=[row(D),
                  pl.BlockSpec((None, 1, D), lambda b, i: (b, 0, 0)),
                  pl.BlockSpec((None, 1, D), lambda b, i: (b, 0, 0)),
                  pl.BlockSpec(w.shape, lambda b, i: (0, 0))],
        out_specs=[row(n) for n in widths],
        out_shape=[jax.ShapeDtypeStruct((B, S, n), dt) for n, dt in zip(widths, dtypes)],
        compiler_params=_cp(("parallel", "parallel"), VMEM_LIMIT),
        name="inproj",
    )(x, sc, sh, w)


def _shift_dn(a, k, row):
    return jnp.where(row >= k, pltpu.roll(a, k, 0), 0.0)


def _shift_up(a, k, row):
    n = a.shape[0]
    return jnp.where(row < n - k, pltpu.roll(a, n - k, 0), 0.0)


def _pool_kernel(u_ref, w_ref, s_ref, y_ref):
    x = u_ref[...]
    S, C = x.shape
    row = lax.broadcasted_iota(I32, (S, C), 0)
    lane = lax.broadcasted_iota(I32, (S, C), 1)
    rowf = row.astype(F32)
    p1, f1 = x, x
    p2 = p1 + _shift_dn(p1, 1, row)
    f2 = f1 + _shift_up(f1, 1, row)
    p4 = p2 + _shift_dn(p2, 2, row)
    f4 = f2 + _shift_up(f2, 2, row)
    p8 = p4 + _shift_dn(p4, 4, row)
    f8 = f4 + _shift_up(f4, 4, row)
    wins = []
    for half, p, f in ((1, p1, f1), (2, p2, f2), (4, p4, f4), (8, p8, f8)):
        total = _shift_dn(p, 1, row) + f
        cnt = jnp.minimum(rowf + half, float(S)) - jnp.maximum(rowf - half, 0.0)
        wins.append(total / cnt)
    pooled = jnp.where(lane < 64, wins[0],
                       jnp.where(lane < 128, wins[1], jnp.where(lane < 192, wins[2], wins[3])))
    d = (pooled - x).astype(BF16)
    y_ref[...] = (_dot(d, w_ref[...]) * s_ref[...]).astype(BF16)


def _pool(u_pool, w_bd, s_pool):
    B, S, C = u_pool.shape
    return pl.pallas_call(
        _pool_kernel,
        grid=(B,),
        in_specs=[pl.BlockSpec((None, S, C), lambda b: (b, 0, 0)),
                  pl.BlockSpec((C, C), lambda b: (0, 0)),
                  pl.BlockSpec((1, C), lambda b: (0, 0))],
        out_specs=pl.BlockSpec((None, S, C), lambda b: (b, 0, 0)),
        out_shape=jax.ShapeDtypeStruct((B, S, C), BF16),
        compiler_params=_cp(("parallel",), VMEM_LIMIT),
        name="pool",
    )(u_pool, w_bd, s_pool)


def _conv_kernel(u_ref, w_ref, b_ref, scale_ref, y_ref):
    x = u_ref[...]
    S, C = x.shape
    row = lax.broadcasted_iota(I32, (S, C), 0)
    y = (_shift_dn(x, 2, row) * w_ref[0:1, :] + _shift_dn(x, 1, row) * w_ref[1:2, :]
         + x * w_ref[2:3, :] + _shift_up(x, 1, row) * w_ref[3:4, :]
         + _shift_up(x, 2, row) * w_ref[4:5, :] + b_ref[...])
    y_ref[...] = (_silu(y) * scale_ref[...]).astype(BF16)


def _conv(u_qk, conv_w, conv_b, scale):
    B, S, C = u_qk.shape
    return pl.pallas_call(
        _conv_kernel,
        grid=(B,),
        in_specs=[pl.BlockSpec((None, S, C), lambda b: (b, 0, 0)),
                  pl.BlockSpec((ML_CONV, C), lambda b: (0, 0)),
                  pl.BlockSpec((1, C), lambda b: (0, 0)),
                  pl.BlockSpec((1, C), lambda b: (0, 0))],
        out_specs=pl.BlockSpec((None, S, C), lambda b: (b, 0, 0)),
        out_shape=jax.ShapeDtypeStruct((B, S, C), BF16),
        compiler_params=_cp(("parallel",), VMEM_LIMIT),
        name="mlstm_conv",
    )(u_qk, conv_w, conv_b, scale)


def _log_sigmoid(x):
    return jnp.minimum(x, 0.0) - jnp.log(1.0 + jnp.exp(-jnp.abs(x)))


def _split3(x):
    a = x.astype(BF16)
    r = x - a.astype(F32)
    b = r.astype(BF16)
    c = (r - b.astype(F32)).astype(BF16)
    return a, b, c


def _mlstm_chunk(q, k, vext, b_col, b_row, i_col, i_row, b_last, allowed, c_ext, m_prev):
    logw = jnp.where(allowed, b_col - b_row + i_row, NEG_INF)
    m_inter = b_col + m_prev
    m_t = jnp.maximum(m_inter, jnp.max(logw, axis=1, keepdims=True))
    w_intra = jnp.exp(logw - m_t)
    w_inter = jnp.exp(m_inter - m_t)
    qk = (_dot_nt(q, k) * w_intra).astype(BF16)
    nd = _dot(qk, vext) + w_inter * _dot(q, c_ext.astype(BF16))
    den = jnp.maximum(jnp.abs(nd[:, ML_DH:ML_DH + 1]), jnp.exp(-m_t))
    h = nd[:, :ML_DH] / den
    m_new = jnp.maximum(b_last + m_prev, jnp.max(b_last - b_row + i_row, axis=1, keepdims=True))
    w_k = jnp.exp(b_last - b_col + i_col - m_new)
    decay = jnp.exp(b_last + m_prev - m_new)
    c_new = decay * c_ext + _dot_tn(k, (w_k * vext.astype(F32)).astype(BF16))
    return h, c_new, m_new


def _mlstm_kernel(q_ref, k_ref, v_ref, uo_ref, gc_ref, gr_ref, bc_ref, br_ref, gn_ref, y_ref,
                  hf_ref, hb_ref, *, chunk):
    S = q_ref.shape[0]
    L = chunk
    nc = S // L
    ri = lax.broadcasted_iota(I32, (L, L), 0)
    ci = lax.broadcasted_iota(I32, (L, L), 1)
    lower = ci <= ri
    upper = ci >= ri
    tril = jnp.where(lower, 1.0, 0.0).astype(BF16)
    triu = jnp.where(upper, 1.0, 0.0).astype(BF16)
    bias_c = bc_ref[...]
    bias_r = br_ref[...]

    def cums(r0):
        gc = gc_ref[pl.ds(r0, L), :] + bias_c
        gr = gr_ref[:, pl.ds(r0, L)] + bias_r
        c1, c2, c3 = _split3(_log_sigmoid(gc))
        r1, r2, r3 = _split3(_log_sigmoid(gr))
        pre_c = _dot(tril, c1) + _dot(tril, c2) + _dot(tril, c3)
        suf_c = _dot(triu, c1) + _dot(triu, c2) + _dot(triu, c3)
        pre_r = _dot(r1, triu) + _dot(r2, triu) + _dot(r3, triu)
        suf_r = _dot(r1, tril) + _dot(r2, tril) + _dot(r3, tril)
        return gc, gr, pre_c, suf_c, pre_r, suf_r

    def step(i, carry):
        cf, mf, cb, mb = carry
        rf = pl.multiple_of(i * L, L)
        rb = pl.multiple_of((nc - 1 - i) * L, L)
        gc, gr, pre_c, _, pre_r, _ = cums(rf)
        h, cf, mf = _mlstm_chunk(
            q_ref[pl.ds(rf, L), :], k_ref[pl.ds(rf, L), :], v_ref[pl.ds(rf, L), :],
            pre_c[:, 1:2], pre_r[1:2, :], gc[:, 0:1], gr[0:1, :], pre_r[1:2, L - 1:L],
            lower, cf, mf)
        hf_ref[pl.ds(rf, L), :] = h
        gc, gr, _, suf_c, _, suf_r = cums(rb)
        h, cb, mb = _mlstm_chunk(
            q_ref[pl.ds(rb, L), :], k_ref[pl.ds(rb, L), :], v_ref[pl.ds(rb, L), :],
            suf_c[:, 3:4], suf_r[3:4, :], gc[:, 2:3], gr[2:3, :], suf_r[3:4, 0:1],
            upper, cb, mb)
        hb_ref[pl.ds(rb, L), :] = h
        return cf, mf, cb, mb

    c0 = jnp.zeros((ML_DH, 128), F32)
    m0 = jnp.full((1, 1), NEG_BIG, F32)
    lax.fori_loop(0, nc, step, (c0, m0, c0, m0))
    h = hf_ref[...] + hb_ref[...]
    y_ref[...] = (jax.nn.sigmoid(uo_ref[...]) * (_ln(h) * gn_ref[...])).astype(BF16)


def _mlstm(qk_heads, v_ext, uo_heads, g_col, g_row, b_col, b_row, gn_w):
    B, _, S, dh = qk_heads.shape
    H = ML_HEADS
    L = min(256, S)
    seq = lambda n, off: pl.BlockSpec((None, None, S, n), lambda b, h: (b, h + off, 0, 0))
    return pl.pallas_call(
        functools.partial(_mlstm_kernel, chunk=L),
        grid=(B, H),
        in_specs=[seq(dh, 0), seq(dh, H), seq(128, 0), seq(dh, 0), seq(4, 0),
                  pl.BlockSpec((None, None, 4, S), lambda b, h: (b, h, 0, 0)),
                  pl.BlockSpec((None, 1, 4), lambda b, h: (h, 0, 0)),
                  pl.BlockSpec((None, 4, 1), lambda b, h: (h, 0, 0)),
                  pl.BlockSpec((None, 1, dh), lambda b, h: (h, 0, 0))],
        out_specs=seq(dh, 0),
        out_shape=jax.ShapeDtypeStruct((B, H, S, dh), BF16),
        scratch_shapes=[pltpu.VMEM((S, dh), F32), pltpu.VMEM((S, dh), F32)],
        compiler_params=_cp(("parallel", "parallel"), VMEM_LIMIT),
        name="mlstm_scan",
    )(qk_heads, qk_heads, v_ext, uo_heads, g_col, g_row, b_col, b_row, gn_w)


def _mla_proj_kernel(dq_ref, kv_ref, pos_ref, frq_ref, sgn_ref, gq_ref, gkv_ref, wq_ref, wqs_ref,
                     wk_ref, wks_ref, wv_ref, q_ref, k_ref, v_ref, *, scale):
    ang = pos_ref[...].astype(F32) * frq_ref[...]
    cos = jnp.cos(ang)
    sin = jnp.sin(ang) * sgn_ref[...]
    qn = _rms(dq_ref[...], gq_ref[...]).astype(BF16)
    a = _dot(qn, wq_ref[...])
    a_sw = _dot(qn, wqs_ref[...])
    kv = kv_ref[...]
    ckv = _rms(kv[:, :MLA_KV_LORA], gkv_ref[...])
    kin = jnp.concatenate([ckv, kv[:, MLA_KV_LORA:]], axis=1).astype(BF16)
    ak = _dot(kin, wk_ref[...])
    ak_sw = _dot(kin, wks_ref[...])
    for h in range(MLA_HEADS):
        sl = slice(h * MLA_HEAD_PAD, (h + 1) * MLA_HEAD_PAD)
        q_ref[:, sl] = ((a[:, sl] * cos + a_sw[:, sl] * sin) * scale).astype(BF16)
        k_ref[:, sl] = (ak[:, sl] * cos + ak_sw[:, sl] * sin).astype(BF16)
    v_ref[...] = _dot(ckv.astype(BF16), wv_ref[...]).astype(BF16)


def _mla_proj(u_dq, u_kv, pos, frq, sgn, g_q, g_kv, wq, wqs, wk, wks, wv):
    B, S, _ = u_dq.shape
    tm = min(512, S)
    hw = MLA_HEADS * MLA_HEAD_PAD
    vw = MLA_HEADS * MLA_V
    row = lambda n: pl.BlockSpec((None, tm, n), lambda b, i: (b, i, 0))
    full = lambda a: pl.BlockSpec(a.shape, lambda b, i: (0,) * a.ndim)
    scale = (MLA_NOPE + MLA_ROPE) ** -0.5
    return pl.pallas_call(
        functools.partial(_mla_proj_kernel, scale=scale),
        grid=(B, S // tm),
        in_specs=[row(MLA_Q_LORA), row(256), row(1), full(frq), full(sgn), full(g_q), full(g_kv),
                  full(wq), full(wqs), full(wk), full(wks), full(wv)],
        out_specs=[row(hw), row(hw), row(vw)],
        out_shape=[jax.ShapeDtypeStruct((B, S, hw), BF16),
                   jax.ShapeDtypeStruct((B, S, hw), BF16),
                   jax.ShapeDtypeStruct((B, S, vw), BF16)],
        compiler_params=_cp(("parallel", "parallel"), VMEM_LIMIT),
        name="mla_proj",
    )(u_dq, u_kv, pos, frq, sgn, g_q, g_kv, wq, wqs, wk, wks, wv)


def _attn_kernel(q_ref, k_ref, v_ref, o_ref, *, tk):
    tq = q_ref.shape[0]
    S = k_ref.shape[0]
    nk = S // tk
    outs = []
    for h in range(2):
        q = q_ref[:, h * MLA_HEAD_PAD:(h + 1) * MLA_HEAD_PAD]

        def step(j, carry, q=q, h=h):
            m, l, acc = carry
            r0 = pl.multiple_of(j * tk, tk)
            k = k_ref[pl.ds(r0, tk), h * MLA_HEAD_PAD:(h + 1) * MLA_HEAD_PAD]
            s = _dot_nt(q, k)
            m_new = jnp.maximum(m, jnp.max(s, axis=1, keepdims=True))
            alpha = jnp.exp(m - m_new)
            p = jnp.exp(s - m_new)
            l = alpha * l + jnp.sum(p, axis=1, keepdims=True)
            acc = alpha * acc + _dot(p.astype(BF16), v_ref[pl.ds(r0, tk), :])
            return m_new, l, acc

        m0 = jnp.full((tq, 1), NEG_INF, F32)
        l0 = jnp.zeros((tq, 1), F32)
        a0 = jnp.zeros((tq, 2 * MLA_V), F32)
        _, l, acc = lax.fori_loop(0, nk, step, (m0, l0, a0))
        outs.append(acc / l)
    lane = lax.broadcasted_iota(I32, (tq, 2 * MLA_V), 1)
    o_ref[...] = jnp.where(lane < MLA_V, outs[0], outs[1]).astype(BF16)


def _attn(qh, kh, vh):
    B, S, _ = qh.shape
    tq = min(512, S)
    tk = min(512, S)
    pw = 2 * MLA_HEAD_PAD
    return pl.pallas_call(
        functools.partial(_attn_kernel, tk=tk),
        grid=(B, MLA_HEADS // 2, S // tq),
        in_specs=[pl.BlockSpec((None, tq, pw), lambda b, p, i: (b, i, p)),
                  pl.BlockSpec((None, S, pw), lambda b, p, i: (b, 0, p)),
                  pl.BlockSpec((None, S, 2 * MLA_V), lambda b, p, i: (b, 0, p))],
        out_specs=pl.BlockSpec((None, tq, 2 * MLA_V), lambda b, p, i: (b, i, p)),
        out_shape=jax.ShapeDtypeStruct((B, S, MLA_HEADS * MLA_V), BF16),
        compiler_params=_cp(("parallel", "parallel", "parallel"), VMEM_LIMIT),
        name="mla_attn",
    )(qh, kh, vh)


def _outproj_kernel(yp_ref, ym_ref, ya_ref, x_ref, g1_ref, lg_ref, lb_ref, sc_ref, sh_ref, wo_ref,
                    wr_ref, x1_ref, h2_ref, lt_ref, *, alpha):
    mix = (_dot(yp_ref[...], wo_ref[0:256, :]) + _dot(ym_ref[...], wo_ref[256:512, :])
           + _dot(ya_ref[...], wo_ref[512:1024, :]))
    x1 = _ln(alpha * x_ref[...] + g1_ref[...] * mix) * lg_ref[...] + lb_ref[...]
    x1_ref[...] = x1
    h2 = _ln(x1) * (1.0 + sc_ref[...]) + sh_ref[...]
    h2_ref[...] = h2
    lt_ref[...] = _dot_nt(wr_ref[...], h2.astype(BF16))


def _outproj(y_pool, y_ml, y_mla, x, g1, ln_g, ln_b, sc2, sh2, w_out, w_router_t, alpha):
    B, S, D = x.shape
    tm = min(512, S)
    row = lambda n: pl.BlockSpec((None, tm, n), lambda b, i: (b, i, 0))
    per_b = pl.BlockSpec((None, 1, D), lambda b, i: (b, 0, 0))
    vec = pl.BlockSpec((1, D), lambda b, i: (0, 0))
    return pl.pallas_call(
        functools.partial(_outproj_kernel, alpha=alpha),
        grid=(B, S // tm),
        in_specs=[row(256), row(256), row(512), row(D), per_b, vec, vec, per_b, per_b,
                  pl.BlockSpec((D, D), lambda b, i: (0, 0)),
                  pl.BlockSpec((N_EXPERTS, D), lambda b, i: (0, 0))],
        out_specs=[row(D), row(D), pl.BlockSpec((None, N_EXPERTS, tm), lambda b, i: (b, 0, i))],
        out_shape=[jax.ShapeDtypeStruct((B, S, D), F32), jax.ShapeDtypeStruct((B, S, D), F32),
                   jax.ShapeDtypeStruct((B, N_EXPERTS, S), F32)],
        compiler_params=_cp(("parallel", "parallel"), VMEM_LIMIT),
        name="outproj",
    )(y_pool, y_ml, y_mla, x, g1, ln_g, ln_b, sc2, sh2, w_out, w_router_t)


def _first_max(v, iota, n):
    m = jnp.max(v, axis=0, keepdims=True)
    first = jnp.min(jnp.where(v == m, iota, n), axis=0, keepdims=True)
    return m, first


def _route_kernel(lt_ref, eb_ref, idx_ref, w_ref):
    scores = jax.nn.sigmoid(lt_ref[...])
    sel = scores + eb_ref[...]
    E, tt = sel.shape
    gi = lax.broadcasted_iota(I32, (GROUP_SIZE, tt), 0)
    groups = [sel[g * GROUP_SIZE:(g + 1) * GROUP_SIZE, :] for g in range(N_GROUPS)]
    gs = []
    for grp in groups:
        m1, f1 = _first_max(grp, gi, GROUP_SIZE)
        m2 = jnp.max(jnp.where(gi == f1, NEG_INF, grp), axis=0, keepdims=True)
        gs.append(m1 + m2)
    chosen = [jnp.zeros((1, tt), F32) for _ in range(N_GROUPS)]
    for _ in range(TOPK_GROUPS):
        m = functools.reduce(jnp.maximum, gs)
        first = functools.reduce(
            jnp.minimum, [jnp.where(gs[g] == m, g, N_GROUPS) for g in range(N_GROUPS)])
        for g in range(N_GROUPS):
            hit = first == g
            chosen[g] = jnp.where(hit, 1.0, chosen[g])
            gs[g] = jnp.where(hit, NEG_INF, gs[g])
    cand = jnp.concatenate(
        [jnp.where(chosen[g] > 0.5, groups[g], NEG_INF) for g in range(N_GROUPS)], axis=0)
    ei = lax.broadcasted_iota(I32, (E, tt), 0)
    ws = []
    for k in range(TOP_K):
        _, f = _first_max(cand, ei, E)
        hit = ei == f
        idx_ref[k:k + 1, :] = f
        ws.append(jnp.sum(jnp.where(hit, scores, 0.0), axis=0, keepdims=True))
        cand = jnp.where(hit, NEG_INF, cand)
    norm = ROUTED_SCALE / functools.reduce(jnp.add, ws)
    for k in range(TOP_K):
        w_ref[k:k + 1, :] = ws[k] * norm


def _route(logits_t, e_bias):
    B, E, S = logits_t.shape
    tt = min(512, S)
    return pl.pallas_call(
        _route_kernel,
        grid=(B, S // tt),
        in_specs=[pl.BlockSpec((None, E, tt), lambda b, i: (b, 0, i)),
                  pl.BlockSpec((E, 1), lambda b, i: (0, 0))],
        out_specs=[pl.BlockSpec((None, TOP_K, tt), lambda b, i: (b, 0, i)),
                   pl.BlockSpec((None, TOP_K, tt), lambda b, i: (b, 0, i))],
        out_shape=[jax.ShapeDtypeStruct((B, TOP_K, S), I32),
                   jax.ShapeDtypeStruct((B, TOP_K, S), F32)],
        compiler_params=_cp(("parallel", "parallel"), VMEM_LIMIT),
        name="route",
    )(logits_t, e_bias)


def _rank_kernel(idx_ref, rank_ref, cnt_ref, carry_ref):
    first = jnp.logical_and(pl.program_id(0) == 0, pl.program_id(1) == 0)

    @pl.when(first)
    def _():
        carry_ref[...] = jnp.zeros_like(carry_ref)

    K, tt = idx_ref.shape
    E = carry_ref.shape[0]
    ei = lax.broadcasted_iota(I32, (E, tt), 0)
    ri = lax.broadcasted_iota(I32, (tt, tt), 0)
    ci = lax.broadcasted_iota(I32, (tt, tt), 1)
    before = jnp.where(ri < ci, 1.0, 0.0).astype(BF16)
    base = carry_ref[...]
    for k in range(K):
        hit = ei == idx_ref[k:k + 1, :]
        onehot = jnp.where(hit, 1.0, 0.0)
        prefix = _dot(onehot.astype(BF16), before)
        rank = jnp.sum(jnp.where(hit, base + prefix, 0.0), axis=0, keepdims=True)
        rank_ref[k:k + 1, :] = rank.astype(I32)
        base = base + jnp.sum(onehot, axis=1, keepdims=True)
    carry_ref[...] = base
    cnt_ref[...] = base


def _rank(idx_t):
    B, K, S = idx_t.shape
    tt = min(256, S)
    return pl.pallas_call(
        _rank_kernel,
        grid=(B, S // tt),
        in_specs=[pl.BlockSpec((None, K, tt), lambda b, i: (b, 0, i))],
        out_specs=[pl.BlockSpec((None, K, tt), lambda b, i: (b, 0, i)),
                   pl.BlockSpec((N_EXPERTS, 1), lambda b, i: (0, 0))],
        out_shape=[jax.ShapeDtypeStruct((B, K, S), I32),
                   jax.ShapeDtypeStruct((N_EXPERTS, 1), F32)],
        scratch_shapes=[pltpu.VMEM((N_EXPERTS, 1), F32)],
        compiler_params=_cp(("arbitrary", "arbitrary"), VMEM_LIMIT),
        name="rank",
    )(idx_t)


def _pos_kernel(idx_ref, rank_ref, start_ref, pos_ref):
    K, tt = idx_ref.shape
    ei = lax.broadcasted_iota(I32, (N_EXPERTS, tt), 0)
    start = start_ref[...]
    for k in range(K):
        hit = ei == idx_ref[k:k + 1, :]
        off = jnp.sum(jnp.where(hit, start, 0.0), axis=0, keepdims=True)
        pos_ref[k:k + 1, :] = rank_ref[k:k + 1, :] + off.astype(I32)


def _pos(idx_t, rank_t, pstart):
    B, K, S = idx_t.shape
    tt = min(512, S)
    spec = pl.BlockSpec((None, K, tt), lambda b, i: (b, 0, i))
    return pl.pallas_call(
        _pos_kernel,
        grid=(B, S // tt),
        in_specs=[spec, spec, pl.BlockSpec((N_EXPERTS, 1), lambda b, i: (0, 0))],
        out_specs=spec,
        out_shape=jax.ShapeDtypeStruct((B, K, S), I32),
        compiler_params=_cp(("parallel", "parallel"), VMEM_LIMIT),
        name="slot",
    )(idx_t, rank_t, pstart)


def _dispatch_kernel(pos_ref, h_ref, xs_ref, sem, *, tt):
    def copy(t, k):
        return pltpu.make_async_copy(h_ref.at[pl.ds(t, 1), :],
                                     xs_ref.at[pl.ds(pos_ref[k, t], 1), :], sem)

    def issue(t, c):
        for k in range(TOP_K):
            copy(t, k).start()
        return c

    def drain(t, c):
        for k in range(TOP_K):
            copy(t, k).wait()
        return c

    lax.fori_loop(0, tt, issue, 0)
    lax.fori_loop(0, tt, drain, 0)


def _dispatch(pos_kt, h2_flat, n_rows):
    T, D = h2_flat.shape
    tt = 256
    return pl.pallas_call(
        functools.partial(_dispatch_kernel, tt=tt),
        grid=(T // tt,),
        in_specs=[pl.BlockSpec((TOP_K, tt), lambda i: (0, i), memory_space=pltpu.SMEM),
                  pl.BlockSpec((tt, D), lambda i: (i, 0))],
        out_specs=pl.BlockSpec(memory_space=pl.ANY),
        out_shape=jax.ShapeDtypeStruct((n_rows, D), F32),
        scratch_shapes=[pltpu.SemaphoreType.DMA(())],
        compiler_params=pltpu.CompilerParams(dimension_semantics=("arbitrary",),
                                             has_side_effects=True),
        name="dispatch",
    )(pos_kt, h2_flat)


def _expert_kernel(be_ref, nu_ref, xs_ref, w1_ref, w3_ref, w2_ref, ys_ref, w1b, w3b, w2b):
    b = pl.program_id(0)
    prev = be_ref[jnp.maximum(b - 1, 0)]
    fresh = jnp.logical_or(b == 0, be_ref[b] != prev)

    @pl.when(jnp.logical_and(fresh, b < nu_ref[0]))
    def _():
        w1b[...] = w1_ref[...].astype(BF16)
        w3b[...] = w3_ref[...].astype(BF16)
        w2b[...] = w2_ref[...].astype(BF16)

    @pl.when(b < nu_ref[0])
    def _():
        x = xs_ref[...].astype(BF16)
        hb = (_silu(_dot(x, w1b[...])) * _dot(x, w3b[...])).astype(BF16)
        ys_ref[...] = _dot(hb, w2b[...])


def _experts(blk_expert, n_used, xs, w1, w3, w2, layer):
    P, D = xs.shape
    nb = P // EXPERT_BLOCK
    last = lambda b, be, nu: jnp.minimum(b, nu[0] - 1)
    wspec = lambda r, c: pl.BlockSpec((None, None, r, c),
                                      lambda b, be, nu: (layer, be[last(b, be, nu)], 0, 0))
    grid_spec = pltpu.PrefetchScalarGridSpec(
        num_scalar_prefetch=2,
        grid=(nb,),
        in_specs=[pl.BlockSpec((EXPERT_BLOCK, D), lambda b, be, nu: (last(b, be, nu), 0)),
                  wspec(D, D_EXPERT), wspec(D, D_EXPERT), wspec(D_EXPERT, D)],
        out_specs=pl.BlockSpec((EXPERT_BLOCK, D), lambda b, be, nu: (last(b, be, nu), 0)),
        scratch_shapes=[pltpu.VMEM((D, D_EXPERT), BF16), pltpu.VMEM((D, D_EXPERT), BF16),
                        pltpu.VMEM((D_EXPERT, D), BF16)],
    )
    return pl.pallas_call(
        _expert_kernel,
        grid_spec=grid_spec,
        out_shape=jax.ShapeDtypeStruct((P, D), F32),
        compiler_params=_cp(("arbitrary",), VMEM_LIMIT),
        name="experts",
    )(blk_expert, n_used, xs, w1, w3, w2)


def _combine_kernel(pos_ref, ys_ref, wt_ref, h2_ref, x1_ref, g2_ref, lg_ref, lb_ref, ws1_ref,
                    ws3_ref, ws2_ref, o_ref, buf, sem, *, tt, alpha):
    def copy(t, k):
        return pltpu.make_async_copy(ys_ref.at[pl.ds(pos_ref[k, t], 1), :],
                                     buf.at[k, pl.ds(t, 1), :], sem)

    def issue(t, c):
        for k in range(TOP_K):
            copy(t, k).start()
        return c

    def drain(t, c):
        for k in range(TOP_K):
            copy(t, k).wait()
        return c

    lax.fori_loop(0, tt, issue, 0)
    h = h2_ref[...].astype(BF16)
    ffn = _dot((_silu(_dot(h, ws1_ref[...])) * _dot(h, ws3_ref[...])).astype(BF16), ws2_ref[...])
    lax.fori_loop(0, tt, drain, 0)
    wt = wt_ref[...]
    for k in range(TOP_K):
        ffn = ffn + wt[:, k:k + 1] * buf[k]
    o_ref[...] = _ln(alpha * x1_ref[...] + g2_ref[...] * ffn) * lg_ref[...] + lb_ref[...]


def _combine(pos_kt, ys, w_tk, h2, x1, g2, ln_g, ln_b, ws1, ws3, ws2, alpha):
    B, S, D = x1.shape
    tt = 128
    nt = S // tt
    row = lambda n: pl.BlockSpec((None, tt, n), lambda b, i: (b, i, 0))
    full = lambda a: pl.BlockSpec(a.shape, lambda b, i: (0,) * a.ndim)
    return pl.pallas_call(
        functools.partial(_combine_kernel, tt=tt, alpha=alpha),
        grid=(B, nt),
        in_specs=[pl.BlockSpec((TOP_K, tt), lambda b, i: (0, b * nt + i), memory_space=pltpu.SMEM),
                  pl.BlockSpec(memory_space=pl.ANY),
                  row(TOP_K), row(D), row(D),
                  pl.BlockSpec((None, 1, D), lambda b, i: (b, 0, 0)),
                  full(ln_g), full(ln_b), full(ws1), full(ws3), full(ws2)],
        out_specs=row(D),
        out_shape=jax.ShapeDtypeStruct((B, S, D), F32),
        scratch_shapes=[pltpu.VMEM((TOP_K, tt, D), F32), pltpu.SemaphoreType.DMA(())],
        compiler_params=_cp(("arbitrary", "arbitrary"), VMEM_LIMIT),
        name="combine",
    )(pos_kt, ys, w_tk, h2, x1, g2, ln_g, ln_b, ws1, ws3, ws2)


def _arrange_w_in(w):
    z = lambda n: jnp.zeros((w.shape[0], n), w.dtype)
    return jnp.concatenate([w[:, 0:1280], w[:, 1296:1552], w[:, 1552:1680], w[:, 1680:1712],
                            z(96), w[:, 1280:1296], z(112)], axis=1).astype(BF16)


def _head_layout(nope, rope_a, rope_b):
    r = nope.shape[0]
    z = jnp.zeros((r, MLA_HEADS, MLA_HEAD_PAD - MLA_NOPE - MLA_ROPE), nope.dtype)
    return jnp.concatenate([nope, rope_a, rope_b, z], axis=2).reshape(r, MLA_HEADS * MLA_HEAD_PAD)


def _arrange_mla(w_uq, w_uk):
    half = MLA_ROPE // 2
    q = w_uq.reshape(MLA_Q_LORA, MLA_HEADS, MLA_NOPE + MLA_ROPE)
    qn, q1, q2 = q[..., :MLA_NOPE], q[..., MLA_NOPE:MLA_NOPE + half], q[..., MLA_NOPE + half:]
    wq = _head_layout(qn, q1, q2)
    wq_sw = _head_layout(jnp.zeros_like(qn), q2, q1)
    kn = w_uk.reshape(MLA_KV_LORA, MLA_HEADS, MLA_NOPE)
    zk = jnp.zeros((MLA_KV_LORA, MLA_HEADS, half), w_uk.dtype)
    eye = jnp.eye(MLA_ROPE, dtype=w_uk.dtype)
    e1 = jnp.broadcast_to(eye[:, None, :half], (MLA_ROPE, MLA_HEADS, half))
    e2 = jnp.broadcast_to(eye[:, None, half:], (MLA_ROPE, MLA_HEADS, half))
    zn = jnp.zeros((MLA_ROPE, MLA_HEADS, MLA_NOPE), w_uk.dtype)
    pad = jnp.zeros((128 - MLA_ROPE, MLA_HEADS * MLA_HEAD_PAD), w_uk.dtype)
    wk = jnp.concatenate([_head_layout(kn, zk, zk), _head_layout(zn, e1, e2), pad], axis=0)
    wk_sw = jnp.concatenate([_head_layout(jnp.zeros_like(kn), zk, zk), _head_layout(zn, e2, e1), pad],
                            axis=0)
    return wq.astype(BF16), wq_sw.astype(BF16), wk.astype(BF16), wk_sw.astype(BF16)


def _rope_rows():
    half = MLA_ROPE // 2
    inv_freq = ROPE_THETA ** (-jnp.arange(0, MLA_ROPE, 2, dtype=F32) / MLA_ROPE)
    z = lambda n: jnp.zeros((n,), F32)
    frq = jnp.concatenate([z(MLA_NOPE), inv_freq, inv_freq, z(MLA_HEAD_PAD - MLA_NOPE - MLA_ROPE)])
    sgn = jnp.concatenate([z(MLA_NOPE), -jnp.ones((half,), F32), jnp.ones((half,), F32),
                           z(MLA_HEAD_PAD - MLA_NOPE - MLA_ROPE)])
    return frq[None, :], sgn[None, :]


def _block_diag_pool(w_pool):
    G, C, _ = w_pool.shape
    out = jnp.zeros((G * C, G * C), w_pool.dtype)
    for g in range(G):
        out = out.at[g * C:(g + 1) * C, g * C:(g + 1) * C].set(w_pool[g])
    return out.astype(BF16)


def kernel(x, c, positions, w_ada, b_ada, w_in, w_pool, s_pool, conv_w, conv_b, gate_b, gn_w, g_q,
           g_kv, w_uq, w_uk, w_uv, w_out, ln1_g, ln1_b, w_router, e_bias, w1, w3, w2, ws1, ws3, ws2,
           ln2_g, ln2_b):
    B, S, D = x.shape
    depth = w_in.shape[0]
    T = B * S
    H = ML_HEADS
    alpha = float((2 * depth) ** 0.25)
    n_assign = T * TOP_K
    n_blocks = n_assign // EXPERT_BLOCK + N_EXPERTS
    n_rows = n_blocks * EXPERT_BLOCK

    ada = _ada(c, w_ada, b_ada)
    frq, sgn = _rope_rows()
    pos3 = positions.reshape(B, S, 1)
    qk_scale = jnp.concatenate([jnp.ones((1, ML_WIDTH), F32),
                                jnp.full((1, ML_WIDTH), ML_DH ** -0.5, F32)], axis=1)
    ones_col = jnp.concatenate([jnp.ones((B, H, S, 1), BF16),
                                jnp.zeros((B, H, S, 128 - ML_DH - 1), BF16)], axis=-1)

    for l in range(depth):
        sh1, sc1, g1, sh2, sc2, g2 = [a.reshape(B, 1, D) for a in jnp.split(ada[l], 6, axis=-1)]
        u_pool, u_qk, u_v, u_o, u_dq, u_kv, u_gate = _inproj(x, sc1, sh1, _arrange_w_in(w_in[l]))
        y_pool = _pool(u_pool, _block_diag_pool(w_pool[l]), s_pool[l][None, :])

        qk_act = _conv(u_qk, conv_w[l], conv_b[l][None, :], qk_scale)
        heads = lambda a, n: a.reshape(B, S, n, ML_DH).transpose(0, 2, 1, 3)
        gates = u_gate[:, :, :4 * H].reshape(B, S, 4, H)
        y_ml = _mlstm(heads(qk_act, 2 * H),
                      jnp.concatenate([heads(u_v, H), ones_col], axis=-1),
                      heads(u_o, H),
                      gates.transpose(0, 3, 1, 2),
                      gates.transpose(0, 3, 2, 1),
                      gate_b[l].reshape(4, H).T.reshape(H, 1, 4),
                      gate_b[l].reshape(4, H).T.reshape(H, 4, 1),
                      gn_w[l].reshape(H, 1, ML_DH))
        y_ml = y_ml.transpose(0, 2, 1, 3).reshape(B, S, ML_WIDTH)

        wq, wq_sw, wk, wk_sw = _arrange_mla(w_uq[l], w_uk[l])
        qh, kh, vh = _mla_proj(u_dq, u_kv, pos3, frq, sgn, g_q[l][None, :], g_kv[l][None, :],
                               wq, wq_sw, wk, wk_sw, w_uv[l].astype(BF16))
        y_mla = _attn(qh, kh, vh)

        x1, h2, logits_t = _outproj(y_pool, y_ml, y_mla, x, g1, ln1_g[l][None, :], ln1_b[l][None, :],
                                    sc2, sh2, w_out[l].astype(BF16), w_router[l].T.astype(BF16), alpha)
        idx_t, w_t = _route(logits_t, e_bias[l][:, None])
        rank_t, counts = _rank(idx_t)
        counts = counts[:, 0].astype(I32)
        padded = (counts + EXPERT_BLOCK - 1) // EXPERT_BLOCK * EXPERT_BLOCK
        pend = jnp.cumsum(padded)
        pstart = pend - padded
        n_used = (pend[-1:] // EXPERT_BLOCK).astype(I32)
        blk_expert = jnp.minimum(
            jnp.searchsorted(pend, jnp.arange(n_blocks, dtype=I32) * EXPERT_BLOCK, side="right"),
            N_EXPERTS - 1).astype(I32)
        pos_kt = _pos(idx_t, rank_t, pstart.astype(F32)[:, None]).transpose(1, 0, 2).reshape(TOP_K, T)
        xs = _dispatch(pos_kt, h2.reshape(T, D), n_rows)
        ys = _experts(blk_expert, n_used, xs, w1, w3, w2, l)
        x = _combine(pos_kt, ys, w_t.transpose(0, 2, 1), h2, x1, g2, ln2_g[l][None, :],
                     ln2_b[l][None, :], ws1[l].astype(BF16), ws3[l].astype(BF16),
                     ws2[l].astype(BF16), alpha)
    return x
```

```python
import functools

import jax
import jax.numpy as jnp
import numpy as np
from jax import lax
from jax.experimental import pallas as pl
from jax.experimental.pallas import tpu as pltpu

F32 = jnp.float32
BF16 = jnp.bfloat16
I32 = jnp.int32

D_MODEL = 1024
POOL_WINDOWS = (2, 4, 8, 16)
POOL_GROUP_DIM = 64
POOL_WIDTH = 256
ML_HEADS = 4
ML_DH = 64
ML_WIDTH = 256
ML_CONV = 5
MLA_HEADS = 8
MLA_NOPE = 64
MLA_ROPE = 32
MLA_V = 64
MLA_Q_LORA = 256
MLA_KV_LORA = 128
MLA_HEAD_PAD = 128
ROPE_THETA = 10000.0
N_EXPERTS = 256
TOP_K = 8
N_GROUPS = 8
TOPK_GROUPS = 4
GROUP_SIZE = N_EXPERTS // N_GROUPS
D_EXPERT = 256
ROUTED_SCALE = 2.5
LN_EPS = 1e-5
RMS_EPS = 1e-6
NEG_BIG = -1e30
NEG_INF = float("-inf")

EXPERT_BLOCK = 256
VMEM_LIMIT = 56 * 1024 * 1024


def _cp(sem, vmem=None):
    return pltpu.CompilerParams(dimension_semantics=sem, vmem_limit_bytes=vmem)


def _ln(x):
    mu = jnp.mean(x, axis=-1, keepdims=True)
    xc = x - mu
    var = jnp.mean(xc * xc, axis=-1, keepdims=True)
    return xc * lax.rsqrt(var + LN_EPS)


def _silu(x):
    return x * jax.nn.sigmoid(x)


def _rms(x, g):
    return x * lax.rsqrt(jnp.mean(x * x, axis=-1, keepdims=True) + RMS_EPS) * g


def _dot(a, b):
    return jnp.dot(a, b, preferred_element_type=F32)


def _dot_nt(a, b):
    return lax.dot_general(a, b, (((1,), (1,)), ((), ())), preferred_element_type=F32)


def _dot_tn(a, b):
    return lax.dot_general(a, b, (((0,), (0,)), ((), ())), preferred_element_type=F32)


def _pack_pair(lo, hi):
    lo_b = lax.bitcast_convert_type(lo.astype(BF16).astype(F32), I32)
    hi_b = lax.bitcast_convert_type(hi.astype(BF16).astype(F32), I32)
    return jnp.bitwise_or(hi_b, lax.shift_right_logical(lo_b, 16))


def _unpack_pair(w):
    lo = lax.bitcast_convert_type(lax.shift_left(w, 16), F32)
    hi = lax.bitcast_convert_type(jnp.bitwise_and(w, -65536), F32)
    return lo, hi


def _ada_kernel(c_ref, w_ref, b_ref, o_ref):
    ca = _silu(c_ref[...]).astype(BF16)
    o_ref[...] = _dot(ca, w_ref[...].astype(BF16)) + b_ref[...]


def _ada(c, w_ada, b_ada):
    L, D, N = w_ada.shape
    B = c.shape[0]
    tn = 1536
    return pl.pallas_call(
        _ada_kernel,
        grid=(L, N // tn),
        in_specs=[pl.BlockSpec((B, D), lambda l, j: (0, 0)),
                  pl.BlockSpec((None, D, tn), lambda l, j: (l, 0, j)),
                  pl.BlockSpec((None, 1, tn), lambda l, j: (l, 0, j))],
        out_specs=pl.BlockSpec((None, B, tn), lambda l, j: (l, 0, j)),
        out_shape=jax.ShapeDtypeStruct((L, B, N), F32),
        compiler_params=_cp(("parallel", "parallel"), VMEM_LIMIT),
        name="ada",
    )(c, w_ada, b_ada.reshape(L, 1, N))


def _inproj_kernel(x_ref, sc_ref, sh_ref, w_ref, pool_ref, qk_ref, v_ref, o_ref, dq_ref,
                   kv_ref, gate_ref):
    h = (_ln(x_ref[...]) * (1.0 + sc_ref[...]) + sh_ref[...]).astype(BF16)
    pool_ref[...] = _dot(h, w_ref[:, 0:256])
    qk_ref[...] = _dot(h, w_ref[:, 256:768])
    v_ref[...] = _dot(h, w_ref[:, 768:1024]).astype(BF16)
    o_ref[...] = _dot(h, w_ref[:, 1024:1280])
    dq_ref[...] = _dot(h, w_ref[:, 1280:1536])
    kv_ref[...] = _dot(h, w_ref[:, 1536:1792])
    gate_ref[...] = _dot(h, w_ref[:, 1792:1920])


def _inproj(x, sc, sh, w):
    B, S, D = x.shape
    tm = min(512, S)
    widths = (256, 512, 256, 256, 256, 256, 128)
    dtypes = (F32, F32, BF16, F32, F32, F32, F32)
    row = lambda n: pl.BlockSpec((None, tm, n), lambda b, i: (b, i, 0))
    return pl.pallas_call(
        _inproj_kernel,
        grid=(B, S // tm),
        in_specs=[row(D),
                  pl.BlockSpec((None, 1, D), lambda b, i: (b, 0, 0)),
                  pl.BlockSpec((None, 1, D), lambda b, i: (b, 0, 0)),
                  pl.BlockSpec(w.shape, lambda b, i: (0, 0))],
        out_specs=[row(n) for n in widths],
        out_shape=[jax.ShapeDtypeStruct((B, S, n), dt) for n, dt in zip(widths, dtypes)],
        compiler_params=_cp(("parallel", "parallel"), VMEM_LIMIT),
        name="inproj",
    )(x, sc, sh, w)


def _shift_dn(a, k, row):
    return jnp.where(row >= k, pltpu.roll(a, k, 0), 0.0)


def _shift_up(a, k, row):
    n = a.shape[0]
    return jnp.where(row < n - k, pltpu.roll(a, n - k, 0), 0.0)


def _pool_kernel(u_ref, w_ref, s_ref, y_ref):
    x = u_ref[...]
    S, C = x.shape
    row = lax.broadcasted_iota(I32, (S, C), 0)
    lane = lax.broadcasted_iota(I32, (S, C), 1)
    rowf = row.astype(F32)
    p1, f1 = x, x
    p2 = p1 + _shift_dn(p1, 1, row)
    f2 = f1 + _shift_up(f1, 1, row)
    p4 = p2 + _shift_dn(p2, 2, row)
    f4 = f2 + _shift_up(f2, 2, row)
    p8 = p4 + _shift_dn(p4, 4, row)
    f8 = f4 + _shift_up(f4, 4, row)
    wins = []
    for half, p, f in ((1, p1, f1), (2, p2, f2), (4, p4, f4), (8, p8, f8)):
        total = _shift_dn(p, 1, row) + f
        cnt = jnp.minimum(rowf + half, float(S)) - jnp.maximum(rowf - half, 0.0)
        wins.append(total / cnt)
    pooled = jnp.where(lane < 64, wins[0],
                       jnp.where(lane < 128, wins[1], jnp.where(lane < 192, wins[2], wins[3])))
    d = (pooled - x).astype(BF16)
    y_ref[...] = (_dot(d, w_ref[...]) * s_ref[...]).astype(BF16)


def _pool(u_pool, w_bd, s_pool):
    B, S, C = u_pool.shape
    return pl.pallas_call(
        _pool_kernel,
        grid=(B,),
        in_specs=[pl.BlockSpec((None, S, C), lambda b: (b, 0, 0)),
                  pl.BlockSpec((C, C), lambda b: (0, 0)),
                  pl.BlockSpec((1, C), lambda b: (0, 0))],
        out_specs=pl.BlockSpec((None, S, C), lambda b: (b, 0, 0)),
        out_shape=jax.ShapeDtypeStruct((B, S, C), BF16),
        compiler_params=_cp(("parallel",), VMEM_LIMIT),
        name="pool",
    )(u_pool, w_bd, s_pool)


def _conv_kernel(u_ref, w_ref, b_ref, scale_ref, y_ref):
    x = u_ref[...]
    S, C = x.shape
    row = lax.broadcasted_iota(I32, (S, C), 0)
    y = (_shift_dn(x, 2, row) * w_ref[0:1, :] + _shift_dn(x, 1, row) * w_ref[1:2, :]
         + x * w_ref[2:3, :] + _shift_up(x, 1, row) * w_ref[3:4, :]
         + _shift_up(x, 2, row) * w_ref[4:5, :] + b_ref[...])
    y_ref[...] = (_silu(y) * scale_ref[...]).astype(BF16)


def _conv(u_qk, conv_w, conv_b, scale):
    B, S, C = u_qk.shape
    return pl.pallas_call(
        _conv_kernel,
        grid=(B,),
        in_specs=[pl.BlockSpec((None, S, C), lambda b: (b, 0, 0)),
                  pl.BlockSpec((ML_CONV, C), lambda b: (0, 0)),
                  pl.BlockSpec((1, C), lambda b: (0, 0)),
                  pl.BlockSpec((1, C), lambda b: (0, 0))],
        out_specs=pl.BlockSpec((None, S, C), lambda b: (b, 0, 0)),
        out_shape=jax.ShapeDtypeStruct((B, S, C), BF16),
        compiler_params=_cp(("parallel",), VMEM_LIMIT),
        name="mlstm_conv",
    )(u_qk, conv_w, conv_b, scale)


def _log_sigmoid(x):
    return jnp.minimum(x, 0.0) - jnp.log(1.0 + jnp.exp(-jnp.abs(x)))


def _split3(x):
    a = x.astype(BF16)
    r = x - a.astype(F32)
    b = r.astype(BF16)
    c = (r - b.astype(F32)).astype(BF16)
    return a, b, c


def _mlstm_chunk(q, k, vext, b_col, b_row, i_col, i_row, b_last, allowed, c_ext, m_prev):
    logw = jnp.where(allowed, b_col - b_row + i_row, NEG_INF)
    m_inter = b_col + m_prev
    m_t = jnp.maximum(m_inter, jnp.max(logw, axis=1, keepdims=True))
    w_intra = jnp.exp(logw - m_t)
    w_inter = jnp.exp(m_inter - m_t)
    qk = (_dot_nt(q, k) * w_intra).astype(BF16)
    nd = _dot(qk, vext) + w_inter * _dot(q, c_ext.astype(BF16))
    den = jnp.maximum(jnp.abs(nd[:, ML_DH:ML_DH + 1]), jnp.exp(-m_t))
    h = nd[:, :ML_DH] / den
    m_new = jnp.maximum(b_last + m_prev, jnp.max(b_last - b_row + i_row, axis=1, keepdims=True))
    w_k = jnp.exp(b_last - b_col + i_col - m_new)
    decay = jnp.exp(b_last + m_prev - m_new)
    c_new = decay * c_ext + _dot_tn(k, (w_k * vext.astype(F32)).astype(BF16))
    return h, c_new, m_new


def _mlstm_kernel(q_ref, k_ref, v_ref, uo_ref, gc_ref, gr_ref, bc_ref, br_ref, gn_ref, y_ref,
                  hf_ref, hb_ref, *, chunk):
    S = q_ref.shape[0]
    L = chunk
    nc = S // L
    ri = lax.broadcasted_iota(I32, (L, L), 0)
    ci = lax.broadcasted_iota(I32, (L, L), 1)
    lower = ci <= ri
    upper = ci >= ri
    tril = jnp.where(lower, 1.0, 0.0).astype(BF16)
    triu = jnp.where(upper, 1.0, 0.0).astype(BF16)
    bias_c = bc_ref[...]
    bias_r = br_ref[...]

    def cums(r0):
        gc = gc_ref[pl.ds(r0, L), :] + bias_c
        gr = gr_ref[:, pl.ds(r0, L)] + bias_r
        c1, c2, c3 = _split3(_log_sigmoid(gc))
        r1, r2, r3 = _split3(_log_sigmoid(gr))
        pre_c = _dot(tril, c1) + _dot(tril, c2) + _dot(tril, c3)
        suf_c = _dot(triu, c1) + _dot(triu, c2) + _dot(triu, c3)
        pre_r = _dot(r1, triu) + _dot(r2, triu) + _dot(r3, triu)
        suf_r = _dot(r1, tril) + _dot(r2, tril) + _dot(r3, tril)
        return gc, gr, pre_c, suf_c, pre_r, suf_r

    def step(i, carry):
        cf, mf, cb, mb = carry
        rf = pl.multiple_of(i * L, L)
        rb = pl.multiple_of((nc - 1 - i) * L, L)
        gc, gr, pre_c, _, pre_r, _ = cums(rf)
        h, cf, mf = _mlstm_chunk(
            q_ref[pl.ds(rf, L), :], k_ref[pl.ds(rf, L), :], v_ref[pl.ds(rf, L), :],
            pre_c[:, 1:2], pre_r[1:2, :], gc[:, 0:1], gr[0:1, :], pre_r[1:2, L - 1:L],
            lower, cf, mf)
        hf_ref[pl.ds(rf, L), :] = h
        gc, gr, _, suf_c, _, suf_r = cums(rb)
        h, cb, mb = _mlstm_chunk(
            q_ref[pl.ds(rb, L), :], k_ref[pl.ds(rb, L), :], v_ref[pl.ds(rb, L), :],
            suf_c[:, 3:4], suf_r[3:4, :], gc[:, 2:3], gr[2:3, :], suf_r[3:4, 0:1],
            upper, cb, mb)
        hb_ref[pl.ds(rb, L), :] = h
        return cf, mf, cb, mb

    c0 = jnp.zeros((ML_DH, 128), F32)
    m0 = jnp.full((1, 1), NEG_BIG, F32)
    lax.fori_loop(0, nc, step, (c0, m0, c0, m0))
    h = hf_ref[...] + hb_ref[...]
    y_ref[...] = (jax.nn.sigmoid(uo_ref[...]) * (_ln(h) * gn_ref[...])).astype(BF16)


def _mlstm(qk_heads, v_ext, uo_heads, g_col, g_row, b_col, b_row, gn_w):
    B, _, S, dh = qk_heads.shape
    H = ML_HEADS
    L = min(256, S)
    seq = lambda n, off: pl.BlockSpec((None, None, S, n), lambda b, h: (b, h + off, 0, 0))
    return pl.pallas_call(
        functools.partial(_mlstm_kernel, chunk=L),
        grid=(B, H),
        in_specs=[seq(dh, 0), seq(dh, H), seq(128, 0), seq(dh, 0), seq(4, 0),
                  pl.BlockSpec((None, None, 4, S), lambda b, h: (b, h, 0, 0)),
                  pl.BlockSpec((None, 1, 4), lambda b, h: (h, 0, 0)),
                  pl.BlockSpec((None, 4, 1), lambda b, h: (h, 0, 0)),
                  pl.BlockSpec((None, 1, dh), lambda b, h: (h, 0, 0))],
        out_specs=seq(dh, 0),
        out_shape=jax.ShapeDtypeStruct((B, H, S, dh), BF16),
        scratch_shapes=[pltpu.VMEM((S, dh), F32), pltpu.VMEM((S, dh), F32)],
        compiler_params=_cp(("parallel", "parallel"), VMEM_LIMIT),
        name="mlstm_scan",
    )(qk_heads, qk_heads, v_ext, uo_heads, g_col, g_row, b_col, b_row, gn_w)


def _mla_proj_kernel(dq_ref, kv_ref, pos_ref, frq_ref, sgn_ref, gq_ref, gkv_ref, wq_ref, wqs_ref,
                     wk_ref, wks_ref, wv_ref, q_ref, k_ref, v_ref, *, scale):
    ang = pos_ref[...].astype(F32) * frq_ref[...]
    cos = jnp.cos(ang)
    sin = jnp.sin(ang) * sgn_ref[...]
    qn = _rms(dq_ref[...], gq_ref[...]).astype(BF16)
    a = _dot(qn, wq_ref[...])
    a_sw = _dot(qn, wqs_ref[...])
    kv = kv_ref[...]
    ckv = _rms(kv[:, :MLA_KV_LORA], gkv_ref[...])
    kin = jnp.concatenate([ckv, kv[:, MLA_KV_LORA:]], axis=1).astype(BF16)
    ak = _dot(kin, wk_ref[...])
    ak_sw = _dot(kin, wks_ref[...])
    for h in range(MLA_HEADS):
        sl = slice(h * MLA_HEAD_PAD, (h + 1) * MLA_HEAD_PAD)
        q_ref[:, sl] = ((a[:, sl] * cos + a_sw[:, sl] * sin) * scale).astype(BF16)
        k_ref[:, sl] = (ak[:, sl] * cos + ak_sw[:, sl] * sin).astype(BF16)
    v_ref[...] = _dot(ckv.astype(BF16), wv_ref[...]).astype(BF16)


def _mla_proj(u_dq, u_kv, pos, frq, sgn, g_q, g_kv, wq, wqs, wk, wks, wv):
    B, S, _ = u_dq.shape
    tm = min(512, S)
    hw = MLA_HEADS * MLA_HEAD_PAD
    vw = MLA_HEADS * MLA_V
    row = lambda n: pl.BlockSpec((None, tm, n), lambda b, i: (b, i, 0))
    full = lambda a: pl.BlockSpec(a.shape, lambda b, i: (0,) * a.ndim)
    scale = (MLA_NOPE + MLA_ROPE) ** -0.5 * float(np.log2(np.e))
    return pl.pallas_call(
        functools.partial(_mla_proj_kernel, scale=scale),
        grid=(B, S // tm),
        in_specs=[row(MLA_Q_LORA), row(256), row(1), full(frq), full(sgn), full(g_q), full(g_kv),
                  full(wq), full(wqs), full(wk), full(wks), full(wv)],
        out_specs=[row(hw), row(hw), row(vw)],
        out_shape=[jax.ShapeDtypeStruct((B, S, hw), BF16),
                   jax.ShapeDtypeStruct((B, S, hw), BF16),
                   jax.ShapeDtypeStruct((B, S, vw), BF16)],
        compiler_params=_cp(("parallel", "parallel"), VMEM_LIMIT),
        name="mla_proj",
    )(u_dq, u_kv, pos, frq, sgn, g_q, g_kv, wq, wqs, wk, wks, wv)


def _attn_kernel(q_ref, k_ref, vt_ref, o_ref, *, tk):
    tq = q_ref.shape[0]
    S = k_ref.shape[0]
    nk = S // tk
    def scores(h, j):
        k = k_ref[j * tk:(j + 1) * tk, h * MLA_HEAD_PAD:(h + 1) * MLA_HEAD_PAD]
        return _dot_nt(k, q_ref[:, h * MLA_HEAD_PAD:(h + 1) * MLA_HEAD_PAD])

    m = [jnp.full((1, tq), NEG_INF, F32)] * 2
    l = [jnp.zeros((1, tq), F32)] * 2
    acc = [jnp.zeros((MLA_V, tq), F32)] * 2
    s = [scores(0, 0), scores(1, 0)]
    for j in range(nk):
        for h in range(2):
            s_cur = s[h]
            if j + 1 < nk:
                s[h] = scores(h, j + 1)
            m_new = jnp.maximum(m[h], jnp.max(s_cur, axis=0, keepdims=True))
            alpha = jnp.exp2(m[h] - m_new)
            p = jnp.exp2(s_cur - m_new)
            l[h] = alpha * l[h] + jnp.sum(p, axis=0, keepdims=True)
            vt = vt_ref[h * MLA_V:(h + 1) * MLA_V, j * tk:(j + 1) * tk]
            acc[h] = alpha * acc[h] + _dot(vt, p.astype(BF16))
            m[h] = m_new
    for h in range(2):
        o_ref[h * MLA_V:(h + 1) * MLA_V, :] = (acc[h] / l[h]).astype(BF16)


def _attn(qh, kh, vt):
    B, S, _ = qh.shape
    tq = min(512, S)
    tk = min(512, S)
    pw = 2 * MLA_HEAD_PAD
    return pl.pallas_call(
        functools.partial(_attn_kernel, tk=tk),
        grid=(B, MLA_HEADS // 2, S // tq),
        in_specs=[pl.BlockSpec((None, tq, pw), lambda b, p, i: (b, i, p)),
                  pl.BlockSpec((None, S, pw), lambda b, p, i: (b, 0, p)),
                  pl.BlockSpec((None, 2 * MLA_V, S), lambda b, p, i: (b, p, 0))],
        out_specs=pl.BlockSpec((None, 2 * MLA_V, tq), lambda b, p, i: (b, p, i)),
        out_shape=jax.ShapeDtypeStruct((B, MLA_HEADS * MLA_V, S), BF16),
        compiler_params=_cp(("parallel", "parallel", "parallel"), VMEM_LIMIT),
        name="mla_attn",
    )(qh, kh, vt)


def _outproj_kernel(yp_ref, ym_ref, ya_ref, x_ref, g1_ref, lg_ref, lb_ref, sc_ref, sh_ref, wo_ref,
                    wr_ref, x1_ref, h2_ref, lt_ref, *, alpha):
    mix = (_dot(yp_ref[...], wo_ref[0:256, :]) + _dot(ym_ref[...], wo_ref[256:512, :])
           + _dot(ya_ref[...], wo_ref[512:1024, :]))
    x1 = _ln(alpha * x_ref[...] + g1_ref[...] * mix) * lg_ref[...] + lb_ref[...]
    x1_ref[...] = x1
    h2 = _ln(x1) * (1.0 + sc_ref[...]) + sh_ref[...]
    half = h2.shape[1] // 2
    h2_ref[...] = _pack_pair(h2[:, :half], h2[:, half:])
    lt_ref[...] = _dot_nt(wr_ref[...], h2.astype(BF16))


def _outproj(y_pool, y_ml, y_mla, x, g1, ln_g, ln_b, sc2, sh2, w_out, w_router_t, alpha):
    B, S, D = x.shape
    tm = min(512, S)
    row = lambda n: pl.BlockSpec((None, tm, n), lambda b, i: (b, i, 0))
    per_b = pl.BlockSpec((None, 1, D), lambda b, i: (b, 0, 0))
    vec = pl.BlockSpec((1, D), lambda b, i: (0, 0))
    return pl.pallas_call(
        functools.partial(_outproj_kernel, alpha=alpha),
        grid=(B, S // tm),
        in_specs=[row(256), row(256), row(512), row(D), per_b, vec, vec, per_b, per_b,
                  pl.BlockSpec((D, D), lambda b, i: (0, 0)),
                  pl.BlockSpec((N_EXPERTS, D), lambda b, i: (0, 0))],
        out_specs=[row(D), row(D // 2), pl.BlockSpec((None, N_EXPERTS, tm), lambda b, i: (b, 0, i))],
        out_shape=[jax.ShapeDtypeStruct((B, S, D), F32), jax.ShapeDtypeStruct((B, S, D // 2), I32),
                   jax.ShapeDtypeStruct((B, N_EXPERTS, S), F32)],
        compiler_params=_cp(("parallel", "parallel"), VMEM_LIMIT),
        name="outproj",
    )(y_pool, y_ml, y_mla, x, g1, ln_g, ln_b, sc2, sh2, w_out, w_router_t)


def _first_max(v, iota, n):
    m = jnp.max(v, axis=0, keepdims=True)
    first = jnp.min(jnp.where(v == m, iota, n), axis=0, keepdims=True)
    return m, first


def _route_kernel(lt_ref, eb_ref, idx_ref, w_ref):
    scores = jax.nn.sigmoid(lt_ref[...])
    sel = scores + eb_ref[...]
    E, tt = sel.shape
    gi = lax.broadcasted_iota(I32, (GROUP_SIZE, tt), 0)
    groups = [sel[g * GROUP_SIZE:(g + 1) * GROUP_SIZE, :] for g in range(N_GROUPS)]
    gs = []
    for grp in groups:
        m1, f1 = _first_max(grp, gi, GROUP_SIZE)
        m2 = jnp.max(jnp.where(gi == f1, NEG_INF, grp), axis=0, keepdims=True)
        gs.append(m1 + m2)
    chosen = [jnp.zeros((1, tt), F32) for _ in range(N_GROUPS)]
    for _ in range(TOPK_GROUPS):
        m = functools.reduce(jnp.maximum, gs)
        first = functools.reduce(
            jnp.minimum, [jnp.where(gs[g] == m, g, N_GROUPS) for g in range(N_GROUPS)])
        for g in range(N_GROUPS):
            hit = first == g
            chosen[g] = jnp.where(hit, 1.0, chosen[g])
            gs[g] = jnp.where(hit, NEG_INF, gs[g])
    cand = jnp.concatenate(
        [jnp.where(chosen[g] > 0.5, groups[g], NEG_INF) for g in range(N_GROUPS)], axis=0)
    ei = lax.broadcasted_iota(I32, (E, tt), 0)
    ws = []
    for k in range(TOP_K):
        _, f = _first_max(cand, ei, E)
        hit = ei == f
        idx_ref[k:k + 1, :] = f
        ws.append(jnp.sum(jnp.where(hit, scores, 0.0), axis=0, keepdims=True))
        cand = jnp.where(hit, NEG_INF, cand)
    norm = ROUTED_SCALE / functools.reduce(jnp.add, ws)
    for k in range(TOP_K):
        w_ref[k:k + 1, :] = ws[k] * norm


def _route(logits_t, e_bias):
    B, E, S = logits_t.shape
    tt = min(512, S)
    return pl.pallas_call(
        _route_kernel,
        grid=(B, S // tt),
        in_specs=[pl.BlockSpec((None, E, tt), lambda b, i: (b, 0, i)),
                  pl.BlockSpec((E, 1), lambda b, i: (0, 0))],
        out_specs=[pl.BlockSpec((None, TOP_K, tt), lambda b, i: (b, 0, i)),
                   pl.BlockSpec((None, TOP_K, tt), lambda b, i: (b, 0, i))],
        out_shape=[jax.ShapeDtypeStruct((B, TOP_K, S), I32),
                   jax.ShapeDtypeStruct((B, TOP_K, S), F32)],
        compiler_params=_cp(("parallel", "parallel"), VMEM_LIMIT),
        name="route",
    )(logits_t, e_bias)


def _rank_kernel(idx_ref, rank_ref, cnt_ref, carry_ref):
    first = jnp.logical_and(pl.program_id(0) == 0, pl.program_id(1) == 0)

    @pl.when(first)
    def _():
        carry_ref[...] = jnp.zeros_like(carry_ref)

    K, tt = idx_ref.shape
    E = carry_ref.shape[0]
    ei = lax.broadcasted_iota(I32, (E, tt), 0)
    ri = lax.broadcasted_iota(I32, (tt, tt), 0)
    ci = lax.broadcasted_iota(I32, (tt, tt), 1)
    before = jnp.where(ri < ci, 1.0, 0.0).astype(BF16)
    base = carry_ref[...]
    for k in range(K):
        hit = ei == idx_ref[k:k + 1, :]
        onehot = jnp.where(hit, 1.0, 0.0)
        prefix = _dot(onehot.astype(BF16), before)
        rank = jnp.sum(jnp.where(hit, base + prefix, 0.0), axis=0, keepdims=True)
        rank_ref[k:k + 1, :] = rank.astype(I32)
        base = base + jnp.sum(onehot, axis=1, keepdims=True)
    carry_ref[...] = base
    cnt_ref[...] = base


def _rank(idx_t):
    B, K, S = idx_t.shape
    tt = min(256, S)
    return pl.pallas_call(
        _rank_kernel,
        grid=(B, S // tt),
        in_specs=[pl.BlockSpec((None, K, tt), lambda b, i: (b, 0, i))],
        out_specs=[pl.BlockSpec((None, K, tt), lambda b, i: (b, 0, i)),
                   pl.BlockSpec((N_EXPERTS, 1), lambda b, i: (0, 0))],
        out_shape=[jax.ShapeDtypeStruct((B, K, S), I32),
                   jax.ShapeDtypeStruct((N_EXPERTS, 1), F32)],
        scratch_shapes=[pltpu.VMEM((N_EXPERTS, 1), F32)],
        compiler_params=_cp(("arbitrary", "arbitrary"), VMEM_LIMIT),
        name="rank",
    )(idx_t)


def _pos_kernel(idx_ref, rank_ref, start_ref, pos_ref):
    K, tt = idx_ref.shape
    ei = lax.broadcasted_iota(I32, (N_EXPERTS, tt), 0)
    start = start_ref[...]
    for k in range(K):
        hit = ei == idx_ref[k:k + 1, :]
        off = jnp.sum(jnp.where(hit, start, 0.0), axis=0, keepdims=True)
        pos_ref[k:k + 1, :] = rank_ref[k:k + 1, :] + off.astype(I32)


def _pos(idx_t, rank_t, pstart):
    B, K, S = idx_t.shape
    tt = min(512, S)
    spec = pl.BlockSpec((None, K, tt), lambda b, i: (b, 0, i))
    return pl.pallas_call(
        _pos_kernel,
        grid=(B, S // tt),
        in_specs=[spec, spec, pl.BlockSpec((N_EXPERTS, 1), lambda b, i: (0, 0))],
        out_specs=spec,
        out_shape=jax.ShapeDtypeStruct((B, K, S), I32),
        compiler_params=_cp(("parallel", "parallel"), VMEM_LIMIT),
        name="slot",
    )(idx_t, rank_t, pstart)


def _dispatch_kernel(pos_ref, h_ref, xs_ref, sem, *, tt):
    def copy(t, k):
        return pltpu.make_async_copy(h_ref.at[pl.ds(t, 1), :],
                                     xs_ref.at[pl.ds(pos_ref[k, t], 1), :], sem)

    def issue(t, c):
        for k in range(TOP_K):
            copy(t, k).start(priority=k % 2)
        return c

    def drain(t, c):
        for k in range(TOP_K):
            copy(t, k).wait()
        return c

    lax.fori_loop(0, tt, issue, 0)
    lax.fori_loop(0, tt, drain, 0)


def _dispatch(pos_kt, h2_flat, n_rows):
    T, W = h2_flat.shape
    tt = 256
    return pl.pallas_call(
        functools.partial(_dispatch_kernel, tt=tt),
        grid=(T // tt,),
        in_specs=[pl.BlockSpec((TOP_K, tt), lambda i: (0, i), memory_space=pltpu.SMEM),
                  pl.BlockSpec((tt, W), lambda i: (i, 0))],
        out_specs=pl.BlockSpec(memory_space=pl.ANY),
        out_shape=jax.ShapeDtypeStruct((n_rows, W), I32),
        scratch_shapes=[pltpu.SemaphoreType.DMA(())],
        compiler_params=pltpu.CompilerParams(dimension_semantics=("arbitrary",),
                                             has_side_effects=True),
        name="dispatch",
    )(pos_kt, h2_flat)


def _expert_kernel(be_ref, nu_ref, xs_ref, w1_ref, w3_ref, w2_ref, ys_ref, w1b, w3b, w2b):
    b = pl.program_id(0)
    prev = be_ref[jnp.maximum(b - 1, 0)]
    fresh = jnp.logical_or(b == 0, be_ref[b] != prev)

    @pl.when(jnp.logical_and(fresh, b < nu_ref[0]))
    def _():
        w1b[...] = w1_ref[...].astype(BF16)
        w3b[...] = w3_ref[...].astype(BF16)
        w2b[...] = w2_ref[...].astype(BF16)

    @pl.when(b < nu_ref[0])
    def _():
        half = xs_ref.shape[1]
        lo, hi = _unpack_pair(xs_ref[...])
        lo, hi = lo.astype(BF16), hi.astype(BF16)
        h1 = _dot(lo, w1b[:half, :]) + _dot(hi, w1b[half:, :])
        h3 = _dot(lo, w3b[:half, :]) + _dot(hi, w3b[half:, :])
        y = _dot((_silu(h1) * h3).astype(BF16), w2b[...])
        ys_ref[...] = _pack_pair(y[:, :half], y[:, half:])


def _experts(blk_expert, n_used, xs, w1, w3, w2, layer):
    P, W = xs.shape
    D = 2 * W
    nb = P // EXPERT_BLOCK
    last = lambda b, be, nu: jnp.minimum(b, nu[0] - 1)
    wspec = lambda r, c: pl.BlockSpec((None, None, r, c),
                                      lambda b, be, nu: (layer, be[last(b, be, nu)], 0, 0))
    grid_spec = pltpu.PrefetchScalarGridSpec(
        num_scalar_prefetch=2,
        grid=(nb,),
        in_specs=[pl.BlockSpec((EXPERT_BLOCK, W), lambda b, be, nu: (last(b, be, nu), 0)),
                  wspec(D, D_EXPERT), wspec(D, D_EXPERT), wspec(D_EXPERT, D)],
        out_specs=pl.BlockSpec((EXPERT_BLOCK, W), lambda b, be, nu: (last(b, be, nu), 0)),
        scratch_shapes=[pltpu.VMEM((D, D_EXPERT), BF16), pltpu.VMEM((D, D_EXPERT), BF16),
                        pltpu.VMEM((D_EXPERT, D), BF16)],
    )
    return pl.pallas_call(
        _expert_kernel,
        grid_spec=grid_spec,
        out_shape=jax.ShapeDtypeStruct((P, W), I32),
        compiler_params=_cp(("arbitrary",), VMEM_LIMIT),
        name="experts",
    )(blk_expert, n_used, xs, w1, w3, w2)


def _combine_kernel(pos_ref, ys_ref, wt_ref, h2_ref, x1_ref, g2_ref, lg_ref, lb_ref, ws1_ref,
                    ws3_ref, ws2_ref, o_ref, buf, sem, *, tt, alpha):
    def copy(t, k):
        return pltpu.make_async_copy(ys_ref.at[pl.ds(pos_ref[k, t], 1), :],
                                     buf.at[k, pl.ds(t, 1), :], sem)

    def issue(t, c):
        for k in range(TOP_K):
            copy(t, k).start(priority=k % 2)
        return c

    def drain(t, c):
        for k in range(TOP_K):
            copy(t, k).wait()
        return c

    lax.fori_loop(0, tt, issue, 0)
    half = h2_ref.shape[1]
    lo, hi = _unpack_pair(h2_ref[...])
    lo, hi = lo.astype(BF16), hi.astype(BF16)
    h1 = _dot(lo, ws1_ref[:half, :]) + _dot(hi, ws1_ref[half:, :])
    h3 = _dot(lo, ws3_ref[:half, :]) + _dot(hi, ws3_ref[half:, :])
    shared = _dot((_silu(h1) * h3).astype(BF16), ws2_ref[...])
    lax.fori_loop(0, tt, drain, 0)
    wt = wt_ref[...]
    f_lo, f_hi = shared[:, :half], shared[:, half:]
    for k in range(TOP_K):
        y_lo, y_hi = _unpack_pair(buf[k])
        f_lo = f_lo + wt[:, k:k + 1] * y_lo
        f_hi = f_hi + wt[:, k:k + 1] * y_hi
    ffn = jnp.concatenate([f_lo, f_hi], axis=1)
    o_ref[...] = _ln(alpha * x1_ref[...] + g2_ref[...] * ffn) * lg_ref[...] + lb_ref[...]


def _combine(pos_kt, ys, w_tk, h2, x1, g2, ln_g, ln_b, ws1, ws3, ws2, alpha):
    B, S, D = x1.shape
    tt = min(256, S)
    nt = S // tt
    row = lambda n: pl.BlockSpec((None, tt, n), lambda b, i: (b, i, 0))
    full = lambda a: pl.BlockSpec(a.shape, lambda b, i: (0,) * a.ndim)
    return pl.pallas_call(
        functools.partial(_combine_kernel, tt=tt, alpha=alpha),
        grid=(B, nt),
        in_specs=[pl.BlockSpec((TOP_K, tt), lambda b, i: (0, b * nt + i), memory_space=pltpu.SMEM),
                  pl.BlockSpec(memory_space=pl.ANY),
                  row(TOP_K), row(D // 2), row(D),
                  pl.BlockSpec((None, 1, D), lambda b, i: (b, 0, 0)),
                  full(ln_g), full(ln_b), full(ws1), full(ws3), full(ws2)],
        out_specs=row(D),
        out_shape=jax.ShapeDtypeStruct((B, S, D), F32),
        scratch_shapes=[pltpu.VMEM((TOP_K, tt, D // 2), I32), pltpu.SemaphoreType.DMA(())],
        compiler_params=_cp(("arbitrary", "arbitrary"), VMEM_LIMIT),
        name="combine",
    )(pos_kt, ys, w_tk, h2, x1, g2, ln_g, ln_b, ws1, ws3, ws2)


def _arrange_w_in(w):
    z = lambda n: jnp.zeros((w.shape[0], n), w.dtype)
    return jnp.concatenate([w[:, 0:1280], w[:, 1296:1552], w[:, 1552:1680], w[:, 1680:1712],
                            z(96), w[:, 1280:1296], z(112)], axis=1).astype(BF16)


def _head_layout(nope, rope_a, rope_b):
    r = nope.shape[0]
    z = jnp.zeros((r, MLA_HEADS, MLA_HEAD_PAD - MLA_NOPE - MLA_ROPE), nope.dtype)
    return jnp.concatenate([nope, rope_a, rope_b, z], axis=2).reshape(r, MLA_HEADS * MLA_HEAD_PAD)


def _arrange_mla(w_uq, w_uk):
    half = MLA_ROPE // 2
    q = w_uq.reshape(MLA_Q_LORA, MLA_HEADS, MLA_NOPE + MLA_ROPE)
    qn, q1, q2 = q[..., :MLA_NOPE], q[..., MLA_NOPE:MLA_NOPE + half], q[..., MLA_NOPE + half:]
    wq = _head_layout(qn, q1, q2)
    wq_sw = _head_layout(jnp.zeros_like(qn), q2, q1)
    kn = w_uk.reshape(MLA_KV_LORA, MLA_HEADS, MLA_NOPE)
    zk = jnp.zeros((MLA_KV_LORA, MLA_HEADS, half), w_uk.dtype)
    eye = jnp.eye(MLA_ROPE, dtype=w_uk.dtype)
    e1 = jnp.broadcast_to(eye[:, None, :half], (MLA_ROPE, MLA_HEADS, half))
    e2 = jnp.broadcast_to(eye[:, None, half:], (MLA_ROPE, MLA_HEADS, half))
    zn = jnp.zeros((MLA_ROPE, MLA_HEADS, MLA_NOPE), w_uk.dtype)
    pad = jnp.zeros((128 - MLA_ROPE, MLA_HEADS * MLA_HEAD_PAD), w_uk.dtype)
    wk = jnp.concatenate([_head_layout(kn, zk, zk), _head_layout(zn, e1, e2), pad], axis=0)
    wk_sw = jnp.concatenate([_head_layout(jnp.zeros_like(kn), zk, zk), _head_layout(zn, e2, e1), pad],
                            axis=0)
    return wq.astype(BF16), wq_sw.astype(BF16), wk.astype(BF16), wk_sw.astype(BF16)


def _rope_rows():
    half = MLA_ROPE // 2
    inv_freq = ROPE_THETA ** (-jnp.arange(0, MLA_ROPE, 2, dtype=F32) / MLA_ROPE)
    z = lambda n: jnp.zeros((n,), F32)
    frq = jnp.concatenate([z(MLA_NOPE), inv_freq, inv_freq, z(MLA_HEAD_PAD - MLA_NOPE - MLA_ROPE)])
    sgn = jnp.concatenate([z(MLA_NOPE), -jnp.ones((half,), F32), jnp.ones((half,), F32),
                           z(MLA_HEAD_PAD - MLA_NOPE - MLA_ROPE)])
    return frq[None, :], sgn[None, :]


def _block_diag_pool(w_pool):
    G, C, _ = w_pool.shape
    out = jnp.zeros((G * C, G * C), w_pool.dtype)
    for g in range(G):
        out = out.at[g * C:(g + 1) * C, g * C:(g + 1) * C].set(w_pool[g])
    return out.astype(BF16)


def kernel(x, c, positions, w_ada, b_ada, w_in, w_pool, s_pool, conv_w, conv_b, gate_b, gn_w, g_q,
           g_kv, w_uq, w_uk, w_uv, w_out, ln1_g, ln1_b, w_router, e_bias, w1, w3, w2, ws1, ws3, ws2,
           ln2_g, ln2_b):
    B, S, D = x.shape
    depth = w_in.shape[0]
    T = B * S
    H = ML_HEADS
    alpha = float((2 * depth) ** 0.25)
    n_assign = T * TOP_K
    n_blocks = n_assign // EXPERT_BLOCK + N_EXPERTS
    n_rows = n_blocks * EXPERT_BLOCK

    ada = _ada(c, w_ada, b_ada)
    frq, sgn = _rope_rows()
    pos3 = positions.reshape(B, S, 1)
    qk_scale = jnp.concatenate([jnp.ones((1, ML_WIDTH), F32),
                                jnp.full((1, ML_WIDTH), ML_DH ** -0.5, F32)], axis=1)
    ones_col = jnp.concatenate([jnp.ones((B, H, S, 1), BF16),
                                jnp.zeros((B, H, S, 128 - ML_DH - 1), BF16)], axis=-1)

    for l in range(depth):
        sh1, sc1, g1, sh2, sc2, g2 = [a.reshape(B, 1, D) for a in jnp.split(ada[l], 6, axis=-1)]
        u_pool, u_qk, u_v, u_o, u_dq, u_kv, u_gate = _inproj(x, sc1, sh1, _arrange_w_in(w_in[l]))
        y_pool = _pool(u_pool, _block_diag_pool(w_pool[l]), s_pool[l][None, :])

        qk_act = _conv(u_qk, conv_w[l], conv_b[l][None, :], qk_scale)
        heads = lambda a, n: a.reshape(B, S, n, ML_DH).transpose(0, 2, 1, 3)
        gates = u_gate[:, :, :4 * H].reshape(B, S, 4, H)
        y_ml = _mlstm(heads(qk_act, 2 * H),
                      jnp.concatenate([heads(u_v, H), ones_col], axis=-1),
                      heads(u_o, H),
                      gates.transpose(0, 3, 1, 2),
                      gates.transpose(0, 3, 2, 1),
                      gate_b[l].reshape(4, H).T.reshape(H, 1, 4),
                      gate_b[l].reshape(4, H).T.reshape(H, 4, 1),
                      gn_w[l].reshape(H, 1, ML_DH))
        y_ml = y_ml.transpose(0, 2, 1, 3).reshape(B, S, ML_WIDTH)

        wq, wq_sw, wk, wk_sw = _arrange_mla(w_uq[l], w_uk[l])
        qh, kh, vh = _mla_proj(u_dq, u_kv, pos3, frq, sgn, g_q[l][None, :], g_kv[l][None, :],
                               wq, wq_sw, wk, wk_sw, w_uv[l].astype(BF16))
        y_mla = _attn(qh, kh, vh.transpose(0, 2, 1)).transpose(0, 2, 1)

        x1, h2, logits_t = _outproj(y_pool, y_ml, y_mla, x, g1, ln1_g[l][None, :], ln1_b[l][None, :],
                                    sc2, sh2, w_out[l].astype(BF16), w_router[l].T.astype(BF16), alpha)
        idx_t, w_t = _route(logits_t, e_bias[l][:, None])
        rank_t, counts = _rank(idx_t)
        counts = counts[:, 0].astype(I32)
        padded = (counts + EXPERT_BLOCK - 1) // EXPERT_BLOCK * EXPERT_BLOCK
        pend = jnp.cumsum(padded)
        pstart = pend - padded
        n_used = (pend[-1:] // EXPERT_BLOCK).astype(I32)
        blk_start = jnp.arange(n_blocks, dtype=I32) * EXPERT_BLOCK
        blk_expert = jnp.minimum(jnp.sum((pend[None, :] <= blk_start[:, None]).astype(I32), axis=1),
                                 N_EXPERTS - 1)
        pos_kt = _pos(idx_t, rank_t, pstart.astype(F32)[:, None]).transpose(1, 0, 2).reshape(TOP_K, T)
        xs = _dispatch(pos_kt, h2.reshape(T, D // 2), n_rows)
        ys = _experts(blk_expert, n_used, xs, w1, w3, w2, l)
        x = _combine(pos_kt, ys, w_t.transpose(0, 2, 1), h2, x1, g2, ln2_g[l][None, :],
                     ln2_b[l][None, :], ws1[l].astype(BF16), ws3[l].astype(BF16),
                     ws2[l].astype(BF16), alpha)
    return x
```

```python
import functools

import jax
import jax.numpy as jnp
import numpy as np
from jax import lax
from jax.experimental import pallas as pl
from jax.experimental.pallas import tpu as pltpu

F32 = jnp.float32
BF16 = jnp.bfloat16
I32 = jnp.int32

D_MODEL = 1024
POOL_WINDOWS = (2, 4, 8, 16)
POOL_GROUP_DIM = 64
POOL_WIDTH = 256
ML_HEADS = 4
ML_DH = 64
ML_WIDTH = 256
ML_CONV = 5
ML_VT_ROWS = 80
MLA_HEADS = 8
MLA_NOPE = 64
MLA_ROPE = 32
MLA_V = 64
MLA_Q_LORA = 256
MLA_KV_LORA = 128
MLA_HEAD_PAD = 128
ROPE_THETA = 10000.0
N_EXPERTS = 256
TOP_K = 8
N_GROUPS = 8
TOPK_GROUPS = 4
GROUP_SIZE = N_EXPERTS // N_GROUPS
D_EXPERT = 256
ROUTED_SCALE = 2.5
LN_EPS = 1e-5
RMS_EPS = 1e-6
NEG_BIG = -1e30
NEG_INF = float("-inf")

EXPERT_BLOCK = 256
VMEM_LIMIT = 56 * 1024 * 1024


def _cp(sem, vmem=None):
    return pltpu.CompilerParams(dimension_semantics=sem, vmem_limit_bytes=vmem)


def _ln(x):
    mu = jnp.mean(x, axis=-1, keepdims=True)
    xc = x - mu
    var = jnp.mean(xc * xc, axis=-1, keepdims=True)
    return xc * lax.rsqrt(var + LN_EPS)


def _silu(x):
    return x * jax.nn.sigmoid(x)


def _rms(x, g):
    return x * lax.rsqrt(jnp.mean(x * x, axis=-1, keepdims=True) + RMS_EPS) * g


def _dot(a, b):
    return jnp.dot(a, b, preferred_element_type=F32)


def _dot_nt(a, b):
    return lax.dot_general(a, b, (((1,), (1,)), ((), ())), preferred_element_type=F32)


def _dot_tn(a, b):
    return lax.dot_general(a, b, (((0,), (0,)), ((), ())), preferred_element_type=F32)


def _pack_pair(lo, hi):
    lo_b = lax.bitcast_convert_type(lo.astype(BF16).astype(F32), I32)
    hi_b = lax.bitcast_convert_type(hi.astype(BF16).astype(F32), I32)
    return jnp.bitwise_or(hi_b, lax.shift_right_logical(lo_b, 16))


def _unpack_pair(w):
    lo = lax.bitcast_convert_type(lax.shift_left(w, 16), F32)
    hi = lax.bitcast_convert_type(jnp.bitwise_and(w, -65536), F32)
    return lo, hi


def _ada_kernel(c_ref, w_ref, b_ref, o_ref):
    ca = _silu(c_ref[...]).astype(BF16)
    o_ref[...] = _dot(ca, w_ref[...].astype(BF16)) + b_ref[...]


def _ada(c, w_ada, b_ada):
    L, D, N = w_ada.shape
    B = c.shape[0]
    tn = 1536
    return pl.pallas_call(
        _ada_kernel,
        grid=(L, N // tn),
        in_specs=[pl.BlockSpec((B, D), lambda l, j: (0, 0)),
                  pl.BlockSpec((None, D, tn), lambda l, j: (l, 0, j)),
                  pl.BlockSpec((None, 1, tn), lambda l, j: (l, 0, j))],
        out_specs=pl.BlockSpec((None, B, tn), lambda l, j: (l, 0, j)),
        out_shape=jax.ShapeDtypeStruct((L, B, N), F32),
        compiler_params=_cp(("parallel", "parallel"), VMEM_LIMIT),
        name="ada",
    )(c, w_ada, b_ada.reshape(L, 1, N))


def _inproj_kernel(x_ref, sc_ref, sh_ref, w_ref, pool_ref, qk_ref, v_ref, o_ref, dq_ref,
                   kv_ref, gate_ref):
    h = (_ln(x_ref[...]) * (1.0 + sc_ref[...]) + sh_ref[...]).astype(BF16)
    pool_ref[...] = _dot(h, w_ref[:, 0:256])
    qk_ref[...] = _dot(h, w_ref[:, 256:768])
    v_ref[...] = _dot(h, w_ref[:, 768:1024]).astype(BF16)
    o_ref[...] = _dot(h, w_ref[:, 1024:1280])
    dq_ref[...] = _dot(h, w_ref[:, 1280:1536])
    kv_ref[...] = _dot(h, w_ref[:, 1536:1792])
    gate_ref[...] = _dot(h, w_ref[:, 1792:1920])


def _inproj(x, sc, sh, w):
    B, S, D = x.shape
    tm = min(512, S)
    widths = (256, 512, 256, 256, 256, 256, 128)
    dtypes = (F32, F32, BF16, F32, F32, F32, F32)
    row = lambda n: pl.BlockSpec((None, tm, n), lambda b, i: (b, i, 0))
    return pl.pallas_call(
        _inproj_kernel,
        grid=(B, S // tm),
        in_specs=[row(D),
                  pl.BlockSpec((None, 1, D), lambda b, i: (b, 0, 0)),
                  pl.BlockSpec((None, 1, D), lambda b, i: (b, 0, 0)),
                  pl.BlockSpec(w.shape, lambda b, i: (0, 0))],
        out_specs=[row(n) for n in widths],
        out_shape=[jax.ShapeDtypeStruct((B, S, n), dt) for n, dt in zip(widths, dtypes)],
        compiler_params=_cp(("parallel", "parallel"), VMEM_LIMIT),
        name="inproj",
    )(x, sc, sh, w)


def _shift_dn(a, k, row):
    return jnp.where(row >= k, pltpu.roll(a, k, 0), 0.0)


def _shift_up(a, k, row):
    n = a.shape[0]
    return jnp.where(row < n - k, pltpu.roll(a, n - k, 0), 0.0)


def _pool_kernel(u_ref, w_ref, s_ref, y_ref):
    x = u_ref[...]
    S, C = x.shape
    row = lax.broadcasted_iota(I32, (S, C), 0)
    lane = lax.broadcasted_iota(I32, (S, C), 1)
    rowf = row.astype(F32)
    p1, f1 = x, x
    p2 = p1 + _shift_dn(p1, 1, row)
    f2 = f1 + _shift_up(f1, 1, row)
    p4 = p2 + _shift_dn(p2, 2, row)
    f4 = f2 + _shift_up(f2, 2, row)
    p8 = p4 + _shift_dn(p4, 4, row)
    f8 = f4 + _shift_up(f4, 4, row)
    wins = []
    for half, p, f in ((1, p1, f1), (2, p2, f2), (4, p4, f4), (8, p8, f8)):
        total = _shift_dn(p, 1, row) + f
        cnt = jnp.minimum(rowf + half, float(S)) - jnp.maximum(rowf - half, 0.0)
        wins.append(total / cnt)
    pooled = jnp.where(lane < 64, wins[0],
                       jnp.where(lane < 128, wins[1], jnp.where(lane < 192, wins[2], wins[3])))
    d = (pooled - x).astype(BF16)
    y_ref[...] = (_dot(d, w_ref[...]) * s_ref[...]).astype(BF16)


def _pool(u_pool, w_bd, s_pool):
    B, S, C = u_pool.shape
    return pl.pallas_call(
        _pool_kernel,
        grid=(B,),
        in_specs=[pl.BlockSpec((None, S, C), lambda b: (b, 0, 0)),
                  pl.BlockSpec((C, C), lambda b: (0, 0)),
                  pl.BlockSpec((1, C), lambda b: (0, 0))],
        out_specs=pl.BlockSpec((None, S, C), lambda b: (b, 0, 0)),
        out_shape=jax.ShapeDtypeStruct((B, S, C), BF16),
        compiler_params=_cp(("parallel",), VMEM_LIMIT),
        name="pool",
    )(u_pool, w_bd, s_pool)


def _conv_kernel(u_ref, w_ref, b_ref, scale_ref, y_ref):
    x = u_ref[...]
    S, C = x.shape
    row = lax.broadcasted_iota(I32, (S, C), 0)
    y = (_shift_dn(x, 2, row) * w_ref[0:1, :] + _shift_dn(x, 1, row) * w_ref[1:2, :]
         + x * w_ref[2:3, :] + _shift_up(x, 1, row) * w_ref[3:4, :]
         + _shift_up(x, 2, row) * w_ref[4:5, :] + b_ref[...])
    y_ref[...] = (_silu(y) * scale_ref[...]).astype(BF16)


def _conv(u_qk, conv_w, conv_b, scale):
    B, S, C = u_qk.shape
    return pl.pallas_call(
        _conv_kernel,
        grid=(B,),
        in_specs=[pl.BlockSpec((None, S, C), lambda b: (b, 0, 0)),
                  pl.BlockSpec((ML_CONV, C), lambda b: (0, 0)),
                  pl.BlockSpec((1, C), lambda b: (0, 0)),
                  pl.BlockSpec((1, C), lambda b: (0, 0))],
        out_specs=pl.BlockSpec((None, S, C), lambda b: (b, 0, 0)),
        out_shape=jax.ShapeDtypeStruct((B, S, C), BF16),
        compiler_params=_cp(("parallel",), VMEM_LIMIT),
        name="mlstm_conv",
    )(u_qk, conv_w, conv_b, scale)


def _log_sigmoid(x):
    return jnp.minimum(x, 0.0) - jnp.log(1.0 + jnp.exp(-jnp.abs(x)))


def _split3(x):
    a = x.astype(BF16)
    r = x - a.astype(F32)
    b = r.astype(BF16)
    c = (r - b.astype(F32)).astype(BF16)
    return a, b, c


def _mlstm_chunk(s, k, qt, vt, c_col, b_row, i_row, b_last, allowed, ct, m_prev):
    logw = jnp.where(allowed, c_col + b_row, NEG_INF)
    m_inter = b_row + m_prev
    m_t = jnp.maximum(m_inter, jnp.max(logw, axis=0, keepdims=True))
    w_intra = jnp.exp(logw - m_t)
    w_inter = jnp.exp(m_inter - m_t)
    qk = (s * w_intra).astype(BF16)
    nd = _dot(vt, qk) + w_inter * _dot(ct.astype(BF16), qt)
    den = jnp.maximum(jnp.abs(nd[ML_DH:ML_DH + 1, :]), jnp.exp(-m_t))
    h = nd[:ML_DH, :] / den
    g = b_last - b_row + i_row
    m_new = jnp.maximum(b_last + m_prev, jnp.max(g, axis=1, keepdims=True))
    w_k = jnp.exp(g - m_new)
    decay = jnp.exp(b_last + m_prev - m_new)
    ct_new = decay * ct + _dot((vt.astype(F32) * w_k).astype(BF16), k)
    return h, ct_new, m_new


def _mlstm_kernel(k_ref, qt_ref, vt_ref, uo_ref, gc_ref, gr_ref, bc_ref, br_ref, gn_ref, y_ref,
                  hf_ref, hb_ref, *, chunk):
    S = k_ref.shape[0]
    L = chunk
    nc = S // L
    ri = lax.broadcasted_iota(I32, (L, L), 0)
    ci = lax.broadcasted_iota(I32, (L, L), 1)
    lower = ci <= ri
    upper = ci >= ri
    tril = jnp.where(lower, 1.0, 0.0).astype(BF16)
    triu = jnp.where(upper, 1.0, 0.0).astype(BF16)
    bias_c = bc_ref[...]
    bias_r = br_ref[...]

    def prep(r0, forward):
        gc = gc_ref[pl.ds(r0, L), :] + bias_c
        gr = gr_ref[:, pl.ds(r0, L)] + bias_r
        c1, c2, c3 = _split3(_log_sigmoid(gc))
        r1, r2, r3 = _split3(_log_sigmoid(gr))
        k = k_ref[pl.ds(r0, L), :]
        qt = qt_ref[:, pl.ds(r0, L)]
        s = _dot(k, qt)
        if forward:
            cum_c = (_dot(tril, c1) + _dot(tril, c2) + _dot(tril, c3))[:, 1:2]
            cum_r = (_dot(r1, triu) + _dot(r2, triu) + _dot(r3, triu))[1:2, :]
            return s, k, qt, gc[:, 0:1] - cum_c, cum_r, gr[0:1, :], cum_r[:, L - 1:L]
        cum_c = (_dot(triu, c1) + _dot(triu, c2) + _dot(triu, c3))[:, 3:4]
        cum_r = (_dot(r1, tril) + _dot(r2, tril) + _dot(r3, tril))[3:4, :]
        return s, k, qt, gc[:, 2:3] - cum_c, cum_r, gr[2:3, :], cum_r[:, 0:1]

    def step(i, carry):
        cf, mf, cb, mb = carry
        rf = pl.multiple_of(i * L, L)
        rb = pl.multiple_of((nc - 1 - i) * L, L)
        pf = prep(rf, True)
        pb = prep(rb, False)
        h, cf, mf = _mlstm_chunk(*pf[:3], vt_ref[:, pl.ds(rf, L)], *pf[3:], upper, cf, mf)
        hf_ref[:, pl.ds(rf, L)] = h
        h, cb, mb = _mlstm_chunk(*pb[:3], vt_ref[:, pl.ds(rb, L)], *pb[3:], lower, cb, mb)
        hb_ref[:, pl.ds(rb, L)] = h
        return cf, mf, cb, mb

    c0 = jnp.zeros((ML_VT_ROWS, ML_DH), F32)
    m0 = jnp.full((1, 1), NEG_BIG, F32)
    lax.fori_loop(0, nc, step, (c0, m0, c0, m0))
    h = hf_ref[...] + hb_ref[...]
    mu = jnp.mean(h, axis=0, keepdims=True)
    hc = h - mu
    var = jnp.mean(hc * hc, axis=0, keepdims=True)
    hn = hc * lax.rsqrt(var + LN_EPS)
    y_ref[...] = (jax.nn.sigmoid(uo_ref[...]) * (hn * gn_ref[...])).astype(BF16)


def _mlstm(k_heads, qt_heads, vt_ext, uot_heads, g_col, g_row, b_col, b_row, gn_w):
    B, H, S, dh = k_heads.shape
    L = min(256, S)
    tr = lambda n: pl.BlockSpec((None, None, n, S), lambda b, h: (b, h, 0, 0))
    return pl.pallas_call(
        functools.partial(_mlstm_kernel, chunk=L),
        grid=(B, H),
        in_specs=[pl.BlockSpec((None, None, S, dh), lambda b, h: (b, h, 0, 0)),
                  tr(dh), tr(ML_VT_ROWS), tr(dh),
                  pl.BlockSpec((None, None, S, 4), lambda b, h: (b, h, 0, 0)),
                  tr(4),
                  pl.BlockSpec((None, 1, 4), lambda b, h: (h, 0, 0)),
                  pl.BlockSpec((None, 4, 1), lambda b, h: (h, 0, 0)),
                  pl.BlockSpec((None, dh, 1), lambda b, h: (h, 0, 0))],
        out_specs=tr(dh),
        out_shape=jax.ShapeDtypeStruct((B, H, dh, S), BF16),
        scratch_shapes=[pltpu.VMEM((dh, S), F32), pltpu.VMEM((dh, S), F32)],
        compiler_params=_cp(("parallel", "parallel"), VMEM_LIMIT),
        name="mlstm_scan",
    )(k_heads, qt_heads, vt_ext, uot_heads, g_col, g_row, b_col, b_row, gn_w)


def _mla_proj_kernel(dq_ref, kv_ref, pos_ref, frq_ref, sgn_ref, gq_ref, gkv_ref, wq_ref, wqs_ref,
                     wk_ref, wks_ref, wv_ref, q_ref, k_ref, v_ref, *, scale):
    ang = pos_ref[...].astype(F32) * frq_ref[...]
    cos = jnp.cos(ang)
    sin = jnp.sin(ang) * sgn_ref[...]
    qn = _rms(dq_ref[...], gq_ref[...]).astype(BF16)
    a = _dot(qn, wq_ref[...])
    a_sw = _dot(qn, wqs_ref[...])
    kv = kv_ref[...]
    ckv = _rms(kv[:, :MLA_KV_LORA], gkv_ref[...])
    kin = jnp.concatenate([ckv, kv[:, MLA_KV_LORA:]], axis=1).astype(BF16)
    ak = _dot(kin, wk_ref[...])
    ak_sw = _dot(kin, wks_ref[...])
    for h in range(MLA_HEADS):
        sl = slice(h * MLA_HEAD_PAD, (h + 1) * MLA_HEAD_PAD)
        q_ref[:, sl] = ((a[:, sl] * cos + a_sw[:, sl] * sin) * scale).astype(BF16)
        k_ref[:, sl] = (ak[:, sl] * cos + ak_sw[:, sl] * sin).astype(BF16)
    v_ref[...] = _dot(ckv.astype(BF16), wv_ref[...]).astype(BF16)


def _mla_proj(u_dq, u_kv, pos, frq, sgn, g_q, g_kv, wq, wqs, wk, wks, wv):
    B, S, _ = u_dq.shape
    tm = min(512, S)
    hw = MLA_HEADS * MLA_HEAD_PAD
    vw = MLA_HEADS * MLA_V
    row = lambda n: pl.BlockSpec((None, tm, n), lambda b, i: (b, i, 0))
    full = lambda a: pl.BlockSpec(a.shape, lambda b, i: (0,) * a.ndim)
    scale = (MLA_NOPE + MLA_ROPE) ** -0.5 * float(np.log2(np.e))
    return pl.pallas_call(
        functools.partial(_mla_proj_kernel, scale=scale),
        grid=(B, S // tm),
        in_specs=[row(MLA_Q_LORA), row(256), row(1), full(frq), full(sgn), full(g_q), full(g_kv),
                  full(wq), full(wqs), full(wk), full(wks), full(wv)],
        out_specs=[row(hw), row(hw), row(vw)],
        out_shape=[jax.ShapeDtypeStruct((B, S, hw), BF16),
                   jax.ShapeDtypeStruct((B, S, hw), BF16),
                   jax.ShapeDtypeStruct((B, S, vw), BF16)],
        compiler_params=_cp(("parallel", "parallel"), VMEM_LIMIT),
        name="mla_proj",
    )(u_dq, u_kv, pos, frq, sgn, g_q, g_kv, wq, wqs, wk, wks, wv)


def _attn_kernel(q_ref, k_ref, vt_ref, o_ref, *, tk):
    tq = q_ref.shape[0]
    S = k_ref.shape[0]
    nk = S // tk
    def scores(h, j):
        k = k_ref[j * tk:(j + 1) * tk, h * MLA_HEAD_PAD:(h + 1) * MLA_HEAD_PAD]
        return _dot_nt(k, q_ref[:, h * MLA_HEAD_PAD:(h + 1) * MLA_HEAD_PAD])

    m = [jnp.full((1, tq), NEG_INF, F32)] * 2
    l = [jnp.zeros((1, tq), F32)] * 2
    acc = [jnp.zeros((MLA_V, tq), F32)] * 2
    s = [scores(0, 0), scores(1, 0)]
    for j in range(nk):
        for h in range(2):
            s_cur = s[h]
            if j + 1 < nk:
                s[h] = scores(h, j + 1)
            m_new = jnp.maximum(m[h], jnp.max(s_cur, axis=0, keepdims=True))
            alpha = jnp.exp2(m[h] - m_new)
            p = jnp.exp2(s_cur - m_new)
            l[h] = alpha * l[h] + jnp.sum(p, axis=0, keepdims=True)
            vt = vt_ref[h * MLA_V:(h + 1) * MLA_V, j * tk:(j + 1) * tk]
            acc[h] = alpha * acc[h] + _dot(vt, p.astype(BF16))
            m[h] = m_new
    for h in range(2):
        o_ref[h * MLA_V:(h + 1) * MLA_V, :] = (acc[h] / l[h]).astype(BF16)


def _attn(qh, kh, vt):
    B, S, _ = qh.shape
    tq = min(512, S)
    tk = min(256, S)
    pw = 2 * MLA_HEAD_PAD
    return pl.pallas_call(
        functools.partial(_attn_kernel, tk=tk),
        grid=(B, MLA_HEADS // 2, S // tq),
        in_specs=[pl.BlockSpec((None, tq, pw), lambda b, p, i: (b, i, p)),
                  pl.BlockSpec((None, S, pw), lambda b, p, i: (b, 0, p)),
                  pl.BlockSpec((None, 2 * MLA_V, S), lambda b, p, i: (b, p, 0))],
        out_specs=pl.BlockSpec((None, 2 * MLA_V, tq), lambda b, p, i: (b, p, i)),
        out_shape=jax.ShapeDtypeStruct((B, MLA_HEADS * MLA_V, S), BF16),
        compiler_params=_cp(("parallel", "parallel", "parallel"), VMEM_LIMIT),
        name="mla_attn",
    )(qh, kh, vt)


def _outproj_kernel(yp_ref, ym_ref, ya_ref, x_ref, g1_ref, lg_ref, lb_ref, sc_ref, sh_ref, wo_ref,
                    wr_ref, x1_ref, h2_ref, lt_ref, *, alpha):
    mix = (_dot(yp_ref[...], wo_ref[0:256, :]) + _dot(ym_ref[...], wo_ref[256:512, :])
           + _dot(ya_ref[...], wo_ref[512:1024, :]))
    x1 = _ln(alpha * x_ref[...] + g1_ref[...] * mix) * lg_ref[...] + lb_ref[...]
    x1_ref[...] = x1
    h2 = _ln(x1) * (1.0 + sc_ref[...]) + sh_ref[...]
    half = h2.shape[1] // 2
    h2_ref[...] = _pack_pair(h2[:, :half], h2[:, half:])
    lt_ref[...] = _dot_nt(wr_ref[...], h2.astype(BF16))


def _outproj(y_pool, y_ml, y_mla, x, g1, ln_g, ln_b, sc2, sh2, w_out, w_router_t, alpha):
    B, S, D = x.shape
    tm = min(512, S)
    row = lambda n: pl.BlockSpec((None, tm, n), lambda b, i: (b, i, 0))
    per_b = pl.BlockSpec((None, 1, D), lambda b, i: (b, 0, 0))
    vec = pl.BlockSpec((1, D), lambda b, i: (0, 0))
    return pl.pallas_call(
        functools.partial(_outproj_kernel, alpha=alpha),
        grid=(B, S // tm),
        in_specs=[row(256), row(256), row(512), row(D), per_b, vec, vec, per_b, per_b,
                  pl.BlockSpec((D, D), lambda b, i: (0, 0)),
                  pl.BlockSpec((N_EXPERTS, D), lambda b, i: (0, 0))],
        out_specs=[row(D), row(D // 2), pl.BlockSpec((None, N_EXPERTS, tm), lambda b, i: (b, 0, i))],
        out_shape=[jax.ShapeDtypeStruct((B, S, D), F32), jax.ShapeDtypeStruct((B, S, D // 2), I32),
                   jax.ShapeDtypeStruct((B, N_EXPERTS, S), F32)],
        compiler_params=_cp(("parallel", "parallel"), VMEM_LIMIT),
        name="outproj",
    )(y_pool, y_ml, y_mla, x, g1, ln_g, ln_b, sc2, sh2, w_out, w_router_t)


def _first_max(v, iota, n):
    m = jnp.max(v, axis=0, keepdims=True)
    first = jnp.min(jnp.where(v == m, iota, n), axis=0, keepdims=True)
    return m, first


def _route_kernel(lt_ref, eb_ref, idx_ref, w_ref):
    scores = jax.nn.sigmoid(lt_ref[...])
    sel = scores + eb_ref[...]
    E, tt = sel.shape
    gi = lax.broadcasted_iota(I32, (GROUP_SIZE, tt), 0)
    groups = [sel[g * GROUP_SIZE:(g + 1) * GROUP_SIZE, :] for g in range(N_GROUPS)]
    gs = []
    for grp in groups:
        m1, f1 = _first_max(grp, gi, GROUP_SIZE)
        m2 = jnp.max(jnp.where(gi == f1, NEG_INF, grp), axis=0, keepdims=True)
        gs.append(m1 + m2)
    chosen = [jnp.zeros((1, tt), F32) for _ in range(N_GROUPS)]
    for _ in range(TOPK_GROUPS):
        m = functools.reduce(jnp.maximum, gs)
        first = functools.reduce(
            jnp.minimum, [jnp.where(gs[g] == m, g, N_GROUPS) for g in range(N_GROUPS)])
        for g in range(N_GROUPS):
            hit = first == g
            chosen[g] = jnp.where(hit, 1.0, chosen[g])
            gs[g] = jnp.where(hit, NEG_INF, gs[g])
    cand = jnp.concatenate(
        [jnp.where(chosen[g] > 0.5, groups[g], NEG_INF) for g in range(N_GROUPS)], axis=0)
    ei = lax.broadcasted_iota(I32, (E, tt), 0)
    ws = []
    for k in range(TOP_K):
        _, f = _first_max(cand, ei, E)
        hit = ei == f
        idx_ref[k:k + 1, :] = f
        ws.append(jnp.sum(jnp.where(hit, scores, 0.0), axis=0, keepdims=True))
        cand = jnp.where(hit, NEG_INF, cand)
    norm = ROUTED_SCALE / functools.reduce(jnp.add, ws)
    for k in range(TOP_K):
        w_ref[k:k + 1, :] = ws[k] * norm


def _route(logits_t, e_bias):
    B, E, S = logits_t.shape
    tt = min(512, S)
    return pl.pallas_call(
        _route_kernel,
        grid=(B, S // tt),
        in_specs=[pl.BlockSpec((None, E, tt), lambda b, i: (b, 0, i)),
                  pl.BlockSpec((E, 1), lambda b, i: (0, 0))],
        out_specs=[pl.BlockSpec((None, TOP_K, tt), lambda b, i: (b, 0, i)),
                   pl.BlockSpec((None, TOP_K, tt), lambda b, i: (b, 0, i))],
        out_shape=[jax.ShapeDtypeStruct((B, TOP_K, S), I32),
                   jax.ShapeDtypeStruct((B, TOP_K, S), F32)],
        compiler_params=_cp(("parallel", "parallel"), VMEM_LIMIT),
        name="route",
    )(logits_t, e_bias)


def _rank_kernel(idx_ref, rank_ref, cnt_ref, carry_ref):
    first = jnp.logical_and(pl.program_id(0) == 0, pl.program_id(1) == 0)

    @pl.when(first)
    def _():
        carry_ref[...] = jnp.zeros_like(carry_ref)

    K, tt = idx_ref.shape
    E = carry_ref.shape[0]
    ei = lax.broadcasted_iota(I32, (E, tt), 0)
    ri = lax.broadcasted_iota(I32, (tt, tt), 0)
    ci = lax.broadcasted_iota(I32, (tt, tt), 1)
    before = jnp.where(ri < ci, 1.0, 0.0).astype(BF16)
    base = carry_ref[...]
    for k in range(K):
        hit = ei == idx_ref[k:k + 1, :]
        onehot = jnp.where(hit, 1.0, 0.0)
        prefix = _dot(onehot.astype(BF16), before)
        rank = jnp.sum(jnp.where(hit, base + prefix, 0.0), axis=0, keepdims=True)
        rank_ref[k:k + 1, :] = rank.astype(I32)
        base = base + jnp.sum(onehot, axis=1, keepdims=True)
    carry_ref[...] = base
    cnt_ref[...] = base


def _rank(idx_t):
    B, K, S = idx_t.shape
    tt = min(256, S)
    return pl.pallas_call(
        _rank_kernel,
        grid=(B, S // tt),
        in_specs=[pl.BlockSpec((None, K, tt), lambda b, i: (b, 0, i))],
        out_specs=[pl.BlockSpec((None, K, tt), lambda b, i: (b, 0, i)),
                   pl.BlockSpec((N_EXPERTS, 1), lambda b, i: (0, 0))],
        out_shape=[jax.ShapeDtypeStruct((B, K, S), I32),
                   jax.ShapeDtypeStruct((N_EXPERTS, 1), F32)],
        scratch_shapes=[pltpu.VMEM((N_EXPERTS, 1), F32)],
        compiler_params=_cp(("arbitrary", "arbitrary"), VMEM_LIMIT),
        name="rank",
    )(idx_t)


def _pos_kernel(idx_ref, rank_ref, start_ref, pos_ref):
    K, tt = idx_ref.shape
    ei = lax.broadcasted_iota(I32, (N_EXPERTS, tt), 0)
    start = start_ref[...]
    for k in range(K):
        hit = ei == idx_ref[k:k + 1, :]
        off = jnp.sum(jnp.where(hit, start, 0.0), axis=0, keepdims=True)
        pos_ref[k:k + 1, :] = rank_ref[k:k + 1, :] + off.astype(I32)


def _pos(idx_t, rank_t, pstart):
    B, K, S = idx_t.shape
    tt = min(512, S)
    spec = pl.BlockSpec((None, K, tt), lambda b, i: (b, 0, i))
    return pl.pallas_call(
        _pos_kernel,
        grid=(B, S // tt),
        in_specs=[spec, spec, pl.BlockSpec((N_EXPERTS, 1), lambda b, i: (0, 0))],
        out_specs=spec,
        out_shape=jax.ShapeDtypeStruct((B, K, S), I32),
        compiler_params=_cp(("parallel", "parallel"), VMEM_LIMIT),
        name="slot",
    )(idx_t, rank_t, pstart)


def _dispatch_kernel(pos_ref, h_ref, xs_ref, sem, *, tt):
    def copy(t, k):
        return pltpu.make_async_copy(h_ref.at[pl.ds(t, 1), :],
                                     xs_ref.at[pl.ds(pos_ref[k, t], 1), :], sem)

    def issue(t, c):
        for k in range(TOP_K):
            copy(t, k).start(priority=k % 2)
        return c

    def drain(t, c):
        for k in range(TOP_K):
            copy(t, k).wait()
        return c

    lax.fori_loop(0, tt, issue, 0)
    lax.fori_loop(0, tt, drain, 0)


def _dispatch(pos_kt, h2_flat, n_rows):
    T, W = h2_flat.shape
    tt = 256
    return pl.pallas_call(
        functools.partial(_dispatch_kernel, tt=tt),
        grid=(T // tt,),
        in_specs=[pl.BlockSpec((TOP_K, tt), lambda i: (0, i), memory_space=pltpu.SMEM),
                  pl.BlockSpec((tt, W), lambda i: (i, 0))],
        out_specs=pl.BlockSpec(memory_space=pl.ANY),
        out_shape=jax.ShapeDtypeStruct((n_rows, W), I32),
        scratch_shapes=[pltpu.SemaphoreType.DMA(())],
        compiler_params=pltpu.CompilerParams(dimension_semantics=("arbitrary",),
                                             has_side_effects=True),
        name="dispatch",
    )(pos_kt, h2_flat)


def _expert_kernel(be_ref, nu_ref, xs_ref, w1_ref, w3_ref, w2_ref, ys_ref, w1b, w3b, w2b):
    b = pl.program_id(0)
    prev = be_ref[jnp.maximum(b - 1, 0)]
    fresh = jnp.logical_or(b == 0, be_ref[b] != prev)

    @pl.when(jnp.logical_and(fresh, b < nu_ref[0]))
    def _():
        w1b[...] = w1_ref[...].astype(BF16)
        w3b[...] = w3_ref[...].astype(BF16)
        w2b[...] = w2_ref[...].astype(BF16)

    @pl.when(b < nu_ref[0])
    def _():
        half = xs_ref.shape[1]
        lo, hi = _unpack_pair(xs_ref[...])
        lo, hi = lo.astype(BF16), hi.astype(BF16)
        h1 = _dot(lo, w1b[:half, :]) + _dot(hi, w1b[half:, :])
        h3 = _dot(lo, w3b[:half, :]) + _dot(hi, w3b[half:, :])
        y = _dot((_silu(h1) * h3).astype(BF16), w2b[...])
        ys_ref[...] = _pack_pair(y[:, :half], y[:, half:])


def _experts(blk_expert, n_used, xs, w1, w3, w2, layer):
    P, W = xs.shape
    D = 2 * W
    nb = P // EXPERT_BLOCK
    last = lambda b, be, nu: jnp.minimum(b, nu[0] - 1)
    wspec = lambda r, c: pl.BlockSpec((None, None, r, c),
                                      lambda b, be, nu: (layer, be[last(b, be, nu)], 0, 0))
    grid_spec = pltpu.PrefetchScalarGridSpec(
        num_scalar_prefetch=2,
        grid=(nb,),
        in_specs=[pl.BlockSpec((EXPERT_BLOCK, W), lambda b, be, nu: (last(b, be, nu), 0)),
                  wspec(D, D_EXPERT), wspec(D, D_EXPERT), wspec(D_EXPERT, D)],
        out_specs=pl.BlockSpec((EXPERT_BLOCK, W), lambda b, be, nu: (last(b, be, nu), 0)),
        scratch_shapes=[pltpu.VMEM((D, D_EXPERT), BF16), pltpu.VMEM((D, D_EXPERT), BF16),
                        pltpu.VMEM((D_EXPERT, D), BF16)],
    )
    return pl.pallas_call(
        _expert_kernel,
        grid_spec=grid_spec,
        out_shape=jax.ShapeDtypeStruct((P, W), I32),
        compiler_params=_cp(("arbitrary",), VMEM_LIMIT),
        name="experts",
    )(blk_expert, n_used, xs, w1, w3, w2)


def _combine_kernel(pos_ref, ys_ref, wt_ref, h2_ref, x1_ref, g2_ref, lg_ref, lb_ref, ws1_ref,
                    ws3_ref, ws2_ref, o_ref, buf, sem, *, tt, alpha):
    def copy(t, k):
        return pltpu.make_async_copy(ys_ref.at[pl.ds(pos_ref[k, t], 1), :],
                                     buf.at[k, pl.ds(t, 1), :], sem)

    def issue(t, c):
        for k in range(TOP_K):
            copy(t, k).start(priority=k % 2)
        return c

    def drain(t, c):
        for k in range(TOP_K):
            copy(t, k).wait()
        return c

    lax.fori_loop(0, tt, issue, 0)
    half = h2_ref.shape[1]
    lo, hi = _unpack_pair(h2_ref[...])
    lo, hi = lo.astype(BF16), hi.astype(BF16)
    h1 = _dot(lo, ws1_ref[:half, :]) + _dot(hi, ws1_ref[half:, :])
    h3 = _dot(lo, ws3_ref[:half, :]) + _dot(hi, ws3_ref[half:, :])
    shared = _dot((_silu(h1) * h3).astype(BF16), ws2_ref[...])
    lax.fori_loop(0, tt, drain, 0)
    wt = wt_ref[...]
    f_lo, f_hi = shared[:, :half], shared[:, half:]
    for k in range(TOP_K):
        y_lo, y_hi = _unpack_pair(buf[k])
        f_lo = f_lo + wt[:, k:k + 1] * y_lo
        f_hi = f_hi + wt[:, k:k + 1] * y_hi
    ffn = jnp.concatenate([f_lo, f_hi], axis=1)
    o_ref[...] = _ln(alpha * x1_ref[...] + g2_ref[...] * ffn) * lg_ref[...] + lb_ref[...]


def _combine(pos_kt, ys, w_tk, h2, x1, g2, ln_g, ln_b, ws1, ws3, ws2, alpha):
    B, S, D = x1.shape
    tt = min(256, S)
    nt = S // tt
    row = lambda n: pl.BlockSpec((None, tt, n), lambda b, i: (b, i, 0))
    full = lambda a: pl.BlockSpec(a.shape, lambda b, i: (0,) * a.ndim)
    return pl.pallas_call(
        functools.partial(_combine_kernel, tt=tt, alpha=alpha),
        grid=(B, nt),
        in_specs=[pl.BlockSpec((TOP_K, tt), lambda b, i: (0, b * nt + i), memory_space=pltpu.SMEM),
                  pl.BlockSpec(memory_space=pl.ANY),
                  row(TOP_K), row(D // 2), row(D),
                  pl.BlockSpec((None, 1, D), lambda b, i: (b, 0, 0)),
                  full(ln_g), full(ln_b), full(ws1), full(ws3), full(ws2)],
        out_specs=row(D),
        out_shape=jax.ShapeDtypeStruct((B, S, D), F32),
        scratch_shapes=[pltpu.VMEM((TOP_K, tt, D // 2), I32), pltpu.SemaphoreType.DMA(())],
        compiler_params=_cp(("arbitrary", "arbitrary"), VMEM_LIMIT),
        name="combine",
    )(pos_kt, ys, w_tk, h2, x1, g2, ln_g, ln_b, ws1, ws3, ws2)


def _arrange_w_in(w):
    z = lambda n: jnp.zeros((w.shape[0], n), w.dtype)
    return jnp.concatenate([w[:, 0:1280], w[:, 1296:1552], w[:, 1552:1680], w[:, 1680:1712],
                            z(96), w[:, 1280:1296], z(112)], axis=1).astype(BF16)


def _head_layout(nope, rope_a, rope_b):
    r = nope.shape[0]
    z = jnp.zeros((r, MLA_HEADS, MLA_HEAD_PAD - MLA_NOPE - MLA_ROPE), nope.dtype)
    return jnp.concatenate([nope, rope_a, rope_b, z], axis=2).reshape(r, MLA_HEADS * MLA_HEAD_PAD)


def _arrange_mla(w_uq, w_uk):
    half = MLA_ROPE // 2
    q = w_uq.reshape(MLA_Q_LORA, MLA_HEADS, MLA_NOPE + MLA_ROPE)
    qn, q1, q2 = q[..., :MLA_NOPE], q[..., MLA_NOPE:MLA_NOPE + half], q[..., MLA_NOPE + half:]
    wq = _head_layout(qn, q1, q2)
    wq_sw = _head_layout(jnp.zeros_like(qn), q2, q1)
    kn = w_uk.reshape(MLA_KV_LORA, MLA_HEADS, MLA_NOPE)
    zk = jnp.zeros((MLA_KV_LORA, MLA_HEADS, half), w_uk.dtype)
    eye = jnp.eye(MLA_ROPE, dtype=w_uk.dtype)
    e1 = jnp.broadcast_to(eye[:, None, :half], (MLA_ROPE, MLA_HEADS, half))
    e2 = jnp.broadcast_to(eye[:, None, half:], (MLA_ROPE, MLA_HEADS, half))
    zn = jnp.zeros((MLA_ROPE, MLA_HEADS, MLA_NOPE), w_uk.dtype)
    pad = jnp.zeros((128 - MLA_ROPE, MLA_HEADS * MLA_HEAD_PAD), w_uk.dtype)
    wk = jnp.concatenate([_head_layout(kn, zk, zk), _head_layout(zn, e1, e2), pad], axis=0)
    wk_sw = jnp.concatenate([_head_layout(jnp.zeros_like(kn), zk, zk), _head_layout(zn, e2, e1), pad],
                            axis=0)
    return wq.astype(BF16), wq_sw.astype(BF16), wk.astype(BF16), wk_sw.astype(BF16)


def _rope_rows():
    half = MLA_ROPE // 2
    inv_freq = ROPE_THETA ** (-jnp.arange(0, MLA_ROPE, 2, dtype=F32) / MLA_ROPE)
    z = lambda n: jnp.zeros((n,), F32)
    frq = jnp.concatenate([z(MLA_NOPE), inv_freq, inv_freq, z(MLA_HEAD_PAD - MLA_NOPE - MLA_ROPE)])
    sgn = jnp.concatenate([z(MLA_NOPE), -jnp.ones((half,), F32), jnp.ones((half,), F32),
                           z(MLA_HEAD_PAD - MLA_NOPE - MLA_ROPE)])
    return frq[None, :], sgn[None, :]


def _block_diag_pool(w_pool):
    G, C, _ = w_pool.shape
    out = jnp.zeros((G * C, G * C), w_pool.dtype)
    for g in range(G):
        out = out.at[g * C:(g + 1) * C, g * C:(g + 1) * C].set(w_pool[g])
    return out.astype(BF16)


def kernel(x, c, positions, w_ada, b_ada, w_in, w_pool, s_pool, conv_w, conv_b, gate_b, gn_w, g_q,
           g_kv, w_uq, w_uk, w_uv, w_out, ln1_g, ln1_b, w_router, e_bias, w1, w3, w2, ws1, ws3, ws2,
           ln2_g, ln2_b):
    B, S, D = x.shape
    depth = w_in.shape[0]
    T = B * S
    H = ML_HEADS
    alpha = float((2 * depth) ** 0.25)
    n_assign = T * TOP_K
    n_blocks = n_assign // EXPERT_BLOCK + N_EXPERTS
    n_rows = n_blocks * EXPERT_BLOCK

    ada = _ada(c, w_ada, b_ada)
    frq, sgn = _rope_rows()
    pos3 = positions.reshape(B, S, 1)
    qk_scale = jnp.concatenate([jnp.ones((1, ML_WIDTH), F32),
                                jnp.full((1, ML_WIDTH), ML_DH ** -0.5, F32)], axis=1)
    ones_row = lambda n: jnp.ones((B, n, 1, S), BF16)
    zero_rows = lambda n, r: jnp.zeros((B, n, r, S), BF16)

    for l in range(depth):
        sh1, sc1, g1, sh2, sc2, g2 = [a.reshape(B, 1, D) for a in jnp.split(ada[l], 6, axis=-1)]
        u_pool, u_qk, u_v, u_o, u_dq, u_kv, u_gate = _inproj(x, sc1, sh1, _arrange_w_in(w_in[l]))
        y_pool = _pool(u_pool, _block_diag_pool(w_pool[l]), s_pool[l][None, :])

        qk_act = _conv(u_qk, conv_w[l], conv_b[l][None, :], qk_scale)
        heads_t = lambda a: a.reshape(B, S, H, ML_DH).transpose(0, 2, 3, 1)
        gates = u_gate[:, :, :4 * H].reshape(B, S, 4, H)
        vt_ml = jnp.concatenate([heads_t(u_v), ones_row(H), zero_rows(H, ML_VT_ROWS - ML_DH - 1)],
                                axis=2)
        y_ml = _mlstm(qk_act[:, :, ML_WIDTH:].reshape(B, S, H, ML_DH).transpose(0, 2, 1, 3),
                      heads_t(qk_act[:, :, :ML_WIDTH]), vt_ml, heads_t(u_o),
                      gates.transpose(0, 3, 1, 2),
                      gates.transpose(0, 3, 2, 1),
                      gate_b[l].reshape(4, H).T.reshape(H, 1, 4),
                      gate_b[l].reshape(4, H).T.reshape(H, 4, 1),
                      gn_w[l].reshape(H, ML_DH, 1))
        y_ml = y_ml.transpose(0, 3, 1, 2).reshape(B, S, ML_WIDTH)

        wq, wq_sw, wk, wk_sw = _arrange_mla(w_uq[l], w_uk[l])
        qh, kh, vh = _mla_proj(u_dq, u_kv, pos3, frq, sgn, g_q[l][None, :], g_kv[l][None, :],
                               wq, wq_sw, wk, wk_sw, w_uv[l].astype(BF16))
        y_mla = _attn(qh, kh, vh.transpose(0, 2, 1)).transpose(0, 2, 1)

        x1, h2, logits_t = _outproj(y_pool, y_ml, y_mla, x, g1, ln1_g[l][None, :], ln1_b[l][None, :],
                                    sc2, sh2, w_out[l].astype(BF16), w_router[l].T.astype(BF16), alpha)
        idx_t, w_t = _route(logits_t, e_bias[l][:, None])
        rank_t, counts = _rank(idx_t)
        counts = counts[:, 0].astype(I32)
        padded = (counts + EXPERT_BLOCK - 1) // EXPERT_BLOCK * EXPERT_BLOCK
        pend = jnp.cumsum(padded)
        pstart = pend - padded
        n_used = (pend[-1:] // EXPERT_BLOCK).astype(I32)
        blk_start = jnp.arange(n_blocks, dtype=I32) * EXPERT_BLOCK
        blk_expert = jnp.minimum(jnp.sum((pend[None, :] <= blk_start[:, None]).astype(I32), axis=1),
                                 N_EXPERTS - 1)
        pos_kt = _pos(idx_t, rank_t, pstart.astype(F32)[:, None]).transpose(1, 0, 2).reshape(TOP_K, T)
        xs = _dispatch(pos_kt, h2.reshape(T, D // 2), n_rows)
        ys = _experts(blk_expert, n_used, xs, w1, w3, w2, l)
        x = _combine(pos_kt, ys, w_t.transpose(0, 2, 1), h2, x1, g2, ln2_g[l][None, :],
                     ln2_b[l][None, :], ws1[l].astype(BF16), ws3[l].astype(BF16),
                     ws2[l].astype(BF16), alpha)
    return x
```

```python
import functools

import jax
import jax.numpy as jnp
import numpy as np
from jax import lax
from jax.experimental import pallas as pl
from jax.experimental.pallas import tpu as pltpu

F32 = jnp.float32
BF16 = jnp.bfloat16
I32 = jnp.int32

D_MODEL = 1024
POOL_WINDOWS = (2, 4, 8, 16)
POOL_GROUP_DIM = 64
POOL_WIDTH = 256
ML_HEADS = 4
ML_DH = 64
ML_WIDTH = 256
ML_CONV = 5
ML_VT_ROWS = 80
MLA_HEADS = 8
MLA_NOPE = 64
MLA_ROPE = 32
MLA_V = 64
MLA_Q_LORA = 256
MLA_KV_LORA = 128
MLA_HEAD_PAD = 128
ROPE_THETA = 10000.0
N_EXPERTS = 256
TOP_K = 8
N_GROUPS = 8
TOPK_GROUPS = 4
GROUP_SIZE = N_EXPERTS // N_GROUPS
D_EXPERT = 256
ROUTED_SCALE = 2.5
LN_EPS = 1e-5
RMS_EPS = 1e-6
NEG_BIG = -1e30
NEG_INF = float("-inf")

EXPERT_BLOCK = 256
VMEM_LIMIT = 56 * 1024 * 1024


def _cp(sem, vmem=None):
    return pltpu.CompilerParams(dimension_semantics=sem, vmem_limit_bytes=vmem)


def _ln(x):
    mu = jnp.mean(x, axis=-1, keepdims=True)
    xc = x - mu
    var = jnp.mean(xc * xc, axis=-1, keepdims=True)
    return xc * lax.rsqrt(var + LN_EPS)


def _silu(x):
    return x * jax.nn.sigmoid(x)


def _rms(x, g):
    return x * lax.rsqrt(jnp.mean(x * x, axis=-1, keepdims=True) + RMS_EPS) * g


def _dot(a, b):
    return jnp.dot(a, b, preferred_element_type=F32)


def _dot_nt(a, b):
    return lax.dot_general(a, b, (((1,), (1,)), ((), ())), preferred_element_type=F32)


def _dot_tn(a, b):
    return lax.dot_general(a, b, (((0,), (0,)), ((), ())), preferred_element_type=F32)


def _pack_pair(lo, hi):
    lo_b = lax.bitcast_convert_type(lo.astype(BF16).astype(F32), I32)
    hi_b = lax.bitcast_convert_type(hi.astype(BF16).astype(F32), I32)
    return jnp.bitwise_or(hi_b, lax.shift_right_logical(lo_b, 16))


def _unpack_pair(w):
    lo = lax.bitcast_convert_type(lax.shift_left(w, 16), F32)
    hi = lax.bitcast_convert_type(jnp.bitwise_and(w, -65536), F32)
    return lo, hi


def _ada_kernel(c_ref, w_ref, b_ref, o_ref):
    ca = _silu(c_ref[...]).astype(BF16)
    o_ref[...] = _dot(ca, w_ref[...].astype(BF16)) + b_ref[...]


def _ada(c, w_ada, b_ada):
    L, D, N = w_ada.shape
    B = c.shape[0]
    tn = 1536
    return pl.pallas_call(
        _ada_kernel,
        grid=(L, N // tn),
        in_specs=[pl.BlockSpec((B, D), lambda l, j: (0, 0)),
                  pl.BlockSpec((None, D, tn), lambda l, j: (l, 0, j)),
                  pl.BlockSpec((None, 1, tn), lambda l, j: (l, 0, j))],
        out_specs=pl.BlockSpec((None, B, tn), lambda l, j: (l, 0, j)),
        out_shape=jax.ShapeDtypeStruct((L, B, N), F32),
        compiler_params=_cp(("parallel", "parallel"), VMEM_LIMIT),
        name="ada",
    )(c, w_ada, b_ada.reshape(L, 1, N))


def _inproj_kernel(x_ref, sc_ref, sh_ref, w_ref, pool_ref, qk_ref, v_ref, o_ref, dq_ref,
                   kv_ref, gate_ref):
    h = (_ln(x_ref[...]) * (1.0 + sc_ref[...]) + sh_ref[...]).astype(BF16)
    pool_ref[...] = _dot(h, w_ref[:, 0:256])
    qk_ref[...] = _dot(h, w_ref[:, 256:768])
    v_ref[...] = _dot(h, w_ref[:, 768:1024]).astype(BF16)
    o_ref[...] = _dot(h, w_ref[:, 1024:1280])
    dq_ref[...] = _dot(h, w_ref[:, 1280:1536])
    kv_ref[...] = _dot(h, w_ref[:, 1536:1792])
    gate_ref[...] = _dot(h, w_ref[:, 1792:1920])


def _inproj(x, sc, sh, w):
    B, S, D = x.shape
    tm = min(512, S)
    widths = (256, 512, 256, 256, 256, 256, 128)
    dtypes = (F32, F32, BF16, F32, F32, F32, F32)
    row = lambda n: pl.BlockSpec((None, tm, n), lambda b, i: (b, i, 0))
    return pl.pallas_call(
        _inproj_kernel,
        grid=(B, S // tm),
        in_specs=[row(D),
                  pl.BlockSpec((None, 1, D), lambda b, i: (b, 0, 0)),
                  pl.BlockSpec((None, 1, D), lambda b, i: (b, 0, 0)),
                  pl.BlockSpec(w.shape, lambda b, i: (0, 0))],
        out_specs=[row(n) for n in widths],
        out_shape=[jax.ShapeDtypeStruct((B, S, n), dt) for n, dt in zip(widths, dtypes)],
        compiler_params=_cp(("parallel", "parallel"), VMEM_LIMIT),
        name="inproj",
    )(x, sc, sh, w)


def _shift_dn(a, k, row):
    return jnp.where(row >= k, pltpu.roll(a, k, 0), 0.0)


def _shift_up(a, k, row):
    n = a.shape[0]
    return jnp.where(row < n - k, pltpu.roll(a, n - k, 0), 0.0)


def _pool_kernel(u_ref, w_ref, s_ref, y_ref):
    x = u_ref[...]
    S, C = x.shape
    row = lax.broadcasted_iota(I32, (S, C), 0)
    lane = lax.broadcasted_iota(I32, (S, C), 1)
    rowf = row.astype(F32)
    p1, f1 = x, x
    p2 = p1 + _shift_dn(p1, 1, row)
    f2 = f1 + _shift_up(f1, 1, row)
    p4 = p2 + _shift_dn(p2, 2, row)
    f4 = f2 + _shift_up(f2, 2, row)
    p8 = p4 + _shift_dn(p4, 4, row)
    f8 = f4 + _shift_up(f4, 4, row)
    wins = []
    for half, p, f in ((1, p1, f1), (2, p2, f2), (4, p4, f4), (8, p8, f8)):
        total = _shift_dn(p, 1, row) + f
        cnt = jnp.minimum(rowf + half, float(S)) - jnp.maximum(rowf - half, 0.0)
        wins.append(total / cnt)
    pooled = jnp.where(lane < 64, wins[0],
                       jnp.where(lane < 128, wins[1], jnp.where(lane < 192, wins[2], wins[3])))
    d = (pooled - x).astype(BF16)
    y_ref[...] = (_dot(d, w_ref[...]) * s_ref[...]).astype(BF16)


def _pool(u_pool, w_bd, s_pool):
    B, S, C = u_pool.shape
    return pl.pallas_call(
        _pool_kernel,
        grid=(B,),
        in_specs=[pl.BlockSpec((None, S, C), lambda b: (b, 0, 0)),
                  pl.BlockSpec((C, C), lambda b: (0, 0)),
                  pl.BlockSpec((1, C), lambda b: (0, 0))],
        out_specs=pl.BlockSpec((None, S, C), lambda b: (b, 0, 0)),
        out_shape=jax.ShapeDtypeStruct((B, S, C), BF16),
        compiler_params=_cp(("parallel",), VMEM_LIMIT),
        name="pool",
    )(u_pool, w_bd, s_pool)


def _conv_kernel(u_ref, w_ref, b_ref, scale_ref, y_ref):
    x = u_ref[...]
    S, C = x.shape
    row = lax.broadcasted_iota(I32, (S, C), 0)
    y = (_shift_dn(x, 2, row) * w_ref[0:1, :] + _shift_dn(x, 1, row) * w_ref[1:2, :]
         + x * w_ref[2:3, :] + _shift_up(x, 1, row) * w_ref[3:4, :]
         + _shift_up(x, 2, row) * w_ref[4:5, :] + b_ref[...])
    y_ref[...] = (_silu(y) * scale_ref[...]).astype(BF16)


def _conv(u_qk, conv_w, conv_b, scale):
    B, S, C = u_qk.shape
    return pl.pallas_call(
        _conv_kernel,
        grid=(B,),
        in_specs=[pl.BlockSpec((None, S, C), lambda b: (b, 0, 0)),
                  pl.BlockSpec((ML_CONV, C), lambda b: (0, 0)),
                  pl.BlockSpec((1, C), lambda b: (0, 0)),
                  pl.BlockSpec((1, C), lambda b: (0, 0))],
        out_specs=pl.BlockSpec((None, S, C), lambda b: (b, 0, 0)),
        out_shape=jax.ShapeDtypeStruct((B, S, C), BF16),
        compiler_params=_cp(("parallel",), VMEM_LIMIT),
        name="mlstm_conv",
    )(u_qk, conv_w, conv_b, scale)


def _log_sigmoid(x):
    return jnp.minimum(x, 0.0) - jnp.log(1.0 + jnp.exp(-jnp.abs(x)))


def _split3(x):
    a = x.astype(BF16)
    r = x - a.astype(F32)
    b = r.astype(BF16)
    c = (r - b.astype(F32)).astype(BF16)
    return a, b, c


def _mlstm_chunk(s, k, qt, vt, c_col, b_row, i_row, b_last, allowed, ct, m_prev):
    logw = jnp.where(allowed, c_col + b_row, NEG_INF)
    m_inter = b_row + m_prev
    m_t = jnp.maximum(m_inter, jnp.max(logw, axis=0, keepdims=True))
    w_intra = jnp.exp(logw - m_t)
    w_inter = jnp.exp(m_inter - m_t)
    qk = (s * w_intra).astype(BF16)
    nd = _dot(vt, qk) + w_inter * _dot(ct.astype(BF16), qt)
    den = jnp.maximum(jnp.abs(nd[ML_DH:ML_DH + 1, :]), jnp.exp(-m_t))
    h = nd[:ML_DH, :] / den
    g = b_last - b_row + i_row
    m_new = jnp.maximum(b_last + m_prev, jnp.max(g, axis=1, keepdims=True))
    w_k = jnp.exp(g - m_new)
    decay = jnp.exp(b_last + m_prev - m_new)
    ct_new = decay * ct + _dot((vt.astype(F32) * w_k).astype(BF16), k)
    return h, ct_new, m_new


def _mlstm_kernel(k_ref, qt_ref, vt_ref, uo_ref, gc_ref, gr_ref, bc_ref, br_ref, gn_ref, y_ref,
                  hf_ref, hb_ref, *, chunk):
    S = k_ref.shape[0]
    L = chunk
    nc = S // L
    ri = lax.broadcasted_iota(I32, (L, L), 0)
    ci = lax.broadcasted_iota(I32, (L, L), 1)
    lower = ci <= ri
    upper = ci >= ri
    tril = jnp.where(lower, 1.0, 0.0).astype(BF16)
    triu = jnp.where(upper, 1.0, 0.0).astype(BF16)
    bias_c = bc_ref[...]
    bias_r = br_ref[...]

    def prep(r0, forward):
        gc = gc_ref[pl.ds(r0, L), :] + bias_c
        gr = gr_ref[:, pl.ds(r0, L)] + bias_r
        c1, c2, c3 = _split3(_log_sigmoid(gc))
        r1, r2, r3 = _split3(_log_sigmoid(gr))
        k = k_ref[pl.ds(r0, L), :]
        qt = qt_ref[:, pl.ds(r0, L)]
        s = _dot(k, qt)
        if forward:
            cum_c = (_dot(tril, c1) + _dot(tril, c2) + _dot(tril, c3))[:, 1:2]
            cum_r = (_dot(r1, triu) + _dot(r2, triu) + _dot(r3, triu))[1:2, :]
            return s, k, qt, gc[:, 0:1] - cum_c, cum_r, gr[0:1, :], cum_r[:, L - 1:L]
        cum_c = (_dot(triu, c1) + _dot(triu, c2) + _dot(triu, c3))[:, 3:4]
        cum_r = (_dot(r1, tril) + _dot(r2, tril) + _dot(r3, tril))[3:4, :]
        return s, k, qt, gc[:, 2:3] - cum_c, cum_r, gr[2:3, :], cum_r[:, 0:1]

    def step(i, carry):
        cf, mf, cb, mb = carry
        rf = pl.multiple_of(i * L, L)
        rb = pl.multiple_of((nc - 1 - i) * L, L)
        pf = prep(rf, True)
        pb = prep(rb, False)
        h, cf, mf = _mlstm_chunk(*pf[:3], vt_ref[:, pl.ds(rf, L)], *pf[3:], upper, cf, mf)
        hf_ref[:, pl.ds(rf, L)] = h
        h, cb, mb = _mlstm_chunk(*pb[:3], vt_ref[:, pl.ds(rb, L)], *pb[3:], lower, cb, mb)
        hb_ref[:, pl.ds(rb, L)] = h
        return cf, mf, cb, mb

    c0 = jnp.zeros((ML_VT_ROWS, ML_DH), F32)
    m0 = jnp.full((1, 1), NEG_BIG, F32)
    lax.fori_loop(0, nc, step, (c0, m0, c0, m0))
    h = hf_ref[...] + hb_ref[...]
    mu = jnp.mean(h, axis=0, keepdims=True)
    hc = h - mu
    var = jnp.mean(hc * hc, axis=0, keepdims=True)
    hn = hc * lax.rsqrt(var + LN_EPS)
    y_ref[...] = (jax.nn.sigmoid(uo_ref[...]) * (hn * gn_ref[...])).astype(BF16)


def _mlstm(k_heads, qt_heads, vt_ext, uot_heads, g_col, g_row, b_col, b_row, gn_w):
    B, H, S, dh = k_heads.shape
    L = min(256, S)
    tr = lambda n: pl.BlockSpec((None, None, n, S), lambda b, h: (b, h, 0, 0))
    return pl.pallas_call(
        functools.partial(_mlstm_kernel, chunk=L),
        grid=(B, H),
        in_specs=[pl.BlockSpec((None, None, S, dh), lambda b, h: (b, h, 0, 0)),
                  tr(dh), tr(ML_VT_ROWS), tr(dh),
                  pl.BlockSpec((None, None, S, 4), lambda b, h: (b, h, 0, 0)),
                  tr(4),
                  pl.BlockSpec((None, 1, 4), lambda b, h: (h, 0, 0)),
                  pl.BlockSpec((None, 4, 1), lambda b, h: (h, 0, 0)),
                  pl.BlockSpec((None, dh, 1), lambda b, h: (h, 0, 0))],
        out_specs=tr(dh),
        out_shape=jax.ShapeDtypeStruct((B, H, dh, S), BF16),
        scratch_shapes=[pltpu.VMEM((dh, S), F32), pltpu.VMEM((dh, S), F32)],
        compiler_params=_cp(("parallel", "parallel"), VMEM_LIMIT),
        name="mlstm_scan",
    )(k_heads, qt_heads, vt_ext, uot_heads, g_col, g_row, b_col, b_row, gn_w)


def _mla_proj_kernel(dq_ref, kv_ref, pos_ref, frq_ref, sgn_ref, gq_ref, gkv_ref, wq_ref, wqs_ref,
                     wk_ref, wks_ref, wv_ref, q_ref, k_ref, v_ref, *, scale):
    ang = pos_ref[...].astype(F32) * frq_ref[...]
    cos = jnp.cos(ang)
    sin = jnp.sin(ang) * sgn_ref[...]
    qn = _rms(dq_ref[...], gq_ref[...]).astype(BF16)
    a = _dot(qn, wq_ref[...])
    a_sw = _dot(qn, wqs_ref[...])
    kv = kv_ref[...]
    ckv = _rms(kv[:, :MLA_KV_LORA], gkv_ref[...])
    kin = jnp.concatenate([ckv, kv[:, MLA_KV_LORA:]], axis=1).astype(BF16)
    ak = _dot(kin, wk_ref[...])
    ak_sw = _dot(kin, wks_ref[...])
    for h in range(MLA_HEADS):
        sl = slice(h * MLA_HEAD_PAD, (h + 1) * MLA_HEAD_PAD)
        q_ref[:, sl] = ((a[:, sl] * cos + a_sw[:, sl] * sin) * scale).astype(BF16)
        k_ref[:, sl] = (ak[:, sl] * cos + ak_sw[:, sl] * sin).astype(BF16)
    v_ref[...] = _dot(ckv.astype(BF16), wv_ref[...]).astype(BF16)


def _mla_proj(u_dq, u_kv, pos, frq, sgn, g_q, g_kv, wq, wqs, wk, wks, wv):
    B, S, _ = u_dq.shape
    tm = min(512, S)
    hw = MLA_HEADS * MLA_HEAD_PAD
    vw = MLA_HEADS * MLA_V
    row = lambda n: pl.BlockSpec((None, tm, n), lambda b, i: (b, i, 0))
    full = lambda a: pl.BlockSpec(a.shape, lambda b, i: (0,) * a.ndim)
    scale = (MLA_NOPE + MLA_ROPE) ** -0.5 * float(np.log2(np.e))
    return pl.pallas_call(
        functools.partial(_mla_proj_kernel, scale=scale),
        grid=(B, S // tm),
        in_specs=[row(MLA_Q_LORA), row(256), row(1), full(frq), full(sgn), full(g_q), full(g_kv),
                  full(wq), full(wqs), full(wk), full(wks), full(wv)],
        out_specs=[row(hw), row(hw), row(vw)],
        out_shape=[jax.ShapeDtypeStruct((B, S, hw), BF16),
                   jax.ShapeDtypeStruct((B, S, hw), BF16),
                   jax.ShapeDtypeStruct((B, S, vw), BF16)],
        compiler_params=_cp(("parallel", "parallel"), VMEM_LIMIT),
        name="mla_proj",
    )(u_dq, u_kv, pos, frq, sgn, g_q, g_kv, wq, wqs, wk, wks, wv)


def _attn_kernel(q_ref, k_ref, vt_ref, o_ref, *, tk):
    tq = q_ref.shape[0]
    S = k_ref.shape[0]
    nk = S // tk
    def scores(h, j):
        k = k_ref[j * tk:(j + 1) * tk, h * MLA_HEAD_PAD:(h + 1) * MLA_HEAD_PAD]
        return _dot_nt(k, q_ref[:, h * MLA_HEAD_PAD:(h + 1) * MLA_HEAD_PAD])

    m = [jnp.full((1, tq), NEG_INF, F32)] * 2
    l = [jnp.zeros((1, tq), F32)] * 2
    acc = [jnp.zeros((MLA_V, tq), F32)] * 2
    s = [scores(0, 0), scores(1, 0)]
    for j in range(nk):
        for h in range(2):
            s_cur = s[h]
            if j + 1 < nk:
                s[h] = scores(h, j + 1)
            m_new = jnp.maximum(m[h], jnp.max(s_cur, axis=0, keepdims=True))
            alpha = jnp.exp2(m[h] - m_new)
            p = jnp.exp2(s_cur - m_new)
            l[h] = alpha * l[h] + jnp.sum(p, axis=0, keepdims=True)
            vt = vt_ref[h * MLA_V:(h + 1) * MLA_V, j * tk:(j + 1) * tk]
            acc[h] = alpha * acc[h] + _dot(vt, p.astype(BF16))
            m[h] = m_new
    for h in range(2):
        o_ref[h * MLA_V:(h + 1) * MLA_V, :] = (acc[h] / l[h]).astype(BF16)


def _attn(qh, kh, vt):
    B, S, _ = qh.shape
    tq = min(512, S)
    tk = min(256, S)
    pw = 2 * MLA_HEAD_PAD
    return pl.pallas_call(
        functools.partial(_attn_kernel, tk=tk),
        grid=(B, MLA_HEADS // 2, S // tq),
        in_specs=[pl.BlockSpec((None, tq, pw), lambda b, p, i: (b, i, p)),
                  pl.BlockSpec((None, S, pw), lambda b, p, i: (b, 0, p)),
                  pl.BlockSpec((None, 2 * MLA_V, S), lambda b, p, i: (b, p, 0))],
        out_specs=pl.BlockSpec((None, 2 * MLA_V, tq), lambda b, p, i: (b, p, i)),
        out_shape=jax.ShapeDtypeStruct((B, MLA_HEADS * MLA_V, S), BF16),
        compiler_params=_cp(("parallel", "parallel", "parallel"), VMEM_LIMIT),
        name="mla_attn",
    )(qh, kh, vt)


def _outproj_kernel(yp_ref, ym_ref, ya_ref, x_ref, g1_ref, lg_ref, lb_ref, sc_ref, sh_ref, wo_ref,
                    wr_ref, x1_ref, h2_ref, lt_ref, *, alpha):
    mix = (_dot(yp_ref[...], wo_ref[0:256, :]) + _dot(ym_ref[...], wo_ref[256:512, :])
           + _dot(ya_ref[...], wo_ref[512:1024, :]))
    x1 = _ln(alpha * x_ref[...] + g1_ref[...] * mix) * lg_ref[...] + lb_ref[...]
    x1_ref[...] = x1
    h2 = _ln(x1) * (1.0 + sc_ref[...]) + sh_ref[...]
    half = h2.shape[1] // 2
    h2_ref[...] = _pack_pair(h2[:, :half], h2[:, half:])
    lt_ref[...] = _dot_nt(wr_ref[...], h2.astype(BF16))


def _outproj(y_pool, y_ml, y_mla, x, g1, ln_g, ln_b, sc2, sh2, w_out, w_router_t, alpha):
    B, S, D = x.shape
    tm = min(512, S)
    row = lambda n: pl.BlockSpec((None, tm, n), lambda b, i: (b, i, 0))
    per_b = pl.BlockSpec((None, 1, D), lambda b, i: (b, 0, 0))
    vec = pl.BlockSpec((1, D), lambda b, i: (0, 0))
    return pl.pallas_call(
        functools.partial(_outproj_kernel, alpha=alpha),
        grid=(B, S // tm),
        in_specs=[row(256), row(256), row(512), row(D), per_b, vec, vec, per_b, per_b,
                  pl.BlockSpec((D, D), lambda b, i: (0, 0)),
                  pl.BlockSpec((N_EXPERTS, D), lambda b, i: (0, 0))],
        out_specs=[row(D), row(D // 2), pl.BlockSpec((None, N_EXPERTS, tm), lambda b, i: (b, 0, i))],
        out_shape=[jax.ShapeDtypeStruct((B, S, D), F32), jax.ShapeDtypeStruct((B, S, D // 2), I32),
                   jax.ShapeDtypeStruct((B, N_EXPERTS, S), F32)],
        compiler_params=_cp(("parallel", "parallel"), VMEM_LIMIT),
        name="outproj",
    )(y_pool, y_ml, y_mla, x, g1, ln_g, ln_b, sc2, sh2, w_out, w_router_t)


def _first_max(v, iota, n):
    m = jnp.max(v, axis=0, keepdims=True)
    first = jnp.min(jnp.where(v == m, iota, n), axis=0, keepdims=True)
    return m, first


def _route_kernel(lt_ref, eb_ref, idx_ref, w_ref):
    scores = jax.nn.sigmoid(lt_ref[...])
    sel = scores + eb_ref[...]
    E, tt = sel.shape
    gi = lax.broadcasted_iota(I32, (GROUP_SIZE, tt), 0)
    groups = [sel[g * GROUP_SIZE:(g + 1) * GROUP_SIZE, :] for g in range(N_GROUPS)]
    gs = []
    for grp in groups:
        m1, f1 = _first_max(grp, gi, GROUP_SIZE)
        m2 = jnp.max(jnp.where(gi == f1, NEG_INF, grp), axis=0, keepdims=True)
        gs.append(m1 + m2)
    chosen = [jnp.zeros((1, tt), F32) for _ in range(N_GROUPS)]
    for _ in range(TOPK_GROUPS):
        m = functools.reduce(jnp.maximum, gs)
        first = functools.reduce(
            jnp.minimum, [jnp.where(gs[g] == m, g, N_GROUPS) for g in range(N_GROUPS)])
        for g in range(N_GROUPS):
            hit = first == g
            chosen[g] = jnp.where(hit, 1.0, chosen[g])
            gs[g] = jnp.where(hit, NEG_INF, gs[g])
    cand = jnp.concatenate(
        [jnp.where(chosen[g] > 0.5, groups[g], NEG_INF) for g in range(N_GROUPS)], axis=0)
    ei = lax.broadcasted_iota(I32, (E, tt), 0)
    ws = []
    for k in range(TOP_K):
        _, f = _first_max(cand, ei, E)
        hit = ei == f
        idx_ref[k:k + 1, :] = f
        ws.append(jnp.sum(jnp.where(hit, scores, 0.0), axis=0, keepdims=True))
        cand = jnp.where(hit, NEG_INF, cand)
    norm = ROUTED_SCALE / functools.reduce(jnp.add, ws)
    for k in range(TOP_K):
        w_ref[k:k + 1, :] = ws[k] * norm


def _route(logits_t, e_bias):
    B, E, S = logits_t.shape
    tt = min(512, S)
    return pl.pallas_call(
        _route_kernel,
        grid=(B, S // tt),
        in_specs=[pl.BlockSpec((None, E, tt), lambda b, i: (b, 0, i)),
                  pl.BlockSpec((E, 1), lambda b, i: (0, 0))],
        out_specs=[pl.BlockSpec((None, TOP_K, tt), lambda b, i: (b, 0, i)),
                   pl.BlockSpec((None, TOP_K, tt), lambda b, i: (b, 0, i))],
        out_shape=[jax.ShapeDtypeStruct((B, TOP_K, S), I32),
                   jax.ShapeDtypeStruct((B, TOP_K, S), F32)],
        compiler_params=_cp(("parallel", "parallel"), VMEM_LIMIT),
        name="route",
    )(logits_t, e_bias)


def _rank_kernel(idx_ref, rank_ref, cnt_ref, carry_ref):
    first = jnp.logical_and(pl.program_id(0) == 0, pl.program_id(1) == 0)

    @pl.when(first)
    def _():
        carry_ref[...] = jnp.zeros_like(carry_ref)

    K, tt = idx_ref.shape
    E = carry_ref.shape[0]
    ei = lax.broadcasted_iota(I32, (E, tt), 0)
    ri = lax.broadcasted_iota(I32, (tt, tt), 0)
    ci = lax.broadcasted_iota(I32, (tt, tt), 1)
    before = jnp.where(ri < ci, 1.0, 0.0).astype(BF16)
    base = carry_ref[...]
    for k in range(K):
        hit = ei == idx_ref[k:k + 1, :]
        onehot = jnp.where(hit, 1.0, 0.0)
        prefix = _dot(onehot.astype(BF16), before)
        rank = jnp.sum(jnp.where(hit, base + prefix, 0.0), axis=0, keepdims=True)
        rank_ref[k:k + 1, :] = rank.astype(I32)
        base = base + jnp.sum(onehot, axis=1, keepdims=True)
    carry_ref[...] = base
    cnt_ref[...] = base


def _rank(idx_t):
    B, K, S = idx_t.shape
    tt = min(256, S)
    return pl.pallas_call(
        _rank_kernel,
        grid=(B, S // tt),
        in_specs=[pl.BlockSpec((None, K, tt), lambda b, i: (b, 0, i))],
        out_specs=[pl.BlockSpec((None, K, tt), lambda b, i: (b, 0, i)),
                   pl.BlockSpec((N_EXPERTS, 1), lambda b, i: (0, 0))],
        out_shape=[jax.ShapeDtypeStruct((B, K, S), I32),
                   jax.ShapeDtypeStruct((N_EXPERTS, 1), F32)],
        scratch_shapes=[pltpu.VMEM((N_EXPERTS, 1), F32)],
        compiler_params=_cp(("arbitrary", "arbitrary"), VMEM_LIMIT),
        name="rank",
    )(idx_t)


def _pos_kernel(idx_ref, rank_ref, start_ref, pos_ref):
    K, tt = idx_ref.shape
    ei = lax.broadcasted_iota(I32, (N_EXPERTS, tt), 0)
    start = start_ref[...]
    for k in range(K):
        hit = ei == idx_ref[k:k + 1, :]
        off = jnp.sum(jnp.where(hit, start, 0.0), axis=0, keepdims=True)
        pos_ref[k:k + 1, :] = rank_ref[k:k + 1, :] + off.astype(I32)


def _pos(idx_t, rank_t, pstart):
    B, K, S = idx_t.shape
    tt = min(512, S)
    spec = pl.BlockSpec((None, K, tt), lambda b, i: (b, 0, i))
    return pl.pallas_call(
        _pos_kernel,
        grid=(B, S // tt),
        in_specs=[spec, spec, pl.BlockSpec((N_EXPERTS, 1), lambda b, i: (0, 0))],
        out_specs=spec,
        out_shape=jax.ShapeDtypeStruct((B, K, S), I32),
        compiler_params=_cp(("parallel", "parallel"), VMEM_LIMIT),
        name="slot",
    )(idx_t, rank_t, pstart)


def _dispatch_kernel(pos_ref, h_ref, xs_ref, sem, *, tt):
    def copy(t, k):
        return pltpu.make_async_copy(h_ref.at[pl.ds(t, 1), :],
                                     xs_ref.at[pl.ds(pos_ref[k, t], 1), :], sem)

    for t in range(tt):
        for k in range(TOP_K):
            copy(t, k).start(priority=k % 2)
    for t in range(tt):
        for k in range(TOP_K):
            copy(t, k).wait()


def _dispatch(pos_kt, h2_flat, n_rows):
    T, W = h2_flat.shape
    tt = 256
    return pl.pallas_call(
        functools.partial(_dispatch_kernel, tt=tt),
        grid=(T // tt,),
        in_specs=[pl.BlockSpec((TOP_K, tt), lambda i: (0, i), memory_space=pltpu.SMEM),
                  pl.BlockSpec((tt, W), lambda i: (i, 0))],
        out_specs=pl.BlockSpec(memory_space=pl.ANY),
        out_shape=jax.ShapeDtypeStruct((n_rows, W), I32),
        scratch_shapes=[pltpu.SemaphoreType.DMA(())],
        compiler_params=pltpu.CompilerParams(dimension_semantics=("arbitrary",),
                                             has_side_effects=True),
        name="dispatch",
    )(pos_kt, h2_flat)


def _expert_kernel(be_ref, nu_ref, xs_ref, w1_ref, w3_ref, w2_ref, ys_ref, w1b, w3b, w2b):
    b = pl.program_id(0)
    prev = be_ref[jnp.maximum(b - 1, 0)]
    fresh = jnp.logical_or(b == 0, be_ref[b] != prev)

    @pl.when(jnp.logical_and(fresh, b < nu_ref[0]))
    def _():
        w1b[...] = w1_ref[...].astype(BF16)
        w3b[...] = w3_ref[...].astype(BF16)
        w2b[...] = w2_ref[...].astype(BF16)

    @pl.when(b < nu_ref[0])
    def _():
        half = xs_ref.shape[1]
        lo, hi = _unpack_pair(xs_ref[...])
        lo, hi = lo.astype(BF16), hi.astype(BF16)
        h1 = _dot(lo, w1b[:half, :]) + _dot(hi, w1b[half:, :])
        h3 = _dot(lo, w3b[:half, :]) + _dot(hi, w3b[half:, :])
        y = _dot((_silu(h1) * h3).astype(BF16), w2b[...])
        ys_ref[...] = _pack_pair(y[:, :half], y[:, half:])


def _experts(blk_expert, n_used, xs, w1, w3, w2, layer):
    P, W = xs.shape
    D = 2 * W
    nb = P // EXPERT_BLOCK
    last = lambda b, be, nu: jnp.minimum(b, nu[0] - 1)
    wspec = lambda r, c: pl.BlockSpec((None, None, r, c),
                                      lambda b, be, nu: (layer, be[last(b, be, nu)], 0, 0))
    grid_spec = pltpu.PrefetchScalarGridSpec(
        num_scalar_prefetch=2,
        grid=(nb,),
        in_specs=[pl.BlockSpec((EXPERT_BLOCK, W), lambda b, be, nu: (last(b, be, nu), 0)),
                  wspec(D, D_EXPERT), wspec(D, D_EXPERT), wspec(D_EXPERT, D)],
        out_specs=pl.BlockSpec((EXPERT_BLOCK, W), lambda b, be, nu: (last(b, be, nu), 0)),
        scratch_shapes=[pltpu.VMEM((D, D_EXPERT), BF16), pltpu.VMEM((D, D_EXPERT), BF16),
                        pltpu.VMEM((D_EXPERT, D), BF16)],
    )
    return pl.pallas_call(
        _expert_kernel,
        grid_spec=grid_spec,
        out_shape=jax.ShapeDtypeStruct((P, W), I32),
        compiler_params=_cp(("arbitrary",), VMEM_LIMIT),
        name="experts",
    )(blk_expert, n_used, xs, w1, w3, w2)


def _combine_kernel(pos_ref, ys_ref, wt_ref, h2_ref, x1_ref, g2_ref, lg_ref, lb_ref, ws1_ref,
                    ws3_ref, ws2_ref, o_ref, buf, sem, *, tt, alpha):
    def copy(t, k):
        return pltpu.make_async_copy(ys_ref.at[pl.ds(pos_ref[k, t], 1), :],
                                     buf.at[k, pl.ds(t, 1), :], sem)

    for t in range(tt):
        for k in range(TOP_K):
            copy(t, k).start(priority=k % 2)
    half = h2_ref.shape[1]
    lo, hi = _unpack_pair(h2_ref[...])
    lo, hi = lo.astype(BF16), hi.astype(BF16)
    h1 = _dot(lo, ws1_ref[:half, :]) + _dot(hi, ws1_ref[half:, :])
    h3 = _dot(lo, ws3_ref[:half, :]) + _dot(hi, ws3_ref[half:, :])
    shared = _dot((_silu(h1) * h3).astype(BF16), ws2_ref[...])
    for t in range(tt):
        for k in range(TOP_K):
            copy(t, k).wait()
    wt = wt_ref[...]
    f_lo, f_hi = shared[:, :half], shared[:, half:]
    for k in range(TOP_K):
        y_lo, y_hi = _unpack_pair(buf[k])
        f_lo = f_lo + wt[:, k:k + 1] * y_lo
        f_hi = f_hi + wt[:, k:k + 1] * y_hi
    ffn = jnp.concatenate([f_lo, f_hi], axis=1)
    o_ref[...] = _ln(alpha * x1_ref[...] + g2_ref[...] * ffn) * lg_ref[...] + lb_ref[...]


def _combine(pos_kt, ys, w_tk, h2, x1, g2, ln_g, ln_b, ws1, ws3, ws2, alpha):
    B, S, D = x1.shape
    tt = min(256, S)
    nt = S // tt
    row = lambda n: pl.BlockSpec((None, tt, n), lambda b, i: (b, i, 0))
    full = lambda a: pl.BlockSpec(a.shape, lambda b, i: (0,) * a.ndim)
    return pl.pallas_call(
        functools.partial(_combine_kernel, tt=tt, alpha=alpha),
        grid=(B, nt),
        in_specs=[pl.BlockSpec((TOP_K, tt), lambda b, i: (0, b * nt + i), memory_space=pltpu.SMEM),
                  pl.BlockSpec(memory_space=pl.ANY),
                  row(TOP_K), row(D // 2), row(D),
                  pl.BlockSpec((None, 1, D), lambda b, i: (b, 0, 0)),
                  full(ln_g), full(ln_b), full(ws1), full(ws3), full(ws2)],
        out_specs=row(D),
        out_shape=jax.ShapeDtypeStruct((B, S, D), F32),
        scratch_shapes=[pltpu.VMEM((TOP_K, tt, D // 2), I32), pltpu.SemaphoreType.DMA(())],
        compiler_params=_cp(("arbitrary", "arbitrary"), VMEM_LIMIT),
        name="combine",
    )(pos_kt, ys, w_tk, h2, x1, g2, ln_g, ln_b, ws1, ws3, ws2)


def _arrange_w_in(w):
    z = lambda n: jnp.zeros((w.shape[0], n), w.dtype)
    return jnp.concatenate([w[:, 0:1280], w[:, 1296:1552], w[:, 1552:1680], w[:, 1680:1712],
                            z(96), w[:, 1280:1296], z(112)], axis=1).astype(BF16)


def _head_layout(nope, rope_a, rope_b):
    r = nope.shape[0]
    z = jnp.zeros((r, MLA_HEADS, MLA_HEAD_PAD - MLA_NOPE - MLA_ROPE), nope.dtype)
    return jnp.concatenate([nope, rope_a, rope_b, z], axis=2).reshape(r, MLA_HEADS * MLA_HEAD_PAD)


def _arrange_mla(w_uq, w_uk):
    half = MLA_ROPE // 2
    q = w_uq.reshape(MLA_Q_LORA, MLA_HEADS, MLA_NOPE + MLA_ROPE)
    qn, q1, q2 = q[..., :MLA_NOPE], q[..., MLA_NOPE:MLA_NOPE + half], q[..., MLA_NOPE + half:]
    wq = _head_layout(qn, q1, q2)
    wq_sw = _head_layout(jnp.zeros_like(qn), q2, q1)
    kn = w_uk.reshape(MLA_KV_LORA, MLA_HEADS, MLA_NOPE)
    zk = jnp.zeros((MLA_KV_LORA, MLA_HEADS, half), w_uk.dtype)
    eye = jnp.eye(MLA_ROPE, dtype=w_uk.dtype)
    e1 = jnp.broadcast_to(eye[:, None, :half], (MLA_ROPE, MLA_HEADS, half))
    e2 = jnp.broadcast_to(eye[:, None, half:], (MLA_ROPE, MLA_HEADS, half))
    zn = jnp.zeros((MLA_ROPE, MLA_HEADS, MLA_NOPE), w_uk.dtype)
    pad = jnp.zeros((128 - MLA_ROPE, MLA_HEADS * MLA_HEAD_PAD), w_uk.dtype)
    wk = jnp.concatenate([_head_layout(kn, zk, zk), _head_layout(zn, e1, e2), pad], axis=0)
    wk_sw = jnp.concatenate([_head_layout(jnp.zeros_like(kn), zk, zk), _head_layout(zn, e2, e1), pad],
                            axis=0)
    return wq.astype(BF16), wq_sw.astype(BF16), wk.astype(BF16), wk_sw.astype(BF16)


def _rope_rows():
    half = MLA_ROPE // 2
    inv_freq = ROPE_THETA ** (-jnp.arange(0, MLA_ROPE, 2, dtype=F32) / MLA_ROPE)
    z = lambda n: jnp.zeros((n,), F32)
    frq = jnp.concatenate([z(MLA_NOPE), inv_freq, inv_freq, z(MLA_HEAD_PAD - MLA_NOPE - MLA_ROPE)])
    sgn = jnp.concatenate([z(MLA_NOPE), -jnp.ones((half,), F32), jnp.ones((half,), F32),
                           z(MLA_HEAD_PAD - MLA_NOPE - MLA_ROPE)])
    return frq[None, :], sgn[None, :]


def _block_diag_pool(w_pool):
    G, C, _ = w_pool.shape
    out = jnp.zeros((G * C, G * C), w_pool.dtype)
    for g in range(G):
        out = out.at[g * C:(g + 1) * C, g * C:(g + 1) * C].set(w_pool[g])
    return out.astype(BF16)


def kernel(x, c, positions, w_ada, b_ada, w_in, w_pool, s_pool, conv_w, conv_b, gate_b, gn_w, g_q,
           g_kv, w_uq, w_uk, w_uv, w_out, ln1_g, ln1_b, w_router, e_bias, w1, w3, w2, ws1, ws3, ws2,
           ln2_g, ln2_b):
    B, S, D = x.shape
    depth = w_in.shape[0]
    T = B * S
    H = ML_HEADS
    alpha = float((2 * depth) ** 0.25)
    n_assign = T * TOP_K
    n_blocks = n_assign // EXPERT_BLOCK + N_EXPERTS
    n_rows = n_blocks * EXPERT_BLOCK

    ada = _ada(c, w_ada, b_ada)
    frq, sgn = _rope_rows()
    pos3 = positions.reshape(B, S, 1)
    qk_scale = jnp.concatenate([jnp.ones((1, ML_WIDTH), F32),
                                jnp.full((1, ML_WIDTH), ML_DH ** -0.5, F32)], axis=1)
    ones_row = lambda n: jnp.ones((B, n, 1, S), BF16)
    zero_rows = lambda n, r: jnp.zeros((B, n, r, S), BF16)

    for l in range(depth):
        sh1, sc1, g1, sh2, sc2, g2 = [a.reshape(B, 1, D) for a in jnp.split(ada[l], 6, axis=-1)]
        u_pool, u_qk, u_v, u_o, u_dq, u_kv, u_gate = _inproj(x, sc1, sh1, _arrange_w_in(w_in[l]))
        y_pool = _pool(u_pool, _block_diag_pool(w_pool[l]), s_pool[l][None, :])

        qk_act = _conv(u_qk, conv_w[l], conv_b[l][None, :], qk_scale)
        heads_t = lambda a: a.reshape(B, S, H, ML_DH).transpose(0, 2, 3, 1)
        gates = u_gate[:, :, :4 * H].reshape(B, S, 4, H)
        vt_ml = jnp.concatenate([heads_t(u_v), ones_row(H), zero_rows(H, ML_VT_ROWS - ML_DH - 1)],
                                axis=2)
        y_ml = _mlstm(qk_act[:, :, ML_WIDTH:].reshape(B, S, H, ML_DH).transpose(0, 2, 1, 3),
                      heads_t(qk_act[:, :, :ML_WIDTH]), vt_ml, heads_t(u_o),
                      gates.transpose(0, 3, 1, 2),
                      gates.transpose(0, 3, 2, 1),
                      gate_b[l].reshape(4, H).T.reshape(H, 1, 4),
                      gate_b[l].reshape(4, H).T.reshape(H, 4, 1),
                      gn_w[l].reshape(H, ML_DH, 1))
        y_ml = y_ml.transpose(0, 3, 1, 2).reshape(B, S, ML_WIDTH)

        wq, wq_sw, wk, wk_sw = _arrange_mla(w_uq[l], w_uk[l])
        qh, kh, vh = _mla_proj(u_dq, u_kv, pos3, frq, sgn, g_q[l][None, :], g_kv[l][None, :],
                               wq, wq_sw, wk, wk_sw, w_uv[l].astype(BF16))
        y_mla = _attn(qh, kh, vh.transpose(0, 2, 1)).transpose(0, 2, 1)

        x1, h2, logits_t = _outproj(y_pool, y_ml, y_mla, x, g1, ln1_g[l][None, :], ln1_b[l][None, :],
                                    sc2, sh2, w_out[l].astype(BF16), w_router[l].T.astype(BF16), alpha)
        idx_t, w_t = _route(logits_t, e_bias[l][:, None])
        rank_t, counts = _rank(idx_t)
        counts = counts[:, 0].astype(I32)
        padded = (counts + EXPERT_BLOCK - 1) // EXPERT_BLOCK * EXPERT_BLOCK
        pend = jnp.cumsum(padded)
        pstart = pend - padded
        n_used = (pend[-1:] // EXPERT_BLOCK).astype(I32)
        blk_start = jnp.arange(n_blocks, dtype=I32) * EXPERT_BLOCK
        blk_expert = jnp.minimum(jnp.sum((pend[None, :] <= blk_start[:, None]).astype(I32), axis=1),
                                 N_EXPERTS - 1)
        pos_kt = _pos(idx_t, rank_t, pstart.astype(F32)[:, None]).transpose(1, 0, 2).reshape(TOP_K, T)
        xs = _dispatch(pos_kt, h2.reshape(T, D // 2), n_rows)
        ys = _experts(blk_expert, n_used, xs, w1, w3, w2, l)
        x = _combine(pos_kt, ys, w_t.transpose(0, 2, 1), h2, x1, g2, ln2_g[l][None, :],
                     ln2_b[l][None, :], ws1[l].astype(BF16), ws3[l].astype(BF16),
                     ws2[l].astype(BF16), alpha)
    return x
```

```python
import functools

import jax
import jax.numpy as jnp
import numpy as np
from jax import lax
from jax.experimental import pallas as pl
from jax.experimental.pallas import tpu as pltpu

F32 = jnp.float32
BF16 = jnp.bfloat16
I32 = jnp.int32

D_MODEL = 1024
POOL_WINDOWS = (2, 4, 8, 16)
POOL_GROUP_DIM = 64
POOL_WIDTH = 256
ML_HEADS = 4
ML_DH = 64
ML_WIDTH = 256
ML_CONV = 5
ML_VT_ROWS = 80
MLA_HEADS = 8
MLA_NOPE = 64
MLA_ROPE = 32
MLA_V = 64
MLA_Q_LORA = 256
MLA_KV_LORA = 128
MLA_HEAD_PAD = 128
ROPE_THETA = 10000.0
N_EXPERTS = 256
TOP_K = 8
N_GROUPS = 8
TOPK_GROUPS = 4
GROUP_SIZE = N_EXPERTS // N_GROUPS
D_EXPERT = 256
ROUTED_SCALE = 2.5
LN_EPS = 1e-5
RMS_EPS = 1e-6
NEG_BIG = -1e30
NEG_INF = float("-inf")

EXPERT_BLOCK = 256
ROW_PIECES = D_MODEL // 2 // 128
VMEM_LIMIT = 56 * 1024 * 1024


def _cp(sem, vmem=None):
    return pltpu.CompilerParams(dimension_semantics=sem, vmem_limit_bytes=vmem)


def _ln(x):
    mu = jnp.mean(x, axis=-1, keepdims=True)
    xc = x - mu
    var = jnp.mean(xc * xc, axis=-1, keepdims=True)
    return xc * lax.rsqrt(var + LN_EPS)


def _silu(x):
    return x * jax.nn.sigmoid(x)


def _rms(x, g):
    return x * lax.rsqrt(jnp.mean(x * x, axis=-1, keepdims=True) + RMS_EPS) * g


def _dot(a, b):
    return jnp.dot(a, b, preferred_element_type=F32)


def _dot_nt(a, b):
    return lax.dot_general(a, b, (((1,), (1,)), ((), ())), preferred_element_type=F32)


def _dot_tn(a, b):
    return lax.dot_general(a, b, (((0,), (0,)), ((), ())), preferred_element_type=F32)


def _pack_pair(lo, hi):
    lo_b = lax.bitcast_convert_type(lo.astype(BF16).astype(F32), I32)
    hi_b = lax.bitcast_convert_type(hi.astype(BF16).astype(F32), I32)
    return jnp.bitwise_or(hi_b, lax.shift_right_logical(lo_b, 16))


def _unpack_pair(w):
    lo = lax.bitcast_convert_type(lax.shift_left(w, 16), F32)
    hi = lax.bitcast_convert_type(jnp.bitwise_and(w, -65536), F32)
    return lo, hi


def _ada_kernel(c_ref, w_ref, b_ref, o_ref):
    ca = _silu(c_ref[...]).astype(BF16)
    o_ref[...] = _dot(ca, w_ref[...].astype(BF16)) + b_ref[...]


def _ada(c, w_ada, b_ada):
    L, D, N = w_ada.shape
    B = c.shape[0]
    tn = 1536
    return pl.pallas_call(
        _ada_kernel,
        grid=(L, N // tn),
        in_specs=[pl.BlockSpec((B, D), lambda l, j: (0, 0)),
                  pl.BlockSpec((None, D, tn), lambda l, j: (l, 0, j)),
                  pl.BlockSpec((None, 1, tn), lambda l, j: (l, 0, j))],
        out_specs=pl.BlockSpec((None, B, tn), lambda l, j: (l, 0, j)),
        out_shape=jax.ShapeDtypeStruct((L, B, N), F32),
        compiler_params=_cp(("parallel", "parallel"), VMEM_LIMIT),
        name="ada",
    )(c, w_ada, b_ada.reshape(L, 1, N))


def _inproj_kernel(x_ref, sc_ref, sh_ref, w_ref, pool_ref, qk_ref, v_ref, o_ref, dq_ref,
                   kv_ref, gate_ref):
    h = (_ln(x_ref[...]) * (1.0 + sc_ref[...]) + sh_ref[...]).astype(BF16)
    pool_ref[...] = _dot(h, w_ref[:, 0:256])
    qk_ref[...] = _dot(h, w_ref[:, 256:768])
    v_ref[...] = _dot(h, w_ref[:, 768:1024]).astype(BF16)
    o_ref[...] = _dot(h, w_ref[:, 1024:1280])
    dq_ref[...] = _dot(h, w_ref[:, 1280:1536])
    kv_ref[...] = _dot(h, w_ref[:, 1536:1792])
    gate_ref[...] = _dot(h, w_ref[:, 1792:1920])


def _inproj(x, sc, sh, w):
    B, S, D = x.shape
    tm = min(512, S)
    widths = (256, 512, 256, 256, 256, 256, 128)
    dtypes = (F32, F32, BF16, F32, F32, F32, F32)
    row = lambda n: pl.BlockSpec((None, tm, n), lambda b, i: (b, i, 0))
    return pl.pallas_call(
        _inproj_kernel,
        grid=(B, S // tm),
        in_specs=[row(D),
                  pl.BlockSpec((None, 1, D), lambda b, i: (b, 0, 0)),
                  pl.BlockSpec((None, 1, D), lambda b, i: (b, 0, 0)),
                  pl.BlockSpec(w.shape, lambda b, i: (0, 0))],
        out_specs=[row(n) for n in widths],
        out_shape=[jax.ShapeDtypeStruct((B, S, n), dt) for n, dt in zip(widths, dtypes)],
        compiler_params=_cp(("parallel", "parallel"), VMEM_LIMIT),
        name="inproj",
    )(x, sc, sh, w)


def _shift_dn(a, k, row):
    return jnp.where(row >= k, pltpu.roll(a, k, 0), 0.0)


def _shift_up(a, k, row):
    n = a.shape[0]
    return jnp.where(row < n - k, pltpu.roll(a, n - k, 0), 0.0)


def _pool_kernel(u_ref, w_ref, s_ref, y_ref):
    x = u_ref[...]
    S, C = x.shape
    row = lax.broadcasted_iota(I32, (S, C), 0)
    lane = lax.broadcasted_iota(I32, (S, C), 1)
    rowf = row.astype(F32)
    p1, f1 = x, x
    p2 = p1 + _shift_dn(p1, 1, row)
    f2 = f1 + _shift_up(f1, 1, row)
    p4 = p2 + _shift_dn(p2, 2, row)
    f4 = f2 + _shift_up(f2, 2, row)
    p8 = p4 + _shift_dn(p4, 4, row)
    f8 = f4 + _shift_up(f4, 4, row)
    wins = []
    for half, p, f in ((1, p1, f1), (2, p2, f2), (4, p4, f4), (8, p8, f8)):
        total = _shift_dn(p, 1, row) + f
        cnt = jnp.minimum(rowf + half, float(S)) - jnp.maximum(rowf - half, 0.0)
        wins.append(total / cnt)
    pooled = jnp.where(lane < 64, wins[0],
                       jnp.where(lane < 128, wins[1], jnp.where(lane < 192, wins[2], wins[3])))
    d = (pooled - x).astype(BF16)
    y_ref[...] = (_dot(d, w_ref[...]) * s_ref[...]).astype(BF16)


def _pool(u_pool, w_bd, s_pool):
    B, S, C = u_pool.shape
    return pl.pallas_call(
        _pool_kernel,
        grid=(B,),
        in_specs=[pl.BlockSpec((None, S, C), lambda b: (b, 0, 0)),
                  pl.BlockSpec((C, C), lambda b: (0, 0)),
                  pl.BlockSpec((1, C), lambda b: (0, 0))],
        out_specs=pl.BlockSpec((None, S, C), lambda b: (b, 0, 0)),
        out_shape=jax.ShapeDtypeStruct((B, S, C), BF16),
        compiler_params=_cp(("parallel",), VMEM_LIMIT),
        name="pool",
    )(u_pool, w_bd, s_pool)


def _conv_kernel(u_ref, w_ref, b_ref, scale_ref, y_ref):
    x = u_ref[...]
    S, C = x.shape
    row = lax.broadcasted_iota(I32, (S, C), 0)
    y = (_shift_dn(x, 2, row) * w_ref[0:1, :] + _shift_dn(x, 1, row) * w_ref[1:2, :]
         + x * w_ref[2:3, :] + _shift_up(x, 1, row) * w_ref[3:4, :]
         + _shift_up(x, 2, row) * w_ref[4:5, :] + b_ref[...])
    y_ref[...] = (_silu(y) * scale_ref[...]).astype(BF16)


def _conv(u_qk, conv_w, conv_b, scale):
    B, S, C = u_qk.shape
    return pl.pallas_call(
        _conv_kernel,
        grid=(B,),
        in_specs=[pl.BlockSpec((None, S, C), lambda b: (b, 0, 0)),
                  pl.BlockSpec((ML_CONV, C), lambda b: (0, 0)),
                  pl.BlockSpec((1, C), lambda b: (0, 0)),
                  pl.BlockSpec((1, C), lambda b: (0, 0))],
        out_specs=pl.BlockSpec((None, S, C), lambda b: (b, 0, 0)),
        out_shape=jax.ShapeDtypeStruct((B, S, C), BF16),
        compiler_params=_cp(("parallel",), VMEM_LIMIT),
        name="mlstm_conv",
    )(u_qk, conv_w, conv_b, scale)


def _log_sigmoid(x):
    return jnp.minimum(x, 0.0) - jnp.log(1.0 + jnp.exp(-jnp.abs(x)))


def _split3(x):
    a = x.astype(BF16)
    r = x - a.astype(F32)
    b = r.astype(BF16)
    c = (r - b.astype(F32)).astype(BF16)
    return a, b, c


def _mlstm_chunk(s, k, qt, vt, c_col, b_row, i_row, b_last, allowed, ct, m_prev):
    logw = jnp.where(allowed, c_col + b_row, NEG_INF)
    m_inter = b_row + m_prev
    m_t = jnp.maximum(m_inter, jnp.max(logw, axis=0, keepdims=True))
    w_intra = jnp.exp(logw - m_t)
    w_inter = jnp.exp(m_inter - m_t)
    qk = (s * w_intra).astype(BF16)
    nd = _dot(vt, qk) + w_inter * _dot(ct.astype(BF16), qt)
    den = jnp.maximum(jnp.abs(nd[ML_DH:ML_DH + 1, :]), jnp.exp(-m_t))
    h = nd[:ML_DH, :] / den
    g = b_last - b_row + i_row
    m_new = jnp.maximum(b_last + m_prev, jnp.max(g, axis=1, keepdims=True))
    w_k = jnp.exp(g - m_new)
    decay = jnp.exp(b_last + m_prev - m_new)
    ct_new = decay * ct + _dot((vt.astype(F32) * w_k).astype(BF16), k)
    return h, ct_new, m_new


def _mlstm_kernel(k_ref, qt_ref, vt_ref, uo_ref, gc_ref, gr_ref, bc_ref, br_ref, gn_ref, y_ref,
                  hf_ref, hb_ref, *, chunk):
    S = k_ref.shape[0]
    L = chunk
    nc = S // L
    ri = lax.broadcasted_iota(I32, (L, L), 0)
    ci = lax.broadcasted_iota(I32, (L, L), 1)
    lower = ci <= ri
    upper = ci >= ri
    tril = jnp.where(lower, 1.0, 0.0).astype(BF16)
    triu = jnp.where(upper, 1.0, 0.0).astype(BF16)
    bias_c = bc_ref[...]
    bias_r = br_ref[...]

    def prep(r0, forward):
        gc = gc_ref[pl.ds(r0, L), :] + bias_c
        gr = gr_ref[:, pl.ds(r0, L)] + bias_r
        c1, c2, c3 = _split3(_log_sigmoid(gc))
        r1, r2, r3 = _split3(_log_sigmoid(gr))
        k = k_ref[pl.ds(r0, L), :]
        qt = qt_ref[:, pl.ds(r0, L)]
        s = _dot(k, qt)
        if forward:
            cum_c = (_dot(tril, c1) + _dot(tril, c2) + _dot(tril, c3))[:, 1:2]
            cum_r = (_dot(r1, triu) + _dot(r2, triu) + _dot(r3, triu))[1:2, :]
            return s, k, qt, gc[:, 0:1] - cum_c, cum_r, gr[0:1, :], cum_r[:, L - 1:L]
        cum_c = (_dot(triu, c1) + _dot(triu, c2) + _dot(triu, c3))[:, 3:4]
        cum_r = (_dot(r1, tril) + _dot(r2, tril) + _dot(r3, tril))[3:4, :]
        return s, k, qt, gc[:, 2:3] - cum_c, cum_r, gr[2:3, :], cum_r[:, 0:1]

    def step(i, carry):
        cf, mf, cb, mb = carry
        rf = pl.multiple_of(i * L, L)
        rb = pl.multiple_of((nc - 1 - i) * L, L)
        pf = prep(rf, True)
        pb = prep(rb, False)
        h, cf, mf = _mlstm_chunk(*pf[:3], vt_ref[:, pl.ds(rf, L)], *pf[3:], upper, cf, mf)
        hf_ref[:, pl.ds(rf, L)] = h
        h, cb, mb = _mlstm_chunk(*pb[:3], vt_ref[:, pl.ds(rb, L)], *pb[3:], lower, cb, mb)
        hb_ref[:, pl.ds(rb, L)] = h
        return cf, mf, cb, mb

    c0 = jnp.zeros((ML_VT_ROWS, ML_DH), F32)
    m0 = jnp.full((1, 1), NEG_BIG, F32)
    lax.fori_loop(0, nc, step, (c0, m0, c0, m0))
    h = hf_ref[...] + hb_ref[...]
    mu = jnp.mean(h, axis=0, keepdims=True)
    hc = h - mu
    var = jnp.mean(hc * hc, axis=0, keepdims=True)
    hn = hc * lax.rsqrt(var + LN_EPS)
    y_ref[...] = (jax.nn.sigmoid(uo_ref[...]) * (hn * gn_ref[...])).astype(BF16)


def _mlstm(k_heads, qt_heads, vt_ext, uot_heads, g_col, g_row, b_col, b_row, gn_w):
    B, H, S, dh = k_heads.shape
    L = min(256, S)
    tr = lambda n: pl.BlockSpec((None, None, n, S), lambda b, h: (b, h, 0, 0))
    return pl.pallas_call(
        functools.partial(_mlstm_kernel, chunk=L),
        grid=(B, H),
        in_specs=[pl.BlockSpec((None, None, S, dh), lambda b, h: (b, h, 0, 0)),
                  tr(dh), tr(ML_VT_ROWS), tr(dh),
                  pl.BlockSpec((None, None, S, 4), lambda b, h: (b, h, 0, 0)),
                  tr(4),
                  pl.BlockSpec((None, 1, 4), lambda b, h: (h, 0, 0)),
                  pl.BlockSpec((None, 4, 1), lambda b, h: (h, 0, 0)),
                  pl.BlockSpec((None, dh, 1), lambda b, h: (h, 0, 0))],
        out_specs=tr(dh),
        out_shape=jax.ShapeDtypeStruct((B, H, dh, S), BF16),
        scratch_shapes=[pltpu.VMEM((dh, S), F32), pltpu.VMEM((dh, S), F32)],
        compiler_params=_cp(("parallel", "parallel"), VMEM_LIMIT),
        name="mlstm_scan",
    )(k_heads, qt_heads, vt_ext, uot_heads, g_col, g_row, b_col, b_row, gn_w)


def _mla_proj_kernel(dq_ref, kv_ref, pos_ref, frq_ref, sgn_ref, gq_ref, gkv_ref, wq_ref, wqs_ref,
                     wk_ref, wks_ref, wv_ref, q_ref, k_ref, v_ref, *, scale):
    ang = pos_ref[...].astype(F32) * frq_ref[...]
    cos = jnp.cos(ang)
    sin = jnp.sin(ang) * sgn_ref[...]
    qn = _rms(dq_ref[...], gq_ref[...]).astype(BF16)
    a = _dot(qn, wq_ref[...])
    a_sw = _dot(qn, wqs_ref[...])
    kv = kv_ref[...]
    ckv = _rms(kv[:, :MLA_KV_LORA], gkv_ref[...])
    kin = jnp.concatenate([ckv, kv[:, MLA_KV_LORA:]], axis=1).astype(BF16)
    ak = _dot(kin, wk_ref[...])
    ak_sw = _dot(kin, wks_ref[...])
    for h in range(MLA_HEADS):
        sl = slice(h * MLA_HEAD_PAD, (h + 1) * MLA_HEAD_PAD)
        q_ref[:, sl] = ((a[:, sl] * cos + a_sw[:, sl] * sin) * scale).astype(BF16)
        k_ref[:, sl] = (ak[:, sl] * cos + ak_sw[:, sl] * sin).astype(BF16)
    v_ref[...] = _dot(ckv.astype(BF16), wv_ref[...]).astype(BF16)


def _mla_proj(u_dq, u_kv, pos, frq, sgn, g_q, g_kv, wq, wqs, wk, wks, wv):
    B, S, _ = u_dq.shape
    tm = min(512, S)
    hw = MLA_HEADS * MLA_HEAD_PAD
    vw = MLA_HEADS * MLA_V
    row = lambda n: pl.BlockSpec((None, tm, n), lambda b, i: (b, i, 0))
    full = lambda a: pl.BlockSpec(a.shape, lambda b, i: (0,) * a.ndim)
    scale = (MLA_NOPE + MLA_ROPE) ** -0.5 * float(np.log2(np.e))
    return pl.pallas_call(
        functools.partial(_mla_proj_kernel, scale=scale),
        grid=(B, S // tm),
        in_specs=[row(MLA_Q_LORA), row(256), row(1), full(frq), full(sgn), full(g_q), full(g_kv),
                  full(wq), full(wqs), full(wk), full(wks), full(wv)],
        out_specs=[row(hw), row(hw), row(vw)],
        out_shape=[jax.ShapeDtypeStruct((B, S, hw), BF16),
                   jax.ShapeDtypeStruct((B, S, hw), BF16),
                   jax.ShapeDtypeStruct((B, S, vw), BF16)],
        compiler_params=_cp(("parallel", "parallel"), VMEM_LIMIT),
        name="mla_proj",
    )(u_dq, u_kv, pos, frq, sgn, g_q, g_kv, wq, wqs, wk, wks, wv)


def _attn_kernel(q_ref, k_ref, vt_ref, o_ref, *, tk):
    tq = q_ref.shape[0]
    S = k_ref.shape[0]
    nk = S // tk
    def scores(h, j):
        k = k_ref[j * tk:(j + 1) * tk, h * MLA_HEAD_PAD:(h + 1) * MLA_HEAD_PAD]
        return _dot_nt(k, q_ref[:, h * MLA_HEAD_PAD:(h + 1) * MLA_HEAD_PAD])

    m = [jnp.full((1, tq), NEG_INF, F32)] * 2
    l = [jnp.zeros((1, tq), F32)] * 2
    acc = [jnp.zeros((MLA_V, tq), F32)] * 2
    s = [scores(0, 0), scores(1, 0)]
    for j in range(nk):
        for h in range(2):
            s_cur = s[h]
            if j + 1 < nk:
                s[h] = scores(h, j + 1)
            m_new = jnp.maximum(m[h], jnp.max(s_cur, axis=0, keepdims=True))
            alpha = jnp.exp2(m[h] - m_new)
            p = jnp.exp2(s_cur - m_new)
            l[h] = alpha * l[h] + jnp.sum(p, axis=0, keepdims=True)
            vt = vt_ref[h * MLA_V:(h + 1) * MLA_V, j * tk:(j + 1) * tk]
            acc[h] = alpha * acc[h] + _dot(vt, p.astype(BF16))
            m[h] = m_new
    for h in range(2):
        o_ref[h * MLA_V:(h + 1) * MLA_V, :] = (acc[h] / l[h]).astype(BF16)


def _attn(qh, kh, vt):
    B, S, _ = qh.shape
    tq = min(512, S)
    tk = min(256, S)
    pw = 2 * MLA_HEAD_PAD
    return pl.pallas_call(
        functools.partial(_attn_kernel, tk=tk),
        grid=(B, MLA_HEADS // 2, S // tq),
        in_specs=[pl.BlockSpec((None, tq, pw), lambda b, p, i: (b, i, p)),
                  pl.BlockSpec((None, S, pw), lambda b, p, i: (b, 0, p)),
                  pl.BlockSpec((None, 2 * MLA_V, S), lambda b, p, i: (b, p, 0))],
        out_specs=pl.BlockSpec((None, 2 * MLA_V, tq), lambda b, p, i: (b, p, i)),
        out_shape=jax.ShapeDtypeStruct((B, MLA_HEADS * MLA_V, S), BF16),
        compiler_params=_cp(("parallel", "parallel", "parallel"), VMEM_LIMIT),
        name="mla_attn",
    )(qh, kh, vt)


def _outproj_kernel(yp_ref, ym_ref, ya_ref, x_ref, g1_ref, lg_ref, lb_ref, sc_ref, sh_ref, wo_ref,
                    wr_ref, x1_ref, h2_ref, lt_ref, *, alpha):
    mix = (_dot(yp_ref[...], wo_ref[0:256, :]) + _dot(ym_ref[...], wo_ref[256:512, :])
           + _dot(ya_ref[...], wo_ref[512:1024, :]))
    x1 = _ln(alpha * x_ref[...] + g1_ref[...] * mix) * lg_ref[...] + lb_ref[...]
    x1_ref[...] = x1
    h2 = _ln(x1) * (1.0 + sc_ref[...]) + sh_ref[...]
    half = h2.shape[1] // 2
    h2_ref[...] = _pack_pair(h2[:, :half], h2[:, half:])
    lt_ref[...] = _dot_nt(wr_ref[...], h2.astype(BF16))


def _outproj(y_pool, y_ml, y_mla, x, g1, ln_g, ln_b, sc2, sh2, w_out, w_router_t, alpha):
    B, S, D = x.shape
    tm = min(512, S)
    row = lambda n: pl.BlockSpec((None, tm, n), lambda b, i: (b, i, 0))
    per_b = pl.BlockSpec((None, 1, D), lambda b, i: (b, 0, 0))
    vec = pl.BlockSpec((1, D), lambda b, i: (0, 0))
    return pl.pallas_call(
        functools.partial(_outproj_kernel, alpha=alpha),
        grid=(B, S // tm),
        in_specs=[row(256), row(256), row(512), row(D), per_b, vec, vec, per_b, per_b,
                  pl.BlockSpec((D, D), lambda b, i: (0, 0)),
                  pl.BlockSpec((N_EXPERTS, D), lambda b, i: (0, 0))],
        out_specs=[row(D), row(D // 2), pl.BlockSpec((None, N_EXPERTS, tm), lambda b, i: (b, 0, i))],
        out_shape=[jax.ShapeDtypeStruct((B, S, D), F32), jax.ShapeDtypeStruct((B, S, D // 2), I32),
                   jax.ShapeDtypeStruct((B, N_EXPERTS, S), F32)],
        compiler_params=_cp(("parallel", "parallel"), VMEM_LIMIT),
        name="outproj",
    )(y_pool, y_ml, y_mla, x, g1, ln_g, ln_b, sc2, sh2, w_out, w_router_t)


def _first_max(v, iota, n):
    m = jnp.max(v, axis=0, keepdims=True)
    first = jnp.min(jnp.where(v == m, iota, n), axis=0, keepdims=True)
    return m, first


def _route_kernel(lt_ref, eb_ref, idx_ref, w_ref):
    scores = jax.nn.sigmoid(lt_ref[...])
    sel = scores + eb_ref[...]
    E, tt = sel.shape
    gi = lax.broadcasted_iota(I32, (GROUP_SIZE, tt), 0)
    groups = [sel[g * GROUP_SIZE:(g + 1) * GROUP_SIZE, :] for g in range(N_GROUPS)]
    gs = []
    for grp in groups:
        m1, f1 = _first_max(grp, gi, GROUP_SIZE)
        m2 = jnp.max(jnp.where(gi == f1, NEG_INF, grp), axis=0, keepdims=True)
        gs.append(m1 + m2)
    chosen = [jnp.zeros((1, tt), F32) for _ in range(N_GROUPS)]
    for _ in range(TOPK_GROUPS):
        m = functools.reduce(jnp.maximum, gs)
        first = functools.reduce(
            jnp.minimum, [jnp.where(gs[g] == m, g, N_GROUPS) for g in range(N_GROUPS)])
        for g in range(N_GROUPS):
            hit = first == g
            chosen[g] = jnp.where(hit, 1.0, chosen[g])
            gs[g] = jnp.where(hit, NEG_INF, gs[g])
    cand = jnp.concatenate(
        [jnp.where(chosen[g] > 0.5, groups[g], NEG_INF) for g in range(N_GROUPS)], axis=0)
    ei = lax.broadcasted_iota(I32, (E, tt), 0)
    ws = []
    for k in range(TOP_K):
        _, f = _first_max(cand, ei, E)
        hit = ei == f
        idx_ref[k:k + 1, :] = f
        ws.append(jnp.sum(jnp.where(hit, scores, 0.0), axis=0, keepdims=True))
        cand = jnp.where(hit, NEG_INF, cand)
    norm = ROUTED_SCALE / functools.reduce(jnp.add, ws)
    for k in range(TOP_K):
        w_ref[k:k + 1, :] = ws[k] * norm


def _route(logits_t, e_bias):
    B, E, S = logits_t.shape
    tt = min(512, S)
    return pl.pallas_call(
        _route_kernel,
        grid=(B, S // tt),
        in_specs=[pl.BlockSpec((None, E, tt), lambda b, i: (b, 0, i)),
                  pl.BlockSpec((E, 1), lambda b, i: (0, 0))],
        out_specs=[pl.BlockSpec((None, TOP_K, tt), lambda b, i: (b, 0, i)),
                   pl.BlockSpec((None, TOP_K, tt), lambda b, i: (b, 0, i))],
        out_shape=[jax.ShapeDtypeStruct((B, TOP_K, S), I32),
                   jax.ShapeDtypeStruct((B, TOP_K, S), F32)],
        compiler_params=_cp(("parallel", "parallel"), VMEM_LIMIT),
        name="route",
    )(logits_t, e_bias)


def _rank_kernel(idx_ref, rank_ref, cnt_ref, carry_ref):
    first = jnp.logical_and(pl.program_id(0) == 0, pl.program_id(1) == 0)

    @pl.when(first)
    def _():
        carry_ref[...] = jnp.zeros_like(carry_ref)

    K, tt = idx_ref.shape
    E = carry_ref.shape[0]
    ei = lax.broadcasted_iota(I32, (E, tt), 0)
    ri = lax.broadcasted_iota(I32, (tt, tt), 0)
    ci = lax.broadcasted_iota(I32, (tt, tt), 1)
    before = jnp.where(ri < ci, 1.0, 0.0).astype(BF16)
    base = carry_ref[...]
    for k in range(K):
        hit = ei == idx_ref[k:k + 1, :]
        onehot = jnp.where(hit, 1.0, 0.0)
        prefix = _dot(onehot.astype(BF16), before)
        rank = jnp.sum(jnp.where(hit, base + prefix, 0.0), axis=0, keepdims=True)
        rank_ref[k:k + 1, :] = rank.astype(I32)
        base = base + jnp.sum(onehot, axis=1, keepdims=True)
    carry_ref[...] = base
    cnt_ref[...] = base


def _rank(idx_t):
    B, K, S = idx_t.shape
    tt = min(256, S)
    return pl.pallas_call(
        _rank_kernel,
        grid=(B, S // tt),
        in_specs=[pl.BlockSpec((None, K, tt), lambda b, i: (b, 0, i))],
        out_specs=[pl.BlockSpec((None, K, tt), lambda b, i: (b, 0, i)),
                   pl.BlockSpec((N_EXPERTS, 1), lambda b, i: (0, 0))],
        out_shape=[jax.ShapeDtypeStruct((B, K, S), I32),
                   jax.ShapeDtypeStruct((N_EXPERTS, 1), F32)],
        scratch_shapes=[pltpu.VMEM((N_EXPERTS, 1), F32)],
        compiler_params=_cp(("arbitrary", "arbitrary"), VMEM_LIMIT),
        name="rank",
    )(idx_t)


def _pos_kernel(idx_ref, rank_ref, start_ref, pos_ref):
    K, tt = idx_ref.shape
    ei = lax.broadcasted_iota(I32, (N_EXPERTS, tt), 0)
    start = start_ref[...]
    for k in range(K):
        hit = ei == idx_ref[k:k + 1, :]
        off = jnp.sum(jnp.where(hit, start, 0.0), axis=0, keepdims=True)
        pos_ref[k:k + 1, :] = rank_ref[k:k + 1, :] + off.astype(I32)


def _pos(idx_t, rank_t, pstart):
    B, K, S = idx_t.shape
    tt = min(512, S)
    spec = pl.BlockSpec((None, K, tt), lambda b, i: (b, 0, i))
    return pl.pallas_call(
        _pos_kernel,
        grid=(B, S // tt),
        in_specs=[spec, spec, pl.BlockSpec((N_EXPERTS, 1), lambda b, i: (0, 0))],
        out_specs=spec,
        out_shape=jax.ShapeDtypeStruct((B, K, S), I32),
        compiler_params=_cp(("parallel", "parallel"), VMEM_LIMIT),
        name="slot",
    )(idx_t, rank_t, pstart)


def _dispatch_kernel(pos_ref, h_ref, xs_ref, sem, *, tt):
    def copy(t, k):
        return pltpu.make_async_copy(h_ref.at[pl.ds(t, 1), :],
                                     xs_ref.at[pos_ref[k, t]], sem)

    for t in range(tt):
        for k in range(TOP_K):
            copy(t, k).start(priority=k % 2)
    for t in range(tt):
        for k in range(TOP_K):
            copy(t, k).wait()


def _dispatch(pos_kt, h2_flat, n_rows):
    T, W = h2_flat.shape
    tt = 256
    return pl.pallas_call(
        functools.partial(_dispatch_kernel, tt=tt),
        grid=(T // tt,),
        in_specs=[pl.BlockSpec((TOP_K, tt), lambda i: (0, i), memory_space=pltpu.SMEM),
                  pl.BlockSpec((tt, W), lambda i: (i, 0))],
        out_specs=pl.BlockSpec(memory_space=pl.ANY),
        out_shape=jax.ShapeDtypeStruct((n_rows, 1, W), I32),
        scratch_shapes=[pltpu.SemaphoreType.DMA(())],
        compiler_params=pltpu.CompilerParams(dimension_semantics=("arbitrary",),
                                             has_side_effects=True),
        name="dispatch",
    )(pos_kt, h2_flat)


def _expert_kernel(be_ref, nu_ref, xs_ref, w1_ref, w3_ref, w2_ref, ys_ref, w1b, w3b, w2b):
    b = pl.program_id(0)
    prev = be_ref[jnp.maximum(b - 1, 0)]
    fresh = jnp.logical_or(b == 0, be_ref[b] != prev)

    @pl.when(jnp.logical_and(fresh, b < nu_ref[0]))
    def _():
        w1b[...] = w1_ref[...].astype(BF16)
        w3b[...] = w3_ref[...].astype(BF16)
        w2b[...] = w2_ref[...].astype(BF16)

    @pl.when(b < nu_ref[0])
    def _():
        pieces = [xs_ref[pl.ds(c, EXPERT_BLOCK, stride=ROW_PIECES), :] for c in range(ROW_PIECES)]
        half = ROW_PIECES * 128
        lo, hi = _unpack_pair(jnp.concatenate(pieces, axis=1))
        lo, hi = lo.astype(BF16), hi.astype(BF16)
        h1 = _dot(lo, w1b[:half, :]) + _dot(hi, w1b[half:, :])
        h3 = _dot(lo, w3b[:half, :]) + _dot(hi, w3b[half:, :])
        y = _dot((_silu(h1) * h3).astype(BF16), w2b[...])
        words = _pack_pair(y[:, :half], y[:, half:])
        for c in range(ROW_PIECES):
            ys_ref[pl.ds(c, EXPERT_BLOCK, stride=ROW_PIECES), :] = words[:, c * 128:(c + 1) * 128]


def _experts(blk_expert, n_used, xs, w1, w3, w2, layer):
    P, _, W = xs.shape
    D = 2 * W
    nb = P // EXPERT_BLOCK
    xs = xs.reshape(P * ROW_PIECES, 128)
    last = lambda b, be, nu: jnp.minimum(b, nu[0] - 1)
    wspec = lambda r, c: pl.BlockSpec((None, None, r, c),
                                      lambda b, be, nu: (layer, be[last(b, be, nu)], 0, 0))
    grid_spec = pltpu.PrefetchScalarGridSpec(
        num_scalar_prefetch=2,
        grid=(nb,),
        in_specs=[pl.BlockSpec((EXPERT_BLOCK * ROW_PIECES, 128), lambda b, be, nu: (last(b, be, nu), 0)),
                  wspec(D, D_EXPERT), wspec(D, D_EXPERT), wspec(D_EXPERT, D)],
        out_specs=pl.BlockSpec((EXPERT_BLOCK * ROW_PIECES, 128), lambda b, be, nu: (last(b, be, nu), 0)),
        scratch_shapes=[pltpu.VMEM((D, D_EXPERT), BF16), pltpu.VMEM((D, D_EXPERT), BF16),
                        pltpu.VMEM((D_EXPERT, D), BF16)],
    )
    ys = pl.pallas_call(
        _expert_kernel,
        grid_spec=grid_spec,
        out_shape=jax.ShapeDtypeStruct((P * ROW_PIECES, 128), I32),
        compiler_params=_cp(("arbitrary",), VMEM_LIMIT),
        name="experts",
    )(blk_expert, n_used, xs, w1, w3, w2)
    return ys.reshape(P, 1, W)


def _combine_kernel(pos_ref, ys_ref, wt_ref, h2_ref, x1_ref, g2_ref, lg_ref, lb_ref, ws1_ref,
                    ws3_ref, ws2_ref, o_ref, buf, sem, *, tt, alpha):
    def copy(t, k):
        return pltpu.make_async_copy(ys_ref.at[pos_ref[k, t]], buf.at[k, pl.ds(t, 1), :], sem)

    for t in range(tt):
        for k in range(TOP_K):
            copy(t, k).start(priority=k % 2)
    half = h2_ref.shape[1]
    lo, hi = _unpack_pair(h2_ref[...])
    lo, hi = lo.astype(BF16), hi.astype(BF16)
    h1 = _dot(lo, ws1_ref[:half, :]) + _dot(hi, ws1_ref[half:, :])
    h3 = _dot(lo, ws3_ref[:half, :]) + _dot(hi, ws3_ref[half:, :])
    shared = _dot((_silu(h1) * h3).astype(BF16), ws2_ref[...])
    for t in range(tt):
        for k in range(TOP_K):
            copy(t, k).wait()
    wt = wt_ref[...]
    f_lo, f_hi = shared[:, :half], shared[:, half:]
    for k in range(TOP_K):
        y_lo, y_hi = _unpack_pair(buf[k])
        f_lo = f_lo + wt[:, k:k + 1] * y_lo
        f_hi = f_hi + wt[:, k:k + 1] * y_hi
    ffn = jnp.concatenate([f_lo, f_hi], axis=1)
    o_ref[...] = _ln(alpha * x1_ref[...] + g2_ref[...] * ffn) * lg_ref[...] + lb_ref[...]


def _combine(pos_kt, ys, w_tk, h2, x1, g2, ln_g, ln_b, ws1, ws3, ws2, alpha):
    B, S, D = x1.shape
    tt = min(256, S)
    nt = S // tt
    row = lambda n: pl.BlockSpec((None, tt, n), lambda b, i: (b, i, 0))
    full = lambda a: pl.BlockSpec(a.shape, lambda b, i: (0,) * a.ndim)
    return pl.pallas_call(
        functools.partial(_combine_kernel, tt=tt, alpha=alpha),
        grid=(B, nt),
        in_specs=[pl.BlockSpec((TOP_K, tt), lambda b, i: (0, b * nt + i), memory_space=pltpu.SMEM),
                  pl.BlockSpec(memory_space=pl.ANY),
                  row(TOP_K), row(D // 2), row(D),
                  pl.BlockSpec((None, 1, D), lambda b, i: (b, 0, 0)),
                  full(ln_g), full(ln_b), full(ws1), full(ws3), full(ws2)],
        out_specs=row(D),
        out_shape=jax.ShapeDtypeStruct((B, S, D), F32),
        scratch_shapes=[pltpu.VMEM((TOP_K, tt, D // 2), I32), pltpu.SemaphoreType.DMA(())],
        compiler_params=_cp(("arbitrary", "arbitrary"), VMEM_LIMIT),
        name="combine",
    )(pos_kt, ys, w_tk, h2, x1, g2, ln_g, ln_b, ws1, ws3, ws2)


def _arrange_w_in(w):
    z = lambda n: jnp.zeros((w.shape[0], n), w.dtype)
    return jnp.concatenate([w[:, 0:1280], w[:, 1296:1552], w[:, 1552:1680], w[:, 1680:1712],
                            z(96), w[:, 1280:1296], z(112)], axis=1).astype(BF16)


def _head_layout(nope, rope_a, rope_b):
    r = nope.shape[0]
    z = jnp.zeros((r, MLA_HEADS, MLA_HEAD_PAD - MLA_NOPE - MLA_ROPE), nope.dtype)
    return jnp.concatenate([nope, rope_a, rope_b, z], axis=2).reshape(r, MLA_HEADS * MLA_HEAD_PAD)


def _arrange_mla(w_uq, w_uk):
    half = MLA_ROPE // 2
    q = w_uq.reshape(MLA_Q_LORA, MLA_HEADS, MLA_NOPE + MLA_ROPE)
    qn, q1, q2 = q[..., :MLA_NOPE], q[..., MLA_NOPE:MLA_NOPE + half], q[..., MLA_NOPE + half:]
    wq = _head_layout(qn, q1, q2)
    wq_sw = _head_layout(jnp.zeros_like(qn), q2, q1)
    kn = w_uk.reshape(MLA_KV_LORA, MLA_HEADS, MLA_NOPE)
    zk = jnp.zeros((MLA_KV_LORA, MLA_HEADS, half), w_uk.dtype)
    eye = jnp.eye(MLA_ROPE, dtype=w_uk.dtype)
    e1 = jnp.broadcast_to(eye[:, None, :half], (MLA_ROPE, MLA_HEADS, half))
    e2 = jnp.broadcast_to(eye[:, None, half:], (MLA_ROPE, MLA_HEADS, half))
    zn = jnp.zeros((MLA_ROPE, MLA_HEADS, MLA_NOPE), w_uk.dtype)
    pad = jnp.zeros((128 - MLA_ROPE, MLA_HEADS * MLA_HEAD_PAD), w_uk.dtype)
    wk = jnp.concatenate([_head_layout(kn, zk, zk), _head_layout(zn, e1, e2), pad], axis=0)
    wk_sw = jnp.concatenate([_head_layout(jnp.zeros_like(kn), zk, zk), _head_layout(zn, e2, e1), pad],
                            axis=0)
    return wq.astype(BF16), wq_sw.astype(BF16), wk.astype(BF16), wk_sw.astype(BF16)


def _rope_rows():
    half = MLA_ROPE // 2
    inv_freq = ROPE_THETA ** (-jnp.arange(0, MLA_ROPE, 2, dtype=F32) / MLA_ROPE)
    z = lambda n: jnp.zeros((n,), F32)
    frq = jnp.concatenate([z(MLA_NOPE), inv_freq, inv_freq, z(MLA_HEAD_PAD - MLA_NOPE - MLA_ROPE)])
    sgn = jnp.concatenate([z(MLA_NOPE), -jnp.ones((half,), F32), jnp.ones((half,), F32),
                           z(MLA_HEAD_PAD - MLA_NOPE - MLA_ROPE)])
    return frq[None, :], sgn[None, :]


def _block_diag_pool(w_pool):
    G, C, _ = w_pool.shape
    out = jnp.zeros((G * C, G * C), w_pool.dtype)
    for g in range(G):
        out = out.at[g * C:(g + 1) * C, g * C:(g + 1) * C].set(w_pool[g])
    return out.astype(BF16)


def kernel(x, c, positions, w_ada, b_ada, w_in, w_pool, s_pool, conv_w, conv_b, gate_b, gn_w, g_q,
           g_kv, w_uq, w_uk, w_uv, w_out, ln1_g, ln1_b, w_router, e_bias, w1, w3, w2, ws1, ws3, ws2,
           ln2_g, ln2_b):
    B, S, D = x.shape
    depth = w_in.shape[0]
    T = B * S
    H = ML_HEADS
    alpha = float((2 * depth) ** 0.25)
    n_assign = T * TOP_K
    n_blocks = n_assign // EXPERT_BLOCK + N_EXPERTS
    n_rows = n_blocks * EXPERT_BLOCK

    ada = _ada(c, w_ada, b_ada)
    frq, sgn = _rope_rows()
    pos3 = positions.reshape(B, S, 1)
    qk_scale = jnp.concatenate([jnp.ones((1, ML_WIDTH), F32),
                                jnp.full((1, ML_WIDTH), ML_DH ** -0.5, F32)], axis=1)
    ones_row = lambda n: jnp.ones((B, n, 1, S), BF16)
    zero_rows = lambda n, r: jnp.zeros((B, n, r, S), BF16)

    for l in range(depth):
        sh1, sc1, g1, sh2, sc2, g2 = [a.reshape(B, 1, D) for a in jnp.split(ada[l], 6, axis=-1)]
        u_pool, u_qk, u_v, u_o, u_dq, u_kv, u_gate = _inproj(x, sc1, sh1, _arrange_w_in(w_in[l]))
        y_pool = _pool(u_pool, _block_diag_pool(w_pool[l]), s_pool[l][None, :])

        qk_act = _conv(u_qk, conv_w[l], conv_b[l][None, :], qk_scale)
        heads_t = lambda a: a.reshape(B, S, H, ML_DH).transpose(0, 2, 3, 1)
        gates = u_gate[:, :, :4 * H].reshape(B, S, 4, H)
        vt_ml = jnp.concatenate([heads_t(u_v), ones_row(H), zero_rows(H, ML_VT_ROWS - ML_DH - 1)],
                                axis=2)
        y_ml = _mlstm(qk_act[:, :, ML_WIDTH:].reshape(B, S, H, ML_DH).transpose(0, 2, 1, 3),
                      heads_t(qk_act[:, :, :ML_WIDTH]), vt_ml, heads_t(u_o),
                      gates.transpose(0, 3, 1, 2),
                      gates.transpose(0, 3, 2, 1),
                      gate_b[l].reshape(4, H).T.reshape(H, 1, 4),
                      gate_b[l].reshape(4, H).T.reshape(H, 4, 1),
                      gn_w[l].reshape(H, ML_DH, 1))
        y_ml = y_ml.transpose(0, 3, 1, 2).reshape(B, S, ML_WIDTH)

        wq, wq_sw, wk, wk_sw = _arrange_mla(w_uq[l], w_uk[l])
        qh, kh, vh = _mla_proj(u_dq, u_kv, pos3, frq, sgn, g_q[l][None, :], g_kv[l][None, :],
                               wq, wq_sw, wk, wk_sw, w_uv[l].astype(BF16))
        y_mla = _attn(qh, kh, vh.transpose(0, 2, 1)).transpose(0, 2, 1)

        x1, h2, logits_t = _outproj(y_pool, y_ml, y_mla, x, g1, ln1_g[l][None, :], ln1_b[l][None, :],
                                    sc2, sh2, w_out[l].astype(BF16), w_router[l].T.astype(BF16), alpha)
        idx_t, w_t = _route(logits_t, e_bias[l][:, None])
        rank_t, counts = _rank(idx_t)
        counts = counts[:, 0].astype(I32)
        padded = (counts + EXPERT_BLOCK - 1) // EXPERT_BLOCK * EXPERT_BLOCK
        pend = jnp.cumsum(padded)
        pstart = pend - padded
        n_used = (pend[-1:] // EXPERT_BLOCK).astype(I32)
        blk_start = jnp.arange(n_blocks, dtype=I32) * EXPERT_BLOCK
        blk_expert = jnp.minimum(jnp.sum((pend[None, :] <= blk_start[:, None]).astype(I32), axis=1),
                                 N_EXPERTS - 1)
        pos_kt = _pos(idx_t, rank_t, pstart.astype(F32)[:, None]).transpose(1, 0, 2).reshape(TOP_K, T)
        xs = _dispatch(pos_kt, h2.reshape(T, D // 2), n_rows)
        ys = _experts(blk_expert, n_used, xs, w1, w3, w2, l)
        x = _combine(pos_kt, ys, w_t.transpose(0, 2, 1), h2, x1, g2, ln2_g[l][None, :],
                     ln2_b[l][None, :], ws1[l].astype(BF16), ws3[l].astype(BF16),
                     ws2[l].astype(BF16), alpha)
    return x
```

```python
import functools

import jax
import jax.numpy as jnp
import numpy as np
from jax import lax
from jax.experimental import pallas as pl
from jax.experimental.pallas import tpu as pltpu

F32 = jnp.float32
BF16 = jnp.bfloat16
I32 = jnp.int32

D_MODEL = 1024
POOL_WINDOWS = (2, 4, 8, 16)
POOL_GROUP_DIM = 64
POOL_WIDTH = 256
ML_HEADS = 4
ML_DH = 64
ML_WIDTH = 256
ML_CONV = 5
ML_VT_ROWS = 80
MLA_HEADS = 8
MLA_NOPE = 64
MLA_ROPE = 32
MLA_V = 64
MLA_Q_LORA = 256
MLA_KV_LORA = 128
MLA_HEAD_PAD = 128
ATTN_HEADS_PER_STEP = 2
ROPE_THETA = 10000.0
N_EXPERTS = 256
TOP_K = 8
N_GROUPS = 8
TOPK_GROUPS = 4
GROUP_SIZE = N_EXPERTS // N_GROUPS
D_EXPERT = 256
ROUTED_SCALE = 2.5
LN_EPS = 1e-5
RMS_EPS = 1e-6
NEG_BIG = -1e30
NEG_INF = float("-inf")

EXPERT_BLOCK = 512
EXPERT_SUB = 2
ROW_PIECES = D_MODEL // 2 // 128
VMEM_LIMIT = 56 * 1024 * 1024


def _cp(sem, vmem=None):
    return pltpu.CompilerParams(dimension_semantics=sem, vmem_limit_bytes=vmem)


def _ln(x):
    mu = jnp.mean(x, axis=-1, keepdims=True)
    xc = x - mu
    var = jnp.mean(xc * xc, axis=-1, keepdims=True)
    return xc * lax.rsqrt(var + LN_EPS)


def _silu(x):
    return x * jax.nn.sigmoid(x)


def _rms(x, g):
    return x * lax.rsqrt(jnp.mean(x * x, axis=-1, keepdims=True) + RMS_EPS) * g


def _dot(a, b):
    return jnp.dot(a, b, preferred_element_type=F32)


def _dot_nt(a, b):
    return lax.dot_general(a, b, (((1,), (1,)), ((), ())), preferred_element_type=F32)


def _dot_tn(a, b):
    return lax.dot_general(a, b, (((0,), (0,)), ((), ())), preferred_element_type=F32)


def _pack_pair(lo, hi):
    lo_b = lax.bitcast_convert_type(lo.astype(BF16).astype(F32), I32)
    hi_b = lax.bitcast_convert_type(hi.astype(BF16).astype(F32), I32)
    return jnp.bitwise_or(hi_b, lax.shift_right_logical(lo_b, 16))


def _unpack_pair(w):
    lo = lax.bitcast_convert_type(lax.shift_left(w, 16), F32)
    hi = lax.bitcast_convert_type(jnp.bitwise_and(w, -65536), F32)
    return lo, hi


def _ada_kernel(c_ref, w_ref, b_ref, o_ref):
    ca = _silu(c_ref[...]).astype(BF16)
    o_ref[...] = _dot(ca, w_ref[...].astype(BF16)) + b_ref[...]


def _ada(c, w_ada, b_ada):
    L, D, N = w_ada.shape
    B = c.shape[0]
    tn = 1536
    return pl.pallas_call(
        _ada_kernel,
        grid=(L, N // tn),
        in_specs=[pl.BlockSpec((B, D), lambda l, j: (0, 0)),
                  pl.BlockSpec((None, D, tn), lambda l, j: (l, 0, j)),
                  pl.BlockSpec((None, 1, tn), lambda l, j: (l, 0, j))],
        out_specs=pl.BlockSpec((None, B, tn), lambda l, j: (l, 0, j)),
        out_shape=jax.ShapeDtypeStruct((L, B, N), F32),
        compiler_params=_cp(("parallel", "parallel"), VMEM_LIMIT),
        name="ada",
    )(c, w_ada, b_ada.reshape(L, 1, N))


def _inproj_kernel(x_ref, sc_ref, sh_ref, w_ref, pool_ref, qk_ref, v_ref, o_ref, dq_ref,
                   kv_ref, gate_ref):
    h = (_ln(x_ref[...]) * (1.0 + sc_ref[...]) + sh_ref[...]).astype(BF16)
    pool_ref[...] = _dot(h, w_ref[:, 0:256])
    qk_ref[...] = _dot(h, w_ref[:, 256:768])
    v_ref[...] = _dot(h, w_ref[:, 768:1024]).astype(BF16)
    o_ref[...] = _dot(h, w_ref[:, 1024:1280])
    dq_ref[...] = _dot(h, w_ref[:, 1280:1536])
    kv_ref[...] = _dot(h, w_ref[:, 1536:1792])
    gate_ref[...] = _dot(h, w_ref[:, 1792:1920])


def _inproj(x, sc, sh, w):
    B, S, D = x.shape
    tm = min(512, S)
    widths = (256, 512, 256, 256, 256, 256, 128)
    dtypes = (F32, F32, BF16, F32, F32, F32, F32)
    row = lambda n: pl.BlockSpec((None, tm, n), lambda b, i: (b, i, 0))
    return pl.pallas_call(
        _inproj_kernel,
        grid=(B, S // tm),
        in_specs=[row(D),
                  pl.BlockSpec((None, 1, D), lambda b, i: (b, 0, 0)),
                  pl.BlockSpec((None, 1, D), lambda b, i: (b, 0, 0)),
                  pl.BlockSpec(w.shape, lambda b, i: (0, 0))],
        out_specs=[row(n) for n in widths],
        out_shape=[jax.ShapeDtypeStruct((B, S, n), dt) for n, dt in zip(widths, dtypes)],
        compiler_params=_cp(("parallel", "parallel"), VMEM_LIMIT),
        name="inproj",
    )(x, sc, sh, w)


def _shift_dn(a, k, row):
    return jnp.where(row >= k, pltpu.roll(a, k, 0), 0.0)


def _shift_up(a, k, row):
    n = a.shape[0]
    return jnp.where(row < n - k, pltpu.roll(a, n - k, 0), 0.0)


def _pool_kernel(u_ref, w_ref, s_ref, y_ref):
    x = u_ref[...]
    S, C = x.shape
    row = lax.broadcasted_iota(I32, (S, C), 0)
    lane = lax.broadcasted_iota(I32, (S, C), 1)
    rowf = row.astype(F32)
    p1, f1 = x, x
    p2 = p1 + _shift_dn(p1, 1, row)
    f2 = f1 + _shift_up(f1, 1, row)
    p4 = p2 + _shift_dn(p2, 2, row)
    f4 = f2 + _shift_up(f2, 2, row)
    p8 = p4 + _shift_dn(p4, 4, row)
    f8 = f4 + _shift_up(f4, 4, row)
    wins = []
    for half, p, f in ((1, p1, f1), (2, p2, f2), (4, p4, f4), (8, p8, f8)):
        total = _shift_dn(p, 1, row) + f
        cnt = jnp.minimum(rowf + half, float(S)) - jnp.maximum(rowf - half, 0.0)
        wins.append(total / cnt)
    pooled = jnp.where(lane < 64, wins[0],
                       jnp.where(lane < 128, wins[1], jnp.where(lane < 192, wins[2], wins[3])))
    d = (pooled - x).astype(BF16)
    y_ref[...] = (_dot(d, w_ref[...]) * s_ref[...]).astype(BF16)


def _pool(u_pool, w_bd, s_pool):
    B, S, C = u_pool.shape
    return pl.pallas_call(
        _pool_kernel,
        grid=(B,),
        in_specs=[pl.BlockSpec((None, S, C), lambda b: (b, 0, 0)),
                  pl.BlockSpec((C, C), lambda b: (0, 0)),
                  pl.BlockSpec((1, C), lambda b: (0, 0))],
        out_specs=pl.BlockSpec((None, S, C), lambda b: (b, 0, 0)),
        out_shape=jax.ShapeDtypeStruct((B, S, C), BF16),
        compiler_params=_cp(("parallel",), VMEM_LIMIT),
        name="pool",
    )(u_pool, w_bd, s_pool)


def _conv_kernel(u_ref, w_ref, b_ref, scale_ref, y_ref):
    x = u_ref[...]
    S, C = x.shape
    row = lax.broadcasted_iota(I32, (S, C), 0)
    y = (_shift_dn(x, 2, row) * w_ref[0:1, :] + _shift_dn(x, 1, row) * w_ref[1:2, :]
         + x * w_ref[2:3, :] + _shift_up(x, 1, row) * w_ref[3:4, :]
         + _shift_up(x, 2, row) * w_ref[4:5, :] + b_ref[...])
    y_ref[...] = (_silu(y) * scale_ref[...]).astype(BF16)


def _conv(u_qk, conv_w, conv_b, scale):
    B, S, C = u_qk.shape
    return pl.pallas_call(
        _conv_kernel,
        grid=(B,),
        in_specs=[pl.BlockSpec((None, S, C), lambda b: (b, 0, 0)),
                  pl.BlockSpec((ML_CONV, C), lambda b: (0, 0)),
                  pl.BlockSpec((1, C), lambda b: (0, 0)),
                  pl.BlockSpec((1, C), lambda b: (0, 0))],
        out_specs=pl.BlockSpec((None, S, C), lambda b: (b, 0, 0)),
        out_shape=jax.ShapeDtypeStruct((B, S, C), BF16),
        compiler_params=_cp(("parallel",), VMEM_LIMIT),
        name="mlstm_conv",
    )(u_qk, conv_w, conv_b, scale)


def _log_sigmoid(x):
    return jnp.minimum(x, 0.0) - jnp.log(1.0 + jnp.exp(-jnp.abs(x)))


def _split3(x):
    a = x.astype(BF16)
    r = x - a.astype(F32)
    b = r.astype(BF16)
    c = (r - b.astype(F32)).astype(BF16)
    return a, b, c


def _mlstm_chunk(s, k, qt, vt, c_col, b_row, i_row, b_last, allowed, ct, m_prev):
    logw = jnp.where(allowed, c_col + b_row, NEG_INF)
    m_inter = b_row + m_prev
    m_t = jnp.maximum(m_inter, jnp.max(logw, axis=0, keepdims=True))
    w_intra = jnp.exp(logw - m_t)
    w_inter = jnp.exp(m_inter - m_t)
    qk = (s * w_intra).astype(BF16)
    nd = _dot(vt, qk) + w_inter * _dot(ct.astype(BF16), qt)
    den = jnp.maximum(jnp.abs(nd[ML_DH:ML_DH + 1, :]), jnp.exp(-m_t))
    h = nd[:ML_DH, :] / den
    g = b_last - b_row + i_row
    m_new = jnp.maximum(b_last + m_prev, jnp.max(g, axis=1, keepdims=True))
    w_k = jnp.exp(g - m_new)
    decay = jnp.exp(b_last + m_prev - m_new)
    ct_new = decay * ct + _dot((vt.astype(F32) * w_k).astype(BF16), k)
    return h, ct_new, m_new


def _mlstm_kernel(k_ref, qt_ref, vt_ref, uo_ref, gc_ref, gr_ref, bc_ref, br_ref, gn_ref, y_ref,
                  hf_ref, hb_ref, *, chunk):
    S = k_ref.shape[0]
    L = chunk
    nc = S // L
    ri = lax.broadcasted_iota(I32, (L, L), 0)
    ci = lax.broadcasted_iota(I32, (L, L), 1)
    lower = ci <= ri
    upper = ci >= ri
    tril = jnp.where(lower, 1.0, 0.0).astype(BF16)
    triu = jnp.where(upper, 1.0, 0.0).astype(BF16)
    bias_c = bc_ref[...]
    bias_r = br_ref[...]

    def prep(r0, forward):
        gc = gc_ref[pl.ds(r0, L), :] + bias_c
        gr = gr_ref[:, pl.ds(r0, L)] + bias_r
        c1, c2, c3 = _split3(_log_sigmoid(gc))
        r1, r2, r3 = _split3(_log_sigmoid(gr))
        k = k_ref[pl.ds(r0, L), :]
        qt = qt_ref[:, pl.ds(r0, L)]
        s = _dot(k, qt)
        if forward:
            cum_c = (_dot(tril, c1) + _dot(tril, c2) + _dot(tril, c3))[:, 1:2]
            cum_r = (_dot(r1, triu) + _dot(r2, triu) + _dot(r3, triu))[1:2, :]
            return s, k, qt, gc[:, 0:1] - cum_c, cum_r, gr[0:1, :], cum_r[:, L - 1:L]
        cum_c = (_dot(triu, c1) + _dot(triu, c2) + _dot(triu, c3))[:, 3:4]
        cum_r = (_dot(r1, tril) + _dot(r2, tril) + _dot(r3, tril))[3:4, :]
        return s, k, qt, gc[:, 2:3] - cum_c, cum_r, gr[2:3, :], cum_r[:, 0:1]

    def step(i, carry):
        cf, mf, cb, mb = carry
        rf = pl.multiple_of(i * L, L)
        rb = pl.multiple_of((nc - 1 - i) * L, L)
        pf = prep(rf, True)
        pb = prep(rb, False)
        h, cf, mf = _mlstm_chunk(*pf[:3], vt_ref[:, pl.ds(rf, L)], *pf[3:], upper, cf, mf)
        hf_ref[:, pl.ds(rf, L)] = h
        h, cb, mb = _mlstm_chunk(*pb[:3], vt_ref[:, pl.ds(rb, L)], *pb[3:], lower, cb, mb)
        hb_ref[:, pl.ds(rb, L)] = h
        return cf, mf, cb, mb

    c0 = jnp.zeros((ML_VT_ROWS, ML_DH), F32)
    m0 = jnp.full((1, 1), NEG_BIG, F32)
    lax.fori_loop(0, nc, step, (c0, m0, c0, m0))
    h = hf_ref[...] + hb_ref[...]
    mu = jnp.mean(h, axis=0, keepdims=True)
    hc = h - mu
    var = jnp.mean(hc * hc, axis=0, keepdims=True)
    hn = hc * lax.rsqrt(var + LN_EPS)
    y_ref[...] = (jax.nn.sigmoid(uo_ref[...]) * (hn * gn_ref[...])).astype(BF16)


def _mlstm(k_heads, qt_heads, vt_ext, uot_heads, g_col, g_row, b_col, b_row, gn_w):
    B, H, S, dh = k_heads.shape
    L = min(256, S)
    tr = lambda n: pl.BlockSpec((None, None, n, S), lambda b, h: (b, h, 0, 0))
    return pl.pallas_call(
        functools.partial(_mlstm_kernel, chunk=L),
        grid=(B, H),
        in_specs=[pl.BlockSpec((None, None, S, dh), lambda b, h: (b, h, 0, 0)),
                  tr(dh), tr(ML_VT_ROWS), tr(dh),
                  pl.BlockSpec((None, None, S, 4), lambda b, h: (b, h, 0, 0)),
                  tr(4),
                  pl.BlockSpec((None, 1, 4), lambda b, h: (h, 0, 0)),
                  pl.BlockSpec((None, 4, 1), lambda b, h: (h, 0, 0)),
                  pl.BlockSpec((None, dh, 1), lambda b, h: (h, 0, 0))],
        out_specs=tr(dh),
        out_shape=jax.ShapeDtypeStruct((B, H, dh, S), BF16),
        scratch_shapes=[pltpu.VMEM((dh, S), F32), pltpu.VMEM((dh, S), F32)],
        compiler_params=_cp(("parallel", "parallel"), VMEM_LIMIT),
        name="mlstm_scan",
    )(k_heads, qt_heads, vt_ext, uot_heads, g_col, g_row, b_col, b_row, gn_w)


def _rope_kernel(pos_ref, frq_ref, sgn_ref, cos_ref, sin_ref):
    ang = pos_ref[...].astype(F32) * frq_ref[...]
    cos_ref[...] = jnp.cos(ang)
    sin_ref[...] = jnp.sin(ang) * sgn_ref[...]


def _rope_tables(pos, frq, sgn):
    B, S, _ = pos.shape
    tm = min(512, S)
    row = lambda n: pl.BlockSpec((None, tm, n), lambda b, i: (b, i, 0))
    full = lambda a: pl.BlockSpec(a.shape, lambda b, i: (0,) * a.ndim)
    return pl.pallas_call(
        _rope_kernel,
        grid=(B, S // tm),
        in_specs=[row(1), full(frq), full(sgn)],
        out_specs=[row(MLA_HEAD_PAD), row(MLA_HEAD_PAD)],
        out_shape=[jax.ShapeDtypeStruct((B, S, MLA_HEAD_PAD), F32)] * 2,
        compiler_params=_cp(("parallel", "parallel"), VMEM_LIMIT),
        name="rope_tables",
    )(pos, frq, sgn)


def _mla_proj_kernel(dq_ref, kv_ref, cos_ref, sin_ref, gq_ref, gkv_ref, wq_ref, wqs_ref,
                     wk_ref, wks_ref, wv_ref, q_ref, k_ref, v_ref, *, scale):
    cos = cos_ref[...]
    sin = sin_ref[...]
    qn = _rms(dq_ref[...], gq_ref[...]).astype(BF16)
    a = _dot(qn, wq_ref[...])
    a_sw = _dot(qn, wqs_ref[...])
    kv = kv_ref[...]
    ckv = _rms(kv[:, :MLA_KV_LORA], gkv_ref[...])
    kin = jnp.concatenate([ckv, kv[:, MLA_KV_LORA:]], axis=1).astype(BF16)
    ak = _dot(kin, wk_ref[...])
    ak_sw = _dot(kin, wks_ref[...])
    for h in range(MLA_HEADS):
        sl = slice(h * MLA_HEAD_PAD, (h + 1) * MLA_HEAD_PAD)
        q_ref[:, sl] = ((a[:, sl] * cos + a_sw[:, sl] * sin) * scale).astype(BF16)
        k_ref[:, sl] = (ak[:, sl] * cos + ak_sw[:, sl] * sin).astype(BF16)
    v_ref[...] = _dot(ckv.astype(BF16), wv_ref[...]).astype(BF16)


def _mla_proj(u_dq, u_kv, cos, sin, g_q, g_kv, wq, wqs, wk, wks, wv):
    B, S, _ = u_dq.shape
    tm = min(512, S)
    hw = MLA_HEADS * MLA_HEAD_PAD
    vw = MLA_HEADS * MLA_V
    row = lambda n: pl.BlockSpec((None, tm, n), lambda b, i: (b, i, 0))
    full = lambda a: pl.BlockSpec(a.shape, lambda b, i: (0,) * a.ndim)
    scale = (MLA_NOPE + MLA_ROPE) ** -0.5 * float(np.log2(np.e))
    return pl.pallas_call(
        functools.partial(_mla_proj_kernel, scale=scale),
        grid=(B, S // tm),
        in_specs=[row(MLA_Q_LORA), row(256), row(MLA_HEAD_PAD), row(MLA_HEAD_PAD), full(g_q), full(g_kv),
                  full(wq), full(wqs), full(wk), full(wks), full(wv)],
        out_specs=[row(hw), row(hw), row(vw)],
        out_shape=[jax.ShapeDtypeStruct((B, S, hw), BF16),
                   jax.ShapeDtypeStruct((B, S, hw), BF16),
                   jax.ShapeDtypeStruct((B, S, vw), BF16)],
        compiler_params=_cp(("parallel", "parallel"), VMEM_LIMIT),
        name="mla_proj",
    )(u_dq, u_kv, cos, sin, g_q, g_kv, wq, wqs, wk, wks, wv)


def _attn_kernel(q_ref, k_ref, vt_ref, o_ref, *, tk):
    tq = q_ref.shape[0]
    S = k_ref.shape[0]
    nk = S // tk
    def scores(h, j):
        k = k_ref[j * tk:(j + 1) * tk, h * MLA_HEAD_PAD:(h + 1) * MLA_HEAD_PAD]
        return _dot_nt(k, q_ref[:, h * MLA_HEAD_PAD:(h + 1) * MLA_HEAD_PAD])

    hp = q_ref.shape[1] // MLA_HEAD_PAD
    m = [jnp.full((1, tq), NEG_INF, F32)] * hp
    l = [jnp.zeros((1, tq), F32)] * hp
    acc = [jnp.zeros((MLA_V, tq), F32)] * hp
    s = [scores(h, 0) for h in range(hp)]
    for j in range(nk):
        for h in range(hp):
            s_cur = s[h]
            if j + 1 < nk:
                s[h] = scores(h, j + 1)
            m_new = jnp.maximum(m[h], jnp.max(s_cur, axis=0, keepdims=True))
            alpha = jnp.exp2(m[h] - m_new)
            p = jnp.exp2(s_cur - m_new)
            l[h] = alpha * l[h] + jnp.sum(p, axis=0, keepdims=True)
            vt = vt_ref[h * MLA_V:(h + 1) * MLA_V, j * tk:(j + 1) * tk]
            acc[h] = alpha * acc[h] + _dot(vt, p.astype(BF16))
            m[h] = m_new
    for h in range(hp):
        o_ref[h * MLA_V:(h + 1) * MLA_V, :] = (acc[h] / l[h]).astype(BF16)


def _attn(qh, kh, vt):
    B, S, _ = qh.shape
    tq = min(512, S)
    tk = min(256, S)
    hp = ATTN_HEADS_PER_STEP
    pw = hp * MLA_HEAD_PAD
    return pl.pallas_call(
        functools.partial(_attn_kernel, tk=tk),
        grid=(B, MLA_HEADS // hp, S // tq),
        in_specs=[pl.BlockSpec((None, tq, pw), lambda b, p, i: (b, i, p)),
                  pl.BlockSpec((None, S, pw), lambda b, p, i: (b, 0, p)),
                  pl.BlockSpec((None, hp * MLA_V, S), lambda b, p, i: (b, p, 0))],
        out_specs=pl.BlockSpec((None, hp * MLA_V, tq), lambda b, p, i: (b, p, i)),
        out_shape=jax.ShapeDtypeStruct((B, MLA_HEADS * MLA_V, S), BF16),
        compiler_params=_cp(("parallel", "parallel", "parallel"), VMEM_LIMIT),
        name="mla_attn",
    )(qh, kh, vt)


def _outproj_kernel(yp_ref, ym_ref, ya_ref, x_ref, g1_ref, lg_ref, lb_ref, sc_ref, sh_ref, wo_ref,
                    wr_ref, x1_ref, h2_ref, lt_ref, *, alpha):
    mix = (_dot(yp_ref[...], wo_ref[0:256, :]) + _dot(ym_ref[...], wo_ref[256:512, :])
           + _dot(ya_ref[...], wo_ref[512:1024, :]))
    x1 = _ln(alpha * x_ref[...] + g1_ref[...] * mix) * lg_ref[...] + lb_ref[...]
    x1_ref[...] = x1
    h2 = _ln(x1) * (1.0 + sc_ref[...]) + sh_ref[...]
    half = h2.shape[1] // 2
    h2_ref[...] = _pack_pair(h2[:, :half], h2[:, half:])
    lt_ref[...] = _dot_nt(wr_ref[...], h2.astype(BF16))


def _outproj(y_pool, y_ml, y_mla, x, g1, ln_g, ln_b, sc2, sh2, w_out, w_router_t, alpha):
    B, S, D = x.shape
    tm = min(512, S)
    row = lambda n: pl.BlockSpec((None, tm, n), lambda b, i: (b, i, 0))
    per_b = pl.BlockSpec((None, 1, D), lambda b, i: (b, 0, 0))
    vec = pl.BlockSpec((1, D), lambda b, i: (0, 0))
    return pl.pallas_call(
        functools.partial(_outproj_kernel, alpha=alpha),
        grid=(B, S // tm),
        in_specs=[row(256), row(256), row(512), row(D), per_b, vec, vec, per_b, per_b,
                  pl.BlockSpec((D, D), lambda b, i: (0, 0)),
                  pl.BlockSpec((N_EXPERTS, D), lambda b, i: (0, 0))],
        out_specs=[row(D), row(D // 2), pl.BlockSpec((None, N_EXPERTS, tm), lambda b, i: (b, 0, i))],
        out_shape=[jax.ShapeDtypeStruct((B, S, D), F32), jax.ShapeDtypeStruct((B, S, D // 2), I32),
                   jax.ShapeDtypeStruct((B, N_EXPERTS, S), F32)],
        compiler_params=_cp(("parallel", "parallel"), VMEM_LIMIT),
        name="outproj",
    )(y_pool, y_ml, y_mla, x, g1, ln_g, ln_b, sc2, sh2, w_out, w_router_t)


def _first_max(v, iota, n):
    m = jnp.max(v, axis=0, keepdims=True)
    first = jnp.min(jnp.where(v == m, iota, n), axis=0, keepdims=True)
    return m, first


def _route_kernel(lt_ref, eb_ref, idx_ref, w_ref):
    scores = jax.nn.sigmoid(lt_ref[...])
    sel = scores + eb_ref[...]
    E, tt = sel.shape
    gi = lax.broadcasted_iota(I32, (GROUP_SIZE, tt), 0)
    groups = [sel[g * GROUP_SIZE:(g + 1) * GROUP_SIZE, :] for g in range(N_GROUPS)]
    gs = []
    for grp in groups:
        m1, f1 = _first_max(grp, gi, GROUP_SIZE)
        m2 = jnp.max(jnp.where(gi == f1, NEG_INF, grp), axis=0, keepdims=True)
        gs.append(m1 + m2)
    chosen = [jnp.zeros((1, tt), F32) for _ in range(N_GROUPS)]
    for _ in range(TOPK_GROUPS):
        m = functools.reduce(jnp.maximum, gs)
        first = functools.reduce(
            jnp.minimum, [jnp.where(gs[g] == m, g, N_GROUPS) for g in range(N_GROUPS)])
        for g in range(N_GROUPS):
            hit = first == g
            chosen[g] = jnp.where(hit, 1.0, chosen[g])
            gs[g] = jnp.where(hit, NEG_INF, gs[g])
    cand = jnp.concatenate(
        [jnp.where(chosen[g] > 0.5, groups[g], NEG_INF) for g in range(N_GROUPS)], axis=0)
    ei = lax.broadcasted_iota(I32, (E, tt), 0)
    ws = []
    for k in range(TOP_K):
        _, f = _first_max(cand, ei, E)
        hit = ei == f
        idx_ref[k:k + 1, :] = f
        ws.append(jnp.sum(jnp.where(hit, scores, 0.0), axis=0, keepdims=True))
        cand = jnp.where(hit, NEG_INF, cand)
    norm = ROUTED_SCALE / functools.reduce(jnp.add, ws)
    for k in range(TOP_K):
        w_ref[k:k + 1, :] = ws[k] * norm


def _route(logits_t, e_bias):
    B, E, S = logits_t.shape
    tt = min(512, S)
    return pl.pallas_call(
        _route_kernel,
        grid=(B, S // tt),
        in_specs=[pl.BlockSpec((None, E, tt), lambda b, i: (b, 0, i)),
                  pl.BlockSpec((E, 1), lambda b, i: (0, 0))],
        out_specs=[pl.BlockSpec((None, TOP_K, tt), lambda b, i: (b, 0, i)),
                   pl.BlockSpec((None, TOP_K, tt), lambda b, i: (b, 0, i))],
        out_shape=[jax.ShapeDtypeStruct((B, TOP_K, S), I32),
                   jax.ShapeDtypeStruct((B, TOP_K, S), F32)],
        compiler_params=_cp(("parallel", "parallel"), VMEM_LIMIT),
        name="route",
    )(logits_t, e_bias)


def _rank_kernel(idx_ref, rank_ref, cnt_ref, carry_ref):
    first = jnp.logical_and(pl.program_id(0) == 0, pl.program_id(1) == 0)

    @pl.when(first)
    def _():
        carry_ref[...] = jnp.zeros_like(carry_ref)

    K, tt = idx_ref.shape
    E = carry_ref.shape[0]
    ei = lax.broadcasted_iota(I32, (E, tt), 0)
    ri = lax.broadcasted_iota(I32, (tt, tt), 0)
    ci = lax.broadcasted_iota(I32, (tt, tt), 1)
    before = jnp.where(ri < ci, 1.0, 0.0).astype(BF16)
    base = carry_ref[...]
    for k in range(K):
        hit = ei == idx_ref[k:k + 1, :]
        onehot = jnp.where(hit, 1.0, 0.0)
        prefix = _dot(onehot.astype(BF16), before)
        rank = jnp.sum(jnp.where(hit, base + prefix, 0.0), axis=0, keepdims=True)
        rank_ref[k:k + 1, :] = rank.astype(I32)
        base = base + jnp.sum(onehot, axis=1, keepdims=True)
    carry_ref[...] = base
    cnt_ref[...] = base


def _rank(idx_t):
    B, K, S = idx_t.shape
    tt = min(256, S)
    return pl.pallas_call(
        _rank_kernel,
        grid=(B, S // tt),
        in_specs=[pl.BlockSpec((None, K, tt), lambda b, i: (b, 0, i))],
        out_specs=[pl.BlockSpec((None, K, tt), lambda b, i: (b, 0, i)),
                   pl.BlockSpec((N_EXPERTS, 1), lambda b, i: (0, 0))],
        out_shape=[jax.ShapeDtypeStruct((B, K, S), I32),
                   jax.ShapeDtypeStruct((N_EXPERTS, 1), F32)],
        scratch_shapes=[pltpu.VMEM((N_EXPERTS, 1), F32)],
        compiler_params=_cp(("arbitrary", "arbitrary"), VMEM_LIMIT),
        name="rank",
    )(idx_t)


def _pos_kernel(idx_ref, rank_ref, start_ref, pos_ref):
    K, tt = idx_ref.shape
    ei = lax.broadcasted_iota(I32, (N_EXPERTS, tt), 0)
    start = start_ref[...]
    for k in range(K):
        hit = ei == idx_ref[k:k + 1, :]
        off = jnp.sum(jnp.where(hit, start, 0.0), axis=0, keepdims=True)
        pos_ref[k:k + 1, :] = rank_ref[k:k + 1, :] + off.astype(I32)


def _pos(idx_t, rank_t, pstart):
    B, K, S = idx_t.shape
    tt = min(512, S)
    spec = pl.BlockSpec((None, K, tt), lambda b, i: (b, 0, i))
    return pl.pallas_call(
        _pos_kernel,
        grid=(B, S // tt),
        in_specs=[spec, spec, pl.BlockSpec((N_EXPERTS, 1), lambda b, i: (0, 0))],
        out_specs=spec,
        out_shape=jax.ShapeDtypeStruct((B, K, S), I32),
        compiler_params=_cp(("parallel", "parallel"), VMEM_LIMIT),
        name="slot",
    )(idx_t, rank_t, pstart)


def _dispatch_kernel(pos_ref, h_ref, xs_ref, sem, *, tt):
    def copy(t, k):
        return pltpu.make_async_copy(h_ref.at[pl.ds(t, 1), :],
                                     xs_ref.at[pos_ref[k, t]], sem)

    for t in range(tt):
        for k in range(TOP_K):
            copy(t, k).start(priority=k % 2)
    for t in range(tt):
        for k in range(TOP_K):
            copy(t, k).wait()


def _dispatch(pos_kt, h2_flat, n_rows):
    T, W = h2_flat.shape
    tt = 256
    return pl.pallas_call(
        functools.partial(_dispatch_kernel, tt=tt),
        grid=(T // tt,),
        in_specs=[pl.BlockSpec((TOP_K, tt), lambda i: (0, i), memory_space=pltpu.SMEM),
                  pl.BlockSpec((tt, W), lambda i: (i, 0))],
        out_specs=pl.BlockSpec(memory_space=pl.ANY),
        out_shape=jax.ShapeDtypeStruct((n_rows, 1, W), I32),
        scratch_shapes=[pltpu.SemaphoreType.DMA(())],
        compiler_params=pltpu.CompilerParams(dimension_semantics=("arbitrary",),
                                             has_side_effects=True),
        name="dispatch",
    )(pos_kt, h2_flat)


def _expert_kernel(be_ref, nu_ref, xs_ref, w1_ref, w3_ref, w2_ref, ys_ref, w1b, w3b, w2b):
    b = pl.program_id(0)
    prev = be_ref[jnp.maximum(b - 1, 0)]
    fresh = jnp.logical_or(b == 0, be_ref[b] != prev)

    @pl.when(jnp.logical_and(fresh, b < nu_ref[0]))
    def _():
        w1b[...] = w1_ref[...].astype(BF16)
        w3b[...] = w3_ref[...].astype(BF16)
        w2b[...] = w2_ref[...].astype(BF16)

    @pl.when(b < nu_ref[0])
    def _():
        half = ROW_PIECES * 128
        sub = EXPERT_BLOCK // EXPERT_SUB
        xs, hs = [], []
        for i in range(EXPERT_SUB):
            pieces = [xs_ref[pl.ds(i * sub * ROW_PIECES + c, sub, stride=ROW_PIECES), :]
                      for c in range(ROW_PIECES)]
            lo, hi = _unpack_pair(jnp.concatenate(pieces, axis=1))
            xs.append((lo.astype(BF16), hi.astype(BF16)))
        for lo, hi in xs:
            h1 = _dot(lo, w1b[:half, :]) + _dot(hi, w1b[half:, :])
            h3 = _dot(lo, w3b[:half, :]) + _dot(hi, w3b[half:, :])
            hs.append((h1, h3))
        for i, (h1, h3) in enumerate(hs):
            y = _dot((_silu(h1) * h3).astype(BF16), w2b[...])
            words = _pack_pair(y[:, :half], y[:, half:])
            for c in range(ROW_PIECES):
                ys_ref[pl.ds(i * sub * ROW_PIECES + c, sub, stride=ROW_PIECES), :] = (
                    words[:, c * 128:(c + 1) * 128])


def _experts(blk_expert, n_used, xs, w1, w3, w2, layer):
    P, _, W = xs.shape
    D = 2 * W
    nb = P // EXPERT_BLOCK
    xs = xs.reshape(P * ROW_PIECES, 128)
    last = lambda b, be, nu: jnp.minimum(b, nu[0] - 1)
    wspec = lambda r, c: pl.BlockSpec((None, None, r, c),
                                      lambda b, be, nu: (layer, be[last(b, be, nu)], 0, 0))
    grid_spec = pltpu.PrefetchScalarGridSpec(
        num_scalar_prefetch=2,
        grid=(nb,),
        in_specs=[pl.BlockSpec((EXPERT_BLOCK * ROW_PIECES, 128), lambda b, be, nu: (last(b, be, nu), 0)),
                  wspec(D, D_EXPERT), wspec(D, D_EXPERT), wspec(D_EXPERT, D)],
        out_specs=pl.BlockSpec((EXPERT_BLOCK * ROW_PIECES, 128), lambda b, be, nu: (last(b, be, nu), 0)),
        scratch_shapes=[pltpu.VMEM((D, D_EXPERT), BF16), pltpu.VMEM((D, D_EXPERT), BF16),
                        pltpu.VMEM((D_EXPERT, D), BF16)],
    )
    ys = pl.pallas_call(
        _expert_kernel,
        grid_spec=grid_spec,
        out_shape=jax.ShapeDtypeStruct((P * ROW_PIECES, 128), I32),
        compiler_params=_cp(("arbitrary",), VMEM_LIMIT),
        name="experts",
    )(blk_expert, n_used, xs, w1, w3, w2)
    return ys.reshape(P, 1, W)


def _combine_kernel(pos_ref, ys_ref, wt_ref, h2_ref, x1_ref, g2_ref, lg_ref, lb_ref, ws1_ref,
                    ws3_ref, ws2_ref, o_ref, buf, sem, *, tt, alpha):
    def copy(t, k):
        return pltpu.make_async_copy(ys_ref.at[pos_ref[k, t]], buf.at[k, pl.ds(t, 1), :], sem)

    for t in range(tt):
        for k in range(TOP_K):
            copy(t, k).start(priority=k % 2)
    half = h2_ref.shape[1]
    lo, hi = _unpack_pair(h2_ref[...])
    lo, hi = lo.astype(BF16), hi.astype(BF16)
    h1 = _dot(lo, ws1_ref[:half, :]) + _dot(hi, ws1_ref[half:, :])
    h3 = _dot(lo, ws3_ref[:half, :]) + _dot(hi, ws3_ref[half:, :])
    shared = _dot((_silu(h1) * h3).astype(BF16), ws2_ref[...])
    for t in range(tt):
        for k in range(TOP_K):
            copy(t, k).wait()
    wt = wt_ref[...]
    f_lo, f_hi = shared[:, :half], shared[:, half:]
    for k in range(TOP_K):
        y_lo, y_hi = _unpack_pair(buf[k])
        f_lo = f_lo + wt[:, k:k + 1] * y_lo
        f_hi = f_hi + wt[:, k:k + 1] * y_hi
    ffn = jnp.concatenate([f_lo, f_hi], axis=1)
    o_ref[...] = _ln(alpha * x1_ref[...] + g2_ref[...] * ffn) * lg_ref[...] + lb_ref[...]


def _combine(pos_kt, ys, w_tk, h2, x1, g2, ln_g, ln_b, ws1, ws3, ws2, alpha):
    B, S, D = x1.shape
    tt = min(256, S)
    nt = S // tt
    row = lambda n: pl.BlockSpec((None, tt, n), lambda b, i: (b, i, 0))
    full = lambda a: pl.BlockSpec(a.shape, lambda b, i: (0,) * a.ndim)
    return pl.pallas_call(
        functools.partial(_combine_kernel, tt=tt, alpha=alpha),
        grid=(B, nt),
        in_specs=[pl.BlockSpec((TOP_K, tt), lambda b, i: (0, b * nt + i), memory_space=pltpu.SMEM),
                  pl.BlockSpec(memory_space=pl.ANY),
                  row(TOP_K), row(D // 2), row(D),
                  pl.BlockSpec((None, 1, D), lambda b, i: (b, 0, 0)),
                  full(ln_g), full(ln_b), full(ws1), full(ws3), full(ws2)],
        out_specs=row(D),
        out_shape=jax.ShapeDtypeStruct((B, S, D), F32),
        scratch_shapes=[pltpu.VMEM((TOP_K, tt, D // 2), I32), pltpu.SemaphoreType.DMA(())],
        compiler_params=_cp(("arbitrary", "arbitrary"), VMEM_LIMIT),
        name="combine",
    )(pos_kt, ys, w_tk, h2, x1, g2, ln_g, ln_b, ws1, ws3, ws2)


def _arrange_w_in(w):
    z = lambda n: jnp.zeros((w.shape[0], n), w.dtype)
    return jnp.concatenate([w[:, 0:1280], w[:, 1296:1552], w[:, 1552:1680], w[:, 1680:1712],
                            z(96), w[:, 1280:1296], z(112)], axis=1).astype(BF16)


def _head_layout(nope, rope_a, rope_b):
    r = nope.shape[0]
    z = jnp.zeros((r, MLA_HEADS, MLA_HEAD_PAD - MLA_NOPE - MLA_ROPE), nope.dtype)
    return jnp.concatenate([nope, rope_a, rope_b, z], axis=2).reshape(r, MLA_HEADS * MLA_HEAD_PAD)


def _arrange_mla(w_uq, w_uk):
    half = MLA_ROPE // 2
    q = w_uq.reshape(MLA_Q_LORA, MLA_HEADS, MLA_NOPE + MLA_ROPE)
    qn, q1, q2 = q[..., :MLA_NOPE], q[..., MLA_NOPE:MLA_NOPE + half], q[..., MLA_NOPE + half:]
    wq = _head_layout(qn, q1, q2)
    wq_sw = _head_layout(jnp.zeros_like(qn), q2, q1)
    kn = w_uk.reshape(MLA_KV_LORA, MLA_HEADS, MLA_NOPE)
    zk = jnp.zeros((MLA_KV_LORA, MLA_HEADS, half), w_uk.dtype)
    eye = jnp.eye(MLA_ROPE, dtype=w_uk.dtype)
    e1 = jnp.broadcast_to(eye[:, None, :half], (MLA_ROPE, MLA_HEADS, half))
    e2 = jnp.broadcast_to(eye[:, None, half:], (MLA_ROPE, MLA_HEADS, half))
    zn = jnp.zeros((MLA_ROPE, MLA_HEADS, MLA_NOPE), w_uk.dtype)
    pad = jnp.zeros((128 - MLA_ROPE, MLA_HEADS * MLA_HEAD_PAD), w_uk.dtype)
    wk = jnp.concatenate([_head_layout(kn, zk, zk), _head_layout(zn, e1, e2), pad], axis=0)
    wk_sw = jnp.concatenate([_head_layout(jnp.zeros_like(kn), zk, zk), _head_layout(zn, e2, e1), pad],
                            axis=0)
    return wq.astype(BF16), wq_sw.astype(BF16), wk.astype(BF16), wk_sw.astype(BF16)


def _rope_rows():
    half = MLA_ROPE // 2
    inv_freq = ROPE_THETA ** (-jnp.arange(0, MLA_ROPE, 2, dtype=F32) / MLA_ROPE)
    z = lambda n: jnp.zeros((n,), F32)
    frq = jnp.concatenate([z(MLA_NOPE), inv_freq, inv_freq, z(MLA_HEAD_PAD - MLA_NOPE - MLA_ROPE)])
    sgn = jnp.concatenate([z(MLA_NOPE), -jnp.ones((half,), F32), jnp.ones((half,), F32),
                           z(MLA_HEAD_PAD - MLA_NOPE - MLA_ROPE)])
    return frq[None, :], sgn[None, :]


def _block_diag_pool(w_pool):
    G, C, _ = w_pool.shape
    out = jnp.zeros((G * C, G * C), w_pool.dtype)
    for g in range(G):
        out = out.at[g * C:(g + 1) * C, g * C:(g + 1) * C].set(w_pool[g])
    return out.astype(BF16)


def kernel(x, c, positions, w_ada, b_ada, w_in, w_pool, s_pool, conv_w, conv_b, gate_b, gn_w, g_q,
           g_kv, w_uq, w_uk, w_uv, w_out, ln1_g, ln1_b, w_router, e_bias, w1, w3, w2, ws1, ws3, ws2,
           ln2_g, ln2_b):
    B, S, D = x.shape
    depth = w_in.shape[0]
    T = B * S
    H = ML_HEADS
    alpha = float((2 * depth) ** 0.25)
    n_assign = T * TOP_K
    n_blocks = n_assign // EXPERT_BLOCK + N_EXPERTS
    n_rows = n_blocks * EXPERT_BLOCK

    ada = _ada(c, w_ada, b_ada)
    rope_cos, rope_sin = _rope_tables(positions.reshape(B, S, 1), *_rope_rows())
    qk_scale = jnp.concatenate([jnp.ones((1, ML_WIDTH), F32),
                                jnp.full((1, ML_WIDTH), ML_DH ** -0.5, F32)], axis=1)
    ones_row = lambda n: jnp.ones((B, n, 1, S), BF16)
    zero_rows = lambda n, r: jnp.zeros((B, n, r, S), BF16)

    for l in range(depth):
        sh1, sc1, g1, sh2, sc2, g2 = [a.reshape(B, 1, D) for a in jnp.split(ada[l], 6, axis=-1)]
        u_pool, u_qk, u_v, u_o, u_dq, u_kv, u_gate = _inproj(x, sc1, sh1, _arrange_w_in(w_in[l]))
        y_pool = _pool(u_pool, _block_diag_pool(w_pool[l]), s_pool[l][None, :])

        qk_act = _conv(u_qk, conv_w[l], conv_b[l][None, :], qk_scale)
        heads_t = lambda a: a.reshape(B, S, H, ML_DH).transpose(0, 2, 3, 1)
        gates = u_gate[:, :, :4 * H].reshape(B, S, 4, H)
        vt_ml = jnp.concatenate([heads_t(u_v), ones_row(H), zero_rows(H, ML_VT_ROWS - ML_DH - 1)],
                                axis=2)
        y_ml = _mlstm(qk_act[:, :, ML_WIDTH:].reshape(B, S, H, ML_DH).transpose(0, 2, 1, 3),
                      heads_t(qk_act[:, :, :ML_WIDTH]), vt_ml, heads_t(u_o),
                      gates.transpose(0, 3, 1, 2),
                      gates.transpose(0, 3, 2, 1),
                      gate_b[l].reshape(4, H).T.reshape(H, 1, 4),
                      gate_b[l].reshape(4, H).T.reshape(H, 4, 1),
                      gn_w[l].reshape(H, ML_DH, 1))
        y_ml = y_ml.transpose(0, 3, 1, 2).reshape(B, S, ML_WIDTH)

        wq, wq_sw, wk, wk_sw = _arrange_mla(w_uq[l], w_uk[l])
        qh, kh, vh = _mla_proj(u_dq, u_kv, rope_cos, rope_sin, g_q[l][None, :], g_kv[l][None, :],
                               wq, wq_sw, wk, wk_sw, w_uv[l].astype(BF16))
        y_mla = _attn(qh, kh, vh.transpose(0, 2, 1)).transpose(0, 2, 1)

        x1, h2, logits_t = _outproj(y_pool, y_ml, y_mla, x, g1, ln1_g[l][None, :], ln1_b[l][None, :],
                                    sc2, sh2, w_out[l].astype(BF16), w_router[l].T.astype(BF16), alpha)
        idx_t, w_t = _route(logits_t, e_bias[l][:, None])
        rank_t, counts = _rank(idx_t)
        counts = counts[:, 0].astype(I32)
        padded = (counts + EXPERT_BLOCK - 1) // EXPERT_BLOCK * EXPERT_BLOCK
        pend = jnp.cumsum(padded)
        pstart = pend - padded
        n_used = (pend[-1:] // EXPERT_BLOCK).astype(I32)
        blk_start = jnp.arange(n_blocks, dtype=I32) * EXPERT_BLOCK
        blk_expert = jnp.minimum(jnp.sum((pend[None, :] <= blk_start[:, None]).astype(I32), axis=1),
                                 N_EXPERTS - 1)
        pos_kt = _pos(idx_t, rank_t, pstart.astype(F32)[:, None]).transpose(1, 0, 2).reshape(TOP_K, T)
        xs = _dispatch(pos_kt, h2.reshape(T, D // 2), n_rows)
        ys = _experts(blk_expert, n_used, xs, w1, w3, w2, l)
        x = _combine(pos_kt, ys, w_t.transpose(0, 2, 1), h2, x1, g2, ln2_g[l][None, :],
                     ln2_b[l][None, :], ws1[l].astype(BF16), ws3[l].astype(BF16),
                     ws2[l].astype(BF16), alpha)
    return x
```

```python
import functools

import jax
import jax.numpy as jnp
import numpy as np
from jax import lax
from jax.experimental import pallas as pl
from jax.experimental.pallas import tpu as pltpu

F32 = jnp.float32
BF16 = jnp.bfloat16
I32 = jnp.int32

D_MODEL = 1024
POOL_WINDOWS = (2, 4, 8, 16)
POOL_GROUP_DIM = 64
POOL_WIDTH = 256
ML_HEADS = 4
ML_DH = 64
ML_WIDTH = 256
ML_CONV = 5
ML_VT_ROWS = 80
MLA_HEADS = 8
MLA_NOPE = 64
MLA_ROPE = 32
MLA_V = 64
MLA_Q_LORA = 256
MLA_KV_LORA = 128
MLA_HEAD_PAD = 128
ATTN_HEADS_PER_STEP = 2
ATTN_LOOKAHEAD = 1
ROPE_THETA = 10000.0
N_EXPERTS = 256
TOP_K = 8
N_GROUPS = 8
TOPK_GROUPS = 4
GROUP_SIZE = N_EXPERTS // N_GROUPS
D_EXPERT = 256
ROUTED_SCALE = 2.5
LN_EPS = 1e-5
RMS_EPS = 1e-6
NEG_BIG = -1e30
NEG_INF = float("-inf")

EXPERT_BLOCK = 512
EXPERT_SUB = 2
ROW_PIECES = D_MODEL // 2 // 128
VMEM_LIMIT = 56 * 1024 * 1024


def _cp(sem, vmem=None):
    return pltpu.CompilerParams(dimension_semantics=sem, vmem_limit_bytes=vmem)


def _ln(x):
    mu = jnp.mean(x, axis=-1, keepdims=True)
    xc = x - mu
    var = jnp.mean(xc * xc, axis=-1, keepdims=True)
    return xc * lax.rsqrt(var + LN_EPS)


def _silu(x):
    return x * jax.nn.sigmoid(x)


def _rms(x, g):
    return x * lax.rsqrt(jnp.mean(x * x, axis=-1, keepdims=True) + RMS_EPS) * g


def _dot(a, b):
    return jnp.dot(a, b, preferred_element_type=F32)


def _dot_nt(a, b):
    return lax.dot_general(a, b, (((1,), (1,)), ((), ())), preferred_element_type=F32)


def _dot_tn(a, b):
    return lax.dot_general(a, b, (((0,), (0,)), ((), ())), preferred_element_type=F32)


def _pack_pair(lo, hi):
    lo_b = lax.bitcast_convert_type(lo.astype(BF16).astype(F32), I32)
    hi_b = lax.bitcast_convert_type(hi.astype(BF16).astype(F32), I32)
    return jnp.bitwise_or(hi_b, lax.shift_right_logical(lo_b, 16))


def _unpack_pair(w):
    lo = lax.bitcast_convert_type(lax.shift_left(w, 16), F32)
    hi = lax.bitcast_convert_type(jnp.bitwise_and(w, -65536), F32)
    return lo, hi


def _ada_kernel(c_ref, w_ref, b_ref, o_ref):
    ca = _silu(c_ref[...]).astype(BF16)
    o_ref[...] = _dot(ca, w_ref[...].astype(BF16)) + b_ref[...]


def _ada(c, w_ada, b_ada):
    L, D, N = w_ada.shape
    B = c.shape[0]
    tn = 1536
    return pl.pallas_call(
        _ada_kernel,
        grid=(L, N // tn),
        in_specs=[pl.BlockSpec((B, D), lambda l, j: (0, 0)),
                  pl.BlockSpec((None, D, tn), lambda l, j: (l, 0, j)),
                  pl.BlockSpec((None, 1, tn), lambda l, j: (l, 0, j))],
        out_specs=pl.BlockSpec((None, B, tn), lambda l, j: (l, 0, j)),
        out_shape=jax.ShapeDtypeStruct((L, B, N), F32),
        compiler_params=_cp(("parallel", "parallel"), VMEM_LIMIT),
        name="ada",
    )(c, w_ada, b_ada.reshape(L, 1, N))


def _inproj_kernel(x_ref, sc_ref, sh_ref, w_ref, pool_ref, qk_ref, v_ref, o_ref, dq_ref,
                   kv_ref, gate_ref):
    h = (_ln(x_ref[...]) * (1.0 + sc_ref[...]) + sh_ref[...]).astype(BF16)
    pool_ref[...] = _dot(h, w_ref[:, 0:256])
    qk_ref[...] = _dot(h, w_ref[:, 256:768])
    v_ref[...] = _dot(h, w_ref[:, 768:1024]).astype(BF16)
    o_ref[...] = _dot(h, w_ref[:, 1024:1280])
    dq_ref[...] = _dot(h, w_ref[:, 1280:1536])
    kv_ref[...] = _dot(h, w_ref[:, 1536:1792])
    gate_ref[...] = _dot(h, w_ref[:, 1792:1920])


def _inproj(x, sc, sh, w):
    B, S, D = x.shape
    tm = min(512, S)
    widths = (256, 512, 256, 256, 256, 256, 128)
    dtypes = (F32, F32, BF16, F32, F32, F32, F32)
    row = lambda n: pl.BlockSpec((None, tm, n), lambda b, i: (b, i, 0))
    return pl.pallas_call(
        _inproj_kernel,
        grid=(B, S // tm),
        in_specs=[row(D),
                  pl.BlockSpec((None, 1, D), lambda b, i: (b, 0, 0)),
                  pl.BlockSpec((None, 1, D), lambda b, i: (b, 0, 0)),
                  pl.BlockSpec(w.shape, lambda b, i: (0, 0))],
        out_specs=[row(n) for n in widths],
        out_shape=[jax.ShapeDtypeStruct((B, S, n), dt) for n, dt in zip(widths, dtypes)],
        compiler_params=_cp(("parallel", "parallel"), VMEM_LIMIT),
        name="inproj",
    )(x, sc, sh, w)


def _shift_dn(a, k, row):
    return jnp.where(row >= k, pltpu.roll(a, k, 0), 0.0)


def _shift_up(a, k, row):
    n = a.shape[0]
    return jnp.where(row < n - k, pltpu.roll(a, n - k, 0), 0.0)


def _pool_kernel(u_ref, w_ref, s_ref, y_ref):
    x = u_ref[...]
    S, C = x.shape
    row = lax.broadcasted_iota(I32, (S, C), 0)
    lane = lax.broadcasted_iota(I32, (S, C), 1)
    rowf = row.astype(F32)
    p1, f1 = x, x
    p2 = p1 + _shift_dn(p1, 1, row)
    f2 = f1 + _shift_up(f1, 1, row)
    p4 = p2 + _shift_dn(p2, 2, row)
    f4 = f2 + _shift_up(f2, 2, row)
    p8 = p4 + _shift_dn(p4, 4, row)
    f8 = f4 + _shift_up(f4, 4, row)
    wins = []
    for half, p, f in ((1, p1, f1), (2, p2, f2), (4, p4, f4), (8, p8, f8)):
        total = _shift_dn(p, 1, row) + f
        cnt = jnp.minimum(rowf + half, float(S)) - jnp.maximum(rowf - half, 0.0)
        wins.append(total / cnt)
    pooled = jnp.where(lane < 64, wins[0],
                       jnp.where(lane < 128, wins[1], jnp.where(lane < 192, wins[2], wins[3])))
    d = (pooled - x).astype(BF16)
    y_ref[...] = (_dot(d, w_ref[...]) * s_ref[...]).astype(BF16)


def _pool(u_pool, w_bd, s_pool):
    B, S, C = u_pool.shape
    return pl.pallas_call(
        _pool_kernel,
        grid=(B,),
        in_specs=[pl.BlockSpec((None, S, C), lambda b: (b, 0, 0)),
                  pl.BlockSpec((C, C), lambda b: (0, 0)),
                  pl.BlockSpec((1, C), lambda b: (0, 0))],
        out_specs=pl.BlockSpec((None, S, C), lambda b: (b, 0, 0)),
        out_shape=jax.ShapeDtypeStruct((B, S, C), BF16),
        compiler_params=_cp(("parallel",), VMEM_LIMIT),
        name="pool",
    )(u_pool, w_bd, s_pool)


def _conv_kernel(u_ref, w_ref, b_ref, scale_ref, y_ref):
    x = u_ref[...]
    S, C = x.shape
    row = lax.broadcasted_iota(I32, (S, C), 0)
    y = (_shift_dn(x, 2, row) * w_ref[0:1, :] + _shift_dn(x, 1, row) * w_ref[1:2, :]
         + x * w_ref[2:3, :] + _shift_up(x, 1, row) * w_ref[3:4, :]
         + _shift_up(x, 2, row) * w_ref[4:5, :] + b_ref[...])
    y_ref[...] = (_silu(y) * scale_ref[...]).astype(BF16)


def _conv(u_qk, conv_w, conv_b, scale):
    B, S, C = u_qk.shape
    return pl.pallas_call(
        _conv_kernel,
        grid=(B,),
        in_specs=[pl.BlockSpec((None, S, C), lambda b: (b, 0, 0)),
                  pl.BlockSpec((ML_CONV, C), lambda b: (0, 0)),
                  pl.BlockSpec((1, C), lambda b: (0, 0)),
                  pl.BlockSpec((1, C), lambda b: (0, 0))],
        out_specs=pl.BlockSpec((None, S, C), lambda b: (b, 0, 0)),
        out_shape=jax.ShapeDtypeStruct((B, S, C), BF16),
        compiler_params=_cp(("parallel",), VMEM_LIMIT),
        name="mlstm_conv",
    )(u_qk, conv_w, conv_b, scale)


def _log_sigmoid(x):
    return jnp.minimum(x, 0.0) - jnp.log(1.0 + jnp.exp(-jnp.abs(x)))


def _split3(x):
    a = x.astype(BF16)
    r = x - a.astype(F32)
    b = r.astype(BF16)
    c = (r - b.astype(F32)).astype(BF16)
    return a, b, c


def _mlstm_chunk(s, k, qt, vt, c_col, b_row, i_row, b_last, allowed, ct, m_prev):
    logw = jnp.where(allowed, c_col + b_row, NEG_INF)
    m_inter = b_row + m_prev
    m_t = jnp.maximum(m_inter, jnp.max(logw, axis=0, keepdims=True))
    w_intra = jnp.exp(logw - m_t)
    w_inter = jnp.exp(m_inter - m_t)
    qk = (s * w_intra).astype(BF16)
    nd = _dot(vt, qk) + w_inter * _dot(ct.astype(BF16), qt)
    den = jnp.maximum(jnp.abs(nd[ML_DH:ML_DH + 1, :]), jnp.exp(-m_t))
    h = nd[:ML_DH, :] / den
    g = b_last - b_row + i_row
    m_new = jnp.maximum(b_last + m_prev, jnp.max(g, axis=1, keepdims=True))
    w_k = jnp.exp(g - m_new)
    decay = jnp.exp(b_last + m_prev - m_new)
    ct_new = decay * ct + _dot((vt.astype(F32) * w_k).astype(BF16), k)
    return h, ct_new, m_new


def _mlstm_kernel(k_ref, qt_ref, vt_ref, uo_ref, gc_ref, gr_ref, bc_ref, br_ref, gn_ref, y_ref,
                  hf_ref, hb_ref, *, chunk):
    S = k_ref.shape[0]
    L = chunk
    nc = S // L
    ri = lax.broadcasted_iota(I32, (L, L), 0)
    ci = lax.broadcasted_iota(I32, (L, L), 1)
    lower = ci <= ri
    upper = ci >= ri
    tril = jnp.where(lower, 1.0, 0.0).astype(BF16)
    triu = jnp.where(upper, 1.0, 0.0).astype(BF16)
    bias_c = bc_ref[...]
    bias_r = br_ref[...]

    def prep(r0, forward):
        gc = gc_ref[pl.ds(r0, L), :] + bias_c
        gr = gr_ref[:, pl.ds(r0, L)] + bias_r
        c1, c2, c3 = _split3(_log_sigmoid(gc))
        r1, r2, r3 = _split3(_log_sigmoid(gr))
        k = k_ref[pl.ds(r0, L), :]
        qt = qt_ref[:, pl.ds(r0, L)]
        s = _dot(k, qt)
        if forward:
            cum_c = (_dot(tril, c1) + _dot(tril, c2) + _dot(tril, c3))[:, 1:2]
            cum_r = (_dot(r1, triu) + _dot(r2, triu) + _dot(r3, triu))[1:2, :]
            return s, k, qt, gc[:, 0:1] - cum_c, cum_r, gr[0:1, :], cum_r[:, L - 1:L]
        cum_c = (_dot(triu, c1) + _dot(triu, c2) + _dot(triu, c3))[:, 3:4]
        cum_r = (_dot(r1, tril) + _dot(r2, tril) + _dot(r3, tril))[3:4, :]
        return s, k, qt, gc[:, 2:3] - cum_c, cum_r, gr[2:3, :], cum_r[:, 0:1]

    def step(i, carry):
        cf, mf, cb, mb = carry
        rf = pl.multiple_of(i * L, L)
        rb = pl.multiple_of((nc - 1 - i) * L, L)
        pf = prep(rf, True)
        pb = prep(rb, False)
        h, cf, mf = _mlstm_chunk(*pf[:3], vt_ref[:, pl.ds(rf, L)], *pf[3:], upper, cf, mf)
        hf_ref[:, pl.ds(rf, L)] = h
        h, cb, mb = _mlstm_chunk(*pb[:3], vt_ref[:, pl.ds(rb, L)], *pb[3:], lower, cb, mb)
        hb_ref[:, pl.ds(rb, L)] = h
        return cf, mf, cb, mb

    c0 = jnp.zeros((ML_VT_ROWS, ML_DH), F32)
    m0 = jnp.full((1, 1), NEG_BIG, F32)
    lax.fori_loop(0, nc, step, (c0, m0, c0, m0))
    h = hf_ref[...] + hb_ref[...]
    mu = jnp.mean(h, axis=0, keepdims=True)
    hc = h - mu
    var = jnp.mean(hc * hc, axis=0, keepdims=True)
    hn = hc * lax.rsqrt(var + LN_EPS)
    y_ref[...] = (jax.nn.sigmoid(uo_ref[...]) * (hn * gn_ref[...])).astype(BF16)


def _mlstm(k_heads, qt_heads, vt_ext, uot_heads, g_col, g_row, b_col, b_row, gn_w):
    B, H, S, dh = k_heads.shape
    L = min(256, S)
    tr = lambda n: pl.BlockSpec((None, None, n, S), lambda b, h: (b, h, 0, 0))
    return pl.pallas_call(
        functools.partial(_mlstm_kernel, chunk=L),
        grid=(B, H),
        in_specs=[pl.BlockSpec((None, None, S, dh), lambda b, h: (b, h, 0, 0)),
                  tr(dh), tr(ML_VT_ROWS), tr(dh),
                  pl.BlockSpec((None, None, S, 4), lambda b, h: (b, h, 0, 0)),
                  tr(4),
                  pl.BlockSpec((None, 1, 4), lambda b, h: (h, 0, 0)),
                  pl.BlockSpec((None, 4, 1), lambda b, h: (h, 0, 0)),
                  pl.BlockSpec((None, dh, 1), lambda b, h: (h, 0, 0))],
        out_specs=tr(dh),
        out_shape=jax.ShapeDtypeStruct((B, H, dh, S), BF16),
        scratch_shapes=[pltpu.VMEM((dh, S), F32), pltpu.VMEM((dh, S), F32)],
        compiler_params=_cp(("parallel", "parallel"), VMEM_LIMIT),
        name="mlstm_scan",
    )(k_heads, qt_heads, vt_ext, uot_heads, g_col, g_row, b_col, b_row, gn_w)


def _rope_kernel(pos_ref, frq_ref, sgn_ref, cos_ref, sin_ref):
    ang = pos_ref[...].astype(F32) * frq_ref[...]
    cos_ref[...] = jnp.cos(ang)
    sin_ref[...] = jnp.sin(ang) * sgn_ref[...]


def _rope_tables(pos, frq, sgn):
    B, S, _ = pos.shape
    tm = min(512, S)
    row = lambda n: pl.BlockSpec((None, tm, n), lambda b, i: (b, i, 0))
    full = lambda a: pl.BlockSpec(a.shape, lambda b, i: (0,) * a.ndim)
    return pl.pallas_call(
        _rope_kernel,
        grid=(B, S // tm),
        in_specs=[row(1), full(frq), full(sgn)],
        out_specs=[row(MLA_HEAD_PAD), row(MLA_HEAD_PAD)],
        out_shape=[jax.ShapeDtypeStruct((B, S, MLA_HEAD_PAD), F32)] * 2,
        compiler_params=_cp(("parallel", "parallel"), VMEM_LIMIT),
        name="rope_tables",
    )(pos, frq, sgn)


def _mla_proj_kernel(dq_ref, kv_ref, cos_ref, sin_ref, gq_ref, gkv_ref, wq_ref, wqs_ref,
                     wk_ref, wks_ref, wv_ref, q_ref, k_ref, v_ref, *, scale):
    cos = cos_ref[...]
    sin = sin_ref[...]
    qn = _rms(dq_ref[...], gq_ref[...]).astype(BF16)
    a = _dot(qn, wq_ref[...])
    a_sw = _dot(qn, wqs_ref[...])
    kv = kv_ref[...]
    ckv = _rms(kv[:, :MLA_KV_LORA], gkv_ref[...])
    kin = jnp.concatenate([ckv, kv[:, MLA_KV_LORA:]], axis=1).astype(BF16)
    ak = _dot(kin, wk_ref[...])
    ak_sw = _dot(kin, wks_ref[...])
    for h in range(MLA_HEADS):
        sl = slice(h * MLA_HEAD_PAD, (h + 1) * MLA_HEAD_PAD)
        q_ref[:, sl] = ((a[:, sl] * cos + a_sw[:, sl] * sin) * scale).astype(BF16)
        k_ref[:, sl] = (ak[:, sl] * cos + ak_sw[:, sl] * sin).astype(BF16)
    v_ref[...] = _dot(ckv.astype(BF16), wv_ref[...]).astype(BF16)


def _mla_proj(u_dq, u_kv, cos, sin, g_q, g_kv, wq, wqs, wk, wks, wv):
    B, S, _ = u_dq.shape
    tm = min(512, S)
    hw = MLA_HEADS * MLA_HEAD_PAD
    vw = MLA_HEADS * MLA_V
    row = lambda n: pl.BlockSpec((None, tm, n), lambda b, i: (b, i, 0))
    full = lambda a: pl.BlockSpec(a.shape, lambda b, i: (0,) * a.ndim)
    scale = (MLA_NOPE + MLA_ROPE) ** -0.5 * float(np.log2(np.e))
    return pl.pallas_call(
        functools.partial(_mla_proj_kernel, scale=scale),
        grid=(B, S // tm),
        in_specs=[row(MLA_Q_LORA), row(256), row(MLA_HEAD_PAD), row(MLA_HEAD_PAD), full(g_q), full(g_kv),
                  full(wq), full(wqs), full(wk), full(wks), full(wv)],
        out_specs=[row(hw), row(hw), row(vw)],
        out_shape=[jax.ShapeDtypeStruct((B, S, hw), BF16),
                   jax.ShapeDtypeStruct((B, S, hw), BF16),
                   jax.ShapeDtypeStruct((B, S, vw), BF16)],
        compiler_params=_cp(("parallel", "parallel"), VMEM_LIMIT),
        name="mla_proj",
    )(u_dq, u_kv, cos, sin, g_q, g_kv, wq, wqs, wk, wks, wv)


def _attn_kernel(q_ref, k_ref, vt_ref, o_ref, *, tk):
    tq = q_ref.shape[0]
    S = k_ref.shape[0]
    nk = S // tk
    def scores(h, j):
        k = k_ref[j * tk:(j + 1) * tk, h * MLA_HEAD_PAD:(h + 1) * MLA_HEAD_PAD]
        return _dot_nt(k, q_ref[:, h * MLA_HEAD_PAD:(h + 1) * MLA_HEAD_PAD])

    hp = q_ref.shape[1] // MLA_HEAD_PAD
    m = [jnp.full((1, tq), NEG_INF, F32)] * hp
    l = [jnp.zeros((1, tq), F32)] * hp
    acc = [jnp.zeros((MLA_V, tq), F32)] * hp
    ahead = min(ATTN_LOOKAHEAD, nk)
    s = [[scores(h, j) for j in range(ahead)] for h in range(hp)]
    for j in range(nk):
        for h in range(hp):
            s_cur = s[h].pop(0)
            if j + ahead < nk:
                s[h].append(scores(h, j + ahead))
            m_new = jnp.maximum(m[h], jnp.max(s_cur, axis=0, keepdims=True))
            alpha = jnp.exp2(m[h] - m_new)
            p = jnp.exp2(s_cur - m_new)
            l[h] = alpha * l[h] + jnp.sum(p, axis=0, keepdims=True)
            vt = vt_ref[h * MLA_V:(h + 1) * MLA_V, j * tk:(j + 1) * tk]
            acc[h] = alpha * acc[h] + _dot(vt, p.astype(BF16))
            m[h] = m_new
    for h in range(hp):
        o_ref[h * MLA_V:(h + 1) * MLA_V, :] = (acc[h] / l[h]).astype(BF16)


def _attn(qh, kh, vt):
    B, S, _ = qh.shape
    tq = min(512, S)
    tk = min(256, S)
    hp = ATTN_HEADS_PER_STEP
    pw = hp * MLA_HEAD_PAD
    return pl.pallas_call(
        functools.partial(_attn_kernel, tk=tk),
        grid=(B, MLA_HEADS // hp, S // tq),
        in_specs=[pl.BlockSpec((None, tq, pw), lambda b, p, i: (b, i, p)),
                  pl.BlockSpec((None, S, pw), lambda b, p, i: (b, 0, p)),
                  pl.BlockSpec((None, hp * MLA_V, S), lambda b, p, i: (b, p, 0))],
        out_specs=pl.BlockSpec((None, hp * MLA_V, tq), lambda b, p, i: (b, p, i)),
        out_shape=jax.ShapeDtypeStruct((B, MLA_HEADS * MLA_V, S), BF16),
        compiler_params=_cp(("parallel", "parallel", "parallel"), VMEM_LIMIT),
        name="mla_attn",
    )(qh, kh, vt)


def _outproj_kernel(yp_ref, ym_ref, ya_ref, x_ref, g1_ref, lg_ref, lb_ref, sc_ref, sh_ref, wo_ref,
                    wr_ref, x1_ref, h2_ref, lt_ref, *, alpha):
    mix = (_dot(yp_ref[...], wo_ref[0:256, :]) + _dot(ym_ref[...], wo_ref[256:512, :])
           + _dot(ya_ref[...], wo_ref[512:1024, :]))
    x1 = _ln(alpha * x_ref[...] + g1_ref[...] * mix) * lg_ref[...] + lb_ref[...]
    x1_ref[...] = x1
    h2 = _ln(x1) * (1.0 + sc_ref[...]) + sh_ref[...]
    half = h2.shape[1] // 2
    h2_ref[...] = _pack_pair(h2[:, :half], h2[:, half:])
    lt_ref[...] = _dot_nt(wr_ref[...], h2.astype(BF16))


def _outproj(y_pool, y_ml, y_mla, x, g1, ln_g, ln_b, sc2, sh2, w_out, w_router_t, alpha):
    B, S, D = x.shape
    tm = min(512, S)
    row = lambda n: pl.BlockSpec((None, tm, n), lambda b, i: (b, i, 0))
    per_b = pl.BlockSpec((None, 1, D), lambda b, i: (b, 0, 0))
    vec = pl.BlockSpec((1, D), lambda b, i: (0, 0))
    return pl.pallas_call(
        functools.partial(_outproj_kernel, alpha=alpha),
        grid=(B, S // tm),
        in_specs=[row(256), row(256), row(512), row(D), per_b, vec, vec, per_b, per_b,
                  pl.BlockSpec((D, D), lambda b, i: (0, 0)),
                  pl.BlockSpec((N_EXPERTS, D), lambda b, i: (0, 0))],
        out_specs=[row(D), row(D // 2), pl.BlockSpec((None, N_EXPERTS, tm), lambda b, i: (b, 0, i))],
        out_shape=[jax.ShapeDtypeStruct((B, S, D), F32), jax.ShapeDtypeStruct((B, S, D // 2), I32),
                   jax.ShapeDtypeStruct((B, N_EXPERTS, S), F32)],
        compiler_params=_cp(("parallel", "parallel"), VMEM_LIMIT),
        name="outproj",
    )(y_pool, y_ml, y_mla, x, g1, ln_g, ln_b, sc2, sh2, w_out, w_router_t)


def _first_max(v, iota, n):
    m = jnp.max(v, axis=0, keepdims=True)
    first = jnp.min(jnp.where(v == m, iota, n), axis=0, keepdims=True)
    return m, first


def _route_kernel(lt_ref, eb_ref, idx_ref, w_ref):
    scores = jax.nn.sigmoid(lt_ref[...])
    sel = scores + eb_ref[...]
    E, tt = sel.shape
    gi = lax.broadcasted_iota(I32, (GROUP_SIZE, tt), 0)
    groups = [sel[g * GROUP_SIZE:(g + 1) * GROUP_SIZE, :] for g in range(N_GROUPS)]
    gs = []
    for grp in groups:
        m1, f1 = _first_max(grp, gi, GROUP_SIZE)
        m2 = jnp.max(jnp.where(gi == f1, NEG_INF, grp), axis=0, keepdims=True)
        gs.append(m1 + m2)
    chosen = [jnp.zeros((1, tt), F32) for _ in range(N_GROUPS)]
    for _ in range(TOPK_GROUPS):
        m = functools.reduce(jnp.maximum, gs)
        first = functools.reduce(
            jnp.minimum, [jnp.where(gs[g] == m, g, N_GROUPS) for g in range(N_GROUPS)])
        for g in range(N_GROUPS):
            hit = first == g
            chosen[g] = jnp.where(hit, 1.0, chosen[g])
            gs[g] = jnp.where(hit, NEG_INF, gs[g])
    cand = jnp.concatenate(
        [jnp.where(chosen[g] > 0.5, groups[g], NEG_INF) for g in range(N_GROUPS)], axis=0)
    ei = lax.broadcasted_iota(I32, (E, tt), 0)
    ws = []
    for k in range(TOP_K):
        _, f = _first_max(cand, ei, E)
        hit = ei == f
        idx_ref[k:k + 1, :] = f
        ws.append(jnp.sum(jnp.where(hit, scores, 0.0), axis=0, keepdims=True))
        cand = jnp.where(hit, NEG_INF, cand)
    norm = ROUTED_SCALE / functools.reduce(jnp.add, ws)
    for k in range(TOP_K):
        w_ref[k:k + 1, :] = ws[k] * norm


def _route(logits_t, e_bias):
    B, E, S = logits_t.shape
    tt = min(512, S)
    return pl.pallas_call(
        _route_kernel,
        grid=(B, S // tt),
        in_specs=[pl.BlockSpec((None, E, tt), lambda b, i: (b, 0, i)),
                  pl.BlockSpec((E, 1), lambda b, i: (0, 0))],
        out_specs=[pl.BlockSpec((None, TOP_K, tt), lambda b, i: (b, 0, i)),
                   pl.BlockSpec((None, TOP_K, tt), lambda b, i: (b, 0, i))],
        out_shape=[jax.ShapeDtypeStruct((B, TOP_K, S), I32),
                   jax.ShapeDtypeStruct((B, TOP_K, S), F32)],
        compiler_params=_cp(("parallel", "parallel"), VMEM_LIMIT),
        name="route",
    )(logits_t, e_bias)


def _rank_kernel(idx_ref, rank_ref, cnt_ref, carry_ref):
    first = jnp.logical_and(pl.program_id(0) == 0, pl.program_id(1) == 0)

    @pl.when(first)
    def _():
        carry_ref[...] = jnp.zeros_like(carry_ref)

    K, tt = idx_ref.shape
    E = carry_ref.shape[0]
    ei = lax.broadcasted_iota(I32, (E, tt), 0)
    ri = lax.broadcasted_iota(I32, (tt, tt), 0)
    ci = lax.broadcasted_iota(I32, (tt, tt), 1)
    before = jnp.where(ri < ci, 1.0, 0.0).astype(BF16)
    base = carry_ref[...]
    for k in range(K):
        hit = ei == idx_ref[k:k + 1, :]
        onehot = jnp.where(hit, 1.0, 0.0)
        prefix = _dot(onehot.astype(BF16), before)
        rank = jnp.sum(jnp.where(hit, base + prefix, 0.0), axis=0, keepdims=True)
        rank_ref[k:k + 1, :] = rank.astype(I32)
        base = base + jnp.sum(onehot, axis=1, keepdims=True)
    carry_ref[...] = base
    cnt_ref[...] = base


def _rank(idx_t):
    B, K, S = idx_t.shape
    tt = min(256, S)
    return pl.pallas_call(
        _rank_kernel,
        grid=(B, S // tt),
        in_specs=[pl.BlockSpec((None, K, tt), lambda b, i: (b, 0, i))],
        out_specs=[pl.BlockSpec((None, K, tt), lambda b, i: (b, 0, i)),
                   pl.BlockSpec((N_EXPERTS, 1), lambda b, i: (0, 0))],
        out_shape=[jax.ShapeDtypeStruct((B, K, S), I32),
                   jax.ShapeDtypeStruct((N_EXPERTS, 1), F32)],
        scratch_shapes=[pltpu.VMEM((N_EXPERTS, 1), F32)],
        compiler_params=_cp(("arbitrary", "arbitrary"), VMEM_LIMIT),
        name="rank",
    )(idx_t)


def _pos_kernel(idx_ref, rank_ref, start_ref, pos_ref):
    K, tt = idx_ref.shape
    ei = lax.broadcasted_iota(I32, (N_EXPERTS, tt), 0)
    start = start_ref[...]
    for k in range(K):
        hit = ei == idx_ref[k:k + 1, :]
        off = jnp.sum(jnp.where(hit, start, 0.0), axis=0, keepdims=True)
        pos_ref[k:k + 1, :] = rank_ref[k:k + 1, :] + off.astype(I32)


def _pos(idx_t, rank_t, pstart):
    B, K, S = idx_t.shape
    tt = min(512, S)
    spec = pl.BlockSpec((None, K, tt), lambda b, i: (b, 0, i))
    return pl.pallas_call(
        _pos_kernel,
        grid=(B, S // tt),
        in_specs=[spec, spec, pl.BlockSpec((N_EXPERTS, 1), lambda b, i: (0, 0))],
        out_specs=spec,
        out_shape=jax.ShapeDtypeStruct((B, K, S), I32),
        compiler_params=_cp(("parallel", "parallel"), VMEM_LIMIT),
        name="slot",
    )(idx_t, rank_t, pstart)


def _dispatch_kernel(pos_ref, h_ref, xs_ref, sem, *, tt):
    def copy(t, k):
        return pltpu.make_async_copy(h_ref.at[pl.ds(t, 1), :],
                                     xs_ref.at[pos_ref[k, t]], sem)

    for t in range(tt):
        for k in range(TOP_K):
            copy(t, k).start(priority=k % 2)
    for t in range(tt):
        for k in range(TOP_K):
            copy(t, k).wait()


def _dispatch(pos_kt, h2_flat, n_rows):
    T, W = h2_flat.shape
    tt = 256
    return pl.pallas_call(
        functools.partial(_dispatch_kernel, tt=tt),
        grid=(T // tt,),
        in_specs=[pl.BlockSpec((TOP_K, tt), lambda i: (0, i), memory_space=pltpu.SMEM),
                  pl.BlockSpec((tt, W), lambda i: (i, 0))],
        out_specs=pl.BlockSpec(memory_space=pl.ANY),
        out_shape=jax.ShapeDtypeStruct((n_rows, 1, W), I32),
        scratch_shapes=[pltpu.SemaphoreType.DMA(())],
        compiler_params=pltpu.CompilerParams(dimension_semantics=("arbitrary",),
                                             has_side_effects=True),
        name="dispatch",
    )(pos_kt, h2_flat)


def _expert_kernel(be_ref, nu_ref, xs_ref, w1_ref, w3_ref, w2_ref, ys_ref, w1b, w3b, w2b):
    b = pl.program_id(0)
    prev = be_ref[jnp.maximum(b - 1, 0)]
    fresh = jnp.logical_or(b == 0, be_ref[b] != prev)

    @pl.when(jnp.logical_and(fresh, b < nu_ref[0]))
    def _():
        w1b[...] = w1_ref[...].astype(BF16)
        w3b[...] = w3_ref[...].astype(BF16)
        w2b[...] = w2_ref[...].astype(BF16)

    @pl.when(b < nu_ref[0])
    def _():
        half = ROW_PIECES * 128
        sub = EXPERT_BLOCK // EXPERT_SUB
        xs, hs = [], []
        for i in range(EXPERT_SUB):
            pieces = [xs_ref[pl.ds(i * sub * ROW_PIECES + c, sub, stride=ROW_PIECES), :]
                      for c in range(ROW_PIECES)]
            lo, hi = _unpack_pair(jnp.concatenate(pieces, axis=1))
            xs.append((lo.astype(BF16), hi.astype(BF16)))
        for lo, hi in xs:
            h1 = _dot(lo, w1b[:half, :]) + _dot(hi, w1b[half:, :])
            h3 = _dot(lo, w3b[:half, :]) + _dot(hi, w3b[half:, :])
            hs.append((h1, h3))
        for i, (h1, h3) in enumerate(hs):
            y = _dot((_silu(h1) * h3).astype(BF16), w2b[...])
            words = _pack_pair(y[:, :half], y[:, half:])
            for c in range(ROW_PIECES):
                ys_ref[pl.ds(i * sub * ROW_PIECES + c, sub, stride=ROW_PIECES), :] = (
                    words[:, c * 128:(c + 1) * 128])


def _experts(blk_expert, n_used, xs, w1, w3, w2, layer):
    P, _, W = xs.shape
    D = 2 * W
    nb = P // EXPERT_BLOCK
    xs = xs.reshape(P * ROW_PIECES, 128)
    last = lambda b, be, nu: jnp.minimum(b, nu[0] - 1)
    wspec = lambda r, c: pl.BlockSpec((None, None, r, c),
                                      lambda b, be, nu: (layer, be[last(b, be, nu)], 0, 0))
    grid_spec = pltpu.PrefetchScalarGridSpec(
        num_scalar_prefetch=2,
        grid=(nb,),
        in_specs=[pl.BlockSpec((EXPERT_BLOCK * ROW_PIECES, 128), lambda b, be, nu: (last(b, be, nu), 0)),
                  wspec(D, D_EXPERT), wspec(D, D_EXPERT), wspec(D_EXPERT, D)],
        out_specs=pl.BlockSpec((EXPERT_BLOCK * ROW_PIECES, 128), lambda b, be, nu: (last(b, be, nu), 0)),
        scratch_shapes=[pltpu.VMEM((D, D_EXPERT), BF16), pltpu.VMEM((D, D_EXPERT), BF16),
                        pltpu.VMEM((D_EXPERT, D), BF16)],
    )
    ys = pl.pallas_call(
        _expert_kernel,
        grid_spec=grid_spec,
        out_shape=jax.ShapeDtypeStruct((P * ROW_PIECES, 128), I32),
        compiler_params=_cp(("arbitrary",), VMEM_LIMIT),
        name="experts",
    )(blk_expert, n_used, xs, w1, w3, w2)
    return ys.reshape(P, 1, W)


def _combine_kernel(pos_ref, nxt_ref, ys_ref, wt_ref, h2_ref, x1_ref, g2_ref, lg_ref, lb_ref, ws1_ref,
                    ws3_ref, ws2_ref, o_ref, buf, sem, *, tt, alpha):
    g = pl.program_id(0)

    def copy(pref, sl, t, k):
        return pltpu.make_async_copy(ys_ref.at[pref[k, t]], buf.at[sl, k, pl.ds(t, 1), :], sem.at[sl])

    def gather(pref, sl):
        for t in range(tt):
            for k in range(TOP_K):
                copy(pref, sl, t, k).start(priority=k % 2)

    @pl.when(g == 0)
    def _():
        gather(pos_ref, 0)

    def step(slot):
        @pl.when(g + 1 < pl.num_programs(0))
        def _():
            gather(nxt_ref, 1 - slot)

        half = h2_ref.shape[1]
        lo, hi = _unpack_pair(h2_ref[...])
        lo, hi = lo.astype(BF16), hi.astype(BF16)
        h1 = _dot(lo, ws1_ref[:half, :]) + _dot(hi, ws1_ref[half:, :])
        h3 = _dot(lo, ws3_ref[:half, :]) + _dot(hi, ws3_ref[half:, :])
        shared = _dot((_silu(h1) * h3).astype(BF16), ws2_ref[...])
        for t in range(tt):
            for k in range(TOP_K):
                copy(pos_ref, slot, t, k).wait()
        wt = wt_ref[...]
        f_lo, f_hi = shared[:, :half], shared[:, half:]
        for k in range(TOP_K):
            y_lo, y_hi = _unpack_pair(buf[slot, k])
            f_lo = f_lo + wt[:, k:k + 1] * y_lo
            f_hi = f_hi + wt[:, k:k + 1] * y_hi
        ffn = jnp.concatenate([f_lo, f_hi], axis=1)
        o_ref[...] = _ln(alpha * x1_ref[...] + g2_ref[...] * ffn) * lg_ref[...] + lb_ref[...]

    for parity in range(2):
        pl.when(g % 2 == parity)(functools.partial(step, parity))


def _combine(pos_kt, ys, w_tk, h2, x1, g2, ln_g, ln_b, ws1, ws3, ws2, alpha):
    B, S, D = x1.shape
    tt = min(256, S)
    nt = S // tt
    n = B * nt
    row = lambda w: pl.BlockSpec((None, tt, w), lambda g: (g // nt, g % nt, 0))
    full = lambda a: pl.BlockSpec(a.shape, lambda g: (0,) * a.ndim)
    return pl.pallas_call(
        functools.partial(_combine_kernel, tt=tt, alpha=alpha),
        grid=(n,),
        in_specs=[pl.BlockSpec((TOP_K, tt), lambda g: (0, g), memory_space=pltpu.SMEM),
                  pl.BlockSpec((TOP_K, tt), lambda g: (0, jnp.minimum(g + 1, n - 1)),
                               memory_space=pltpu.SMEM),
                  pl.BlockSpec(memory_space=pl.ANY),
                  row(TOP_K), row(D // 2), row(D),
                  pl.BlockSpec((None, 1, D), lambda g: (g // nt, 0, 0)),
                  full(ln_g), full(ln_b), full(ws1), full(ws3), full(ws2)],
        out_specs=row(D),
        out_shape=jax.ShapeDtypeStruct((B, S, D), F32),
        scratch_shapes=[pltpu.VMEM((2, TOP_K, tt, D // 2), I32), pltpu.SemaphoreType.DMA((2,))],
        compiler_params=_cp(("arbitrary",), VMEM_LIMIT),
        name="combine",
    )(pos_kt, pos_kt, ys, w_tk, h2, x1, g2, ln_g, ln_b, ws1, ws3, ws2)


def _arrange_w_in(w):
    z = lambda n: jnp.zeros((w.shape[0], n), w.dtype)
    return jnp.concatenate([w[:, 0:1280], w[:, 1296:1552], w[:, 1552:1680], w[:, 1680:1712],
                            z(96), w[:, 1280:1296], z(112)], axis=1).astype(BF16)


def _head_layout(nope, rope_a, rope_b):
    r = nope.shape[0]
    z = jnp.zeros((r, MLA_HEADS, MLA_HEAD_PAD - MLA_NOPE - MLA_ROPE), nope.dtype)
    return jnp.concatenate([nope, rope_a, rope_b, z], axis=2).reshape(r, MLA_HEADS * MLA_HEAD_PAD)


def _arrange_mla(w_uq, w_uk):
    half = MLA_ROPE // 2
    q = w_uq.reshape(MLA_Q_LORA, MLA_HEADS, MLA_NOPE + MLA_ROPE)
    qn, q1, q2 = q[..., :MLA_NOPE], q[..., MLA_NOPE:MLA_NOPE + half], q[..., MLA_NOPE + half:]
    wq = _head_layout(qn, q1, q2)
    wq_sw = _head_layout(jnp.zeros_like(qn), q2, q1)
    kn = w_uk.reshape(MLA_KV_LORA, MLA_HEADS, MLA_NOPE)
    zk = jnp.zeros((MLA_KV_LORA, MLA_HEADS, half), w_uk.dtype)
    eye = jnp.eye(MLA_ROPE, dtype=w_uk.dtype)
    e1 = jnp.broadcast_to(eye[:, None, :half], (MLA_ROPE, MLA_HEADS, half))
    e2 = jnp.broadcast_to(eye[:, None, half:], (MLA_ROPE, MLA_HEADS, half))
    zn = jnp.zeros((MLA_ROPE, MLA_HEADS, MLA_NOPE), w_uk.dtype)
    pad = jnp.zeros((128 - MLA_ROPE, MLA_HEADS * MLA_HEAD_PAD), w_uk.dtype)
    wk = jnp.concatenate([_head_layout(kn, zk, zk), _head_layout(zn, e1, e2), pad], axis=0)
    wk_sw = jnp.concatenate([_head_layout(jnp.zeros_like(kn), zk, zk), _head_layout(zn, e2, e1), pad],
                            axis=0)
    return wq.astype(BF16), wq_sw.astype(BF16), wk.astype(BF16), wk_sw.astype(BF16)


def _rope_rows():
    half = MLA_ROPE // 2
    inv_freq = ROPE_THETA ** (-jnp.arange(0, MLA_ROPE, 2, dtype=F32) / MLA_ROPE)
    z = lambda n: jnp.zeros((n,), F32)
    frq = jnp.concatenate([z(MLA_NOPE), inv_freq, inv_freq, z(MLA_HEAD_PAD - MLA_NOPE - MLA_ROPE)])
    sgn = jnp.concatenate([z(MLA_NOPE), -jnp.ones((half,), F32), jnp.ones((half,), F32),
                           z(MLA_HEAD_PAD - MLA_NOPE - MLA_ROPE)])
    return frq[None, :], sgn[None, :]


def _block_diag_pool(w_pool):
    G, C, _ = w_pool.shape
    out = jnp.zeros((G * C, G * C), w_pool.dtype)
    for g in range(G):
        out = out.at[g * C:(g + 1) * C, g * C:(g + 1) * C].set(w_pool[g])
    return out.astype(BF16)


def kernel(x, c, positions, w_ada, b_ada, w_in, w_pool, s_pool, conv_w, conv_b, gate_b, gn_w, g_q,
           g_kv, w_uq, w_uk, w_uv, w_out, ln1_g, ln1_b, w_router, e_bias, w1, w3, w2, ws1, ws3, ws2,
           ln2_g, ln2_b):
    B, S, D = x.shape
    depth = w_in.shape[0]
    T = B * S
    H = ML_HEADS
    alpha = float((2 * depth) ** 0.25)
    n_assign = T * TOP_K
    n_blocks = n_assign // EXPERT_BLOCK + N_EXPERTS
    n_rows = n_blocks * EXPERT_BLOCK

    ada = _ada(c, w_ada, b_ada)
    rope_cos, rope_sin = _rope_tables(positions.reshape(B, S, 1), *_rope_rows())
    qk_scale = jnp.concatenate([jnp.ones((1, ML_WIDTH), F32),
                                jnp.full((1, ML_WIDTH), ML_DH ** -0.5, F32)], axis=1)
    ones_row = lambda n: jnp.ones((B, n, 1, S), BF16)
    zero_rows = lambda n, r: jnp.zeros((B, n, r, S), BF16)

    for l in range(depth):
        sh1, sc1, g1, sh2, sc2, g2 = [a.reshape(B, 1, D) for a in jnp.split(ada[l], 6, axis=-1)]
        u_pool, u_qk, u_v, u_o, u_dq, u_kv, u_gate = _inproj(x, sc1, sh1, _arrange_w_in(w_in[l]))
        y_pool = _pool(u_pool, _block_diag_pool(w_pool[l]), s_pool[l][None, :])

        qk_act = _conv(u_qk, conv_w[l], conv_b[l][None, :], qk_scale)
        heads_t = lambda a: a.reshape(B, S, H, ML_DH).transpose(0, 2, 3, 1)
        gates = u_gate[:, :, :4 * H].reshape(B, S, 4, H)
        vt_ml = jnp.concatenate([heads_t(u_v), ones_row(H), zero_rows(H, ML_VT_ROWS - ML_DH - 1)],
                                axis=2)
        y_ml = _mlstm(qk_act[:, :, ML_WIDTH:].reshape(B, S, H, ML_DH).transpose(0, 2, 1, 3),
                      heads_t(qk_act[:, :, :ML_WIDTH]), vt_ml, heads_t(u_o),
                      gates.transpose(0, 3, 1, 2),
                      gates.transpose(0, 3, 2, 1),
                      gate_b[l].reshape(4, H).T.reshape(H, 1, 4),
                      gate_b[l].reshape(4, H).T.reshape(H, 4, 1),
                      gn_w[l].reshape(H, ML_DH, 1))
        y_ml = y_ml.transpose(0, 3, 1, 2).reshape(B, S, ML_WIDTH)

        wq, wq_sw, wk, wk_sw = _arrange_mla(w_uq[l], w_uk[l])
        qh, kh, vh = _mla_proj(u_dq, u_kv, rope_cos, rope_sin, g_q[l][None, :], g_kv[l][None, :],
                               wq, wq_sw, wk, wk_sw, w_uv[l].astype(BF16))
        y_mla = _attn(qh, kh, vh.transpose(0, 2, 1)).transpose(0, 2, 1)

        x1, h2, logits_t = _outproj(y_pool, y_ml, y_mla, x, g1, ln1_g[l][None, :], ln1_b[l][None, :],
                                    sc2, sh2, w_out[l].astype(BF16), w_router[l].T.astype(BF16), alpha)
        idx_t, w_t = _route(logits_t, e_bias[l][:, None])
        rank_t, counts = _rank(idx_t)
        counts = counts[:, 0].astype(I32)
        padded = (counts + EXPERT_BLOCK - 1) // EXPERT_BLOCK * EXPERT_BLOCK
        pend = jnp.cumsum(padded)
        pstart = pend - padded
        n_used = (pend[-1:] // EXPERT_BLOCK).astype(I32)
        blk_start = jnp.arange(n_blocks, dtype=I32) * EXPERT_BLOCK
        blk_expert = jnp.minimum(jnp.sum((pend[None, :] <= blk_start[:, None]).astype(I32), axis=1),
                                 N_EXPERTS - 1)
        pos_kt = _pos(idx_t, rank_t, pstart.astype(F32)[:, None]).transpose(1, 0, 2).reshape(TOP_K, T)
        xs = _dispatch(pos_kt, h2.reshape(T, D // 2), n_rows)
        ys = _experts(blk_expert, n_used, xs, w1, w3, w2, l)
        x = _combine(pos_kt, ys, w_t.transpose(0, 2, 1), h2, x1, g2, ln2_g[l][None, :],
                     ln2_b[l][None, :], ws1[l].astype(BF16), ws3[l].astype(BF16),
                     ws2[l].astype(BF16), alpha)
    return x
```

```python
import functools

import jax
import jax.numpy as jnp
import numpy as np
from jax import lax
from jax.experimental import pallas as pl
from jax.experimental.pallas import tpu as pltpu

F32 = jnp.float32
BF16 = jnp.bfloat16
I32 = jnp.int32

D_MODEL = 1024
POOL_WINDOWS = (2, 4, 8, 16)
POOL_GROUP_DIM = 64
POOL_WIDTH = 256
ML_HEADS = 4
ML_DH = 64
ML_WIDTH = 256
ML_CONV = 5
ML_VT_ROWS = 80
ML_HEADS_PER_STEP = 2
MLA_HEADS = 8
MLA_NOPE = 64
MLA_ROPE = 32
MLA_V = 64
MLA_Q_LORA = 256
MLA_KV_LORA = 128
MLA_HEAD_PAD = 128
ATTN_HEADS_PER_STEP = 2
ROPE_THETA = 10000.0
N_EXPERTS = 256
TOP_K = 8
N_GROUPS = 8
TOPK_GROUPS = 4
GROUP_SIZE = N_EXPERTS // N_GROUPS
D_EXPERT = 256
ROUTED_SCALE = 2.5
LN_EPS = 1e-5
RMS_EPS = 1e-6
NEG_BIG = -1e30
NEG_INF = float("-inf")

EXPERT_BLOCK = 512
EXPERT_SUB = 2
ROW_PIECES = D_MODEL // 2 // 128
VMEM_LIMIT = 56 * 1024 * 1024


def _cp(sem, vmem=None):
    return pltpu.CompilerParams(dimension_semantics=sem, vmem_limit_bytes=vmem)


def _ln(x):
    mu = jnp.mean(x, axis=-1, keepdims=True)
    xc = x - mu
    var = jnp.mean(xc * xc, axis=-1, keepdims=True)
    return xc * lax.rsqrt(var + LN_EPS)


def _silu(x):
    return x * jax.nn.sigmoid(x)


def _rms(x, g):
    return x * lax.rsqrt(jnp.mean(x * x, axis=-1, keepdims=True) + RMS_EPS) * g


def _dot(a, b):
    return jnp.dot(a, b, preferred_element_type=F32)


def _dot_nt(a, b):
    return lax.dot_general(a, b, (((1,), (1,)), ((), ())), preferred_element_type=F32)


def _dot_tn(a, b):
    return lax.dot_general(a, b, (((0,), (0,)), ((), ())), preferred_element_type=F32)


def _pack_pair(lo, hi):
    lo_b = lax.bitcast_convert_type(lo.astype(BF16).astype(F32), I32)
    hi_b = lax.bitcast_convert_type(hi.astype(BF16).astype(F32), I32)
    return jnp.bitwise_or(hi_b, lax.shift_right_logical(lo_b, 16))


def _unpack_pair(w):
    lo = lax.bitcast_convert_type(lax.shift_left(w, 16), F32)
    hi = lax.bitcast_convert_type(jnp.bitwise_and(w, -65536), F32)
    return lo, hi


def _ada_kernel(c_ref, w_ref, b_ref, o_ref):
    ca = _silu(c_ref[...]).astype(BF16)
    o_ref[...] = _dot(ca, w_ref[...].astype(BF16)) + b_ref[...]


def _ada(c, w_ada, b_ada):
    L, D, N = w_ada.shape
    B = c.shape[0]
    tn = 1536
    return pl.pallas_call(
        _ada_kernel,
        grid=(L, N // tn),
        in_specs=[pl.BlockSpec((B, D), lambda l, j: (0, 0)),
                  pl.BlockSpec((None, D, tn), lambda l, j: (l, 0, j)),
                  pl.BlockSpec((None, 1, tn), lambda l, j: (l, 0, j))],
        out_specs=pl.BlockSpec((None, B, tn), lambda l, j: (l, 0, j)),
        out_shape=jax.ShapeDtypeStruct((L, B, N), F32),
        compiler_params=_cp(("parallel", "parallel"), VMEM_LIMIT),
        name="ada",
    )(c, w_ada, b_ada.reshape(L, 1, N))


def _inproj_kernel(x_ref, sc_ref, sh_ref, w_ref, pool_ref, qk_ref, v_ref, o_ref, dq_ref,
                   kv_ref, gate_ref):
    h = (_ln(x_ref[...]) * (1.0 + sc_ref[...]) + sh_ref[...]).astype(BF16)
    pool_ref[...] = _dot(h, w_ref[:, 0:256])
    qk_ref[...] = _dot(h, w_ref[:, 256:768])
    v_ref[...] = _dot(h, w_ref[:, 768:1024]).astype(BF16)
    o_ref[...] = _dot(h, w_ref[:, 1024:1280])
    dq_ref[...] = _dot(h, w_ref[:, 1280:1536])
    kv_ref[...] = _dot(h, w_ref[:, 1536:1792])
    gate_ref[...] = _dot(h, w_ref[:, 1792:1920])


def _inproj(x, sc, sh, w):
    B, S, D = x.shape
    tm = min(512, S)
    widths = (256, 512, 256, 256, 256, 256, 128)
    dtypes = (F32, F32, BF16, F32, F32, F32, F32)
    row = lambda n: pl.BlockSpec((None, tm, n), lambda b, i: (b, i, 0))
    return pl.pallas_call(
        _inproj_kernel,
        grid=(B, S // tm),
        in_specs=[row(D),
                  pl.BlockSpec((None, 1, D), lambda b, i: (b, 0, 0)),
                  pl.BlockSpec((None, 1, D), lambda b, i: (b, 0, 0)),
                  pl.BlockSpec(w.shape, lambda b, i: (0, 0))],
        out_specs=[row(n) for n in widths],
        out_shape=[jax.ShapeDtypeStruct((B, S, n), dt) for n, dt in zip(widths, dtypes)],
        compiler_params=_cp(("parallel", "parallel"), VMEM_LIMIT),
        name="inproj",
    )(x, sc, sh, w)


def _shift_dn(a, k, row):
    return jnp.where(row >= k, pltpu.roll(a, k, 0), 0.0)


def _shift_up(a, k, row):
    n = a.shape[0]
    return jnp.where(row < n - k, pltpu.roll(a, n - k, 0), 0.0)


def _pool_kernel(u_ref, w_ref, s_ref, y_ref):
    x = u_ref[...]
    S, C = x.shape
    row = lax.broadcasted_iota(I32, (S, C), 0)
    lane = lax.broadcasted_iota(I32, (S, C), 1)
    rowf = row.astype(F32)
    p1, f1 = x, x
    p2 = p1 + _shift_dn(p1, 1, row)
    f2 = f1 + _shift_up(f1, 1, row)
    p4 = p2 + _shift_dn(p2, 2, row)
    f4 = f2 + _shift_up(f2, 2, row)
    p8 = p4 + _shift_dn(p4, 4, row)
    f8 = f4 + _shift_up(f4, 4, row)
    wins = []
    for half, p, f in ((1, p1, f1), (2, p2, f2), (4, p4, f4), (8, p8, f8)):
        total = _shift_dn(p, 1, row) + f
        cnt = jnp.minimum(rowf + half, float(S)) - jnp.maximum(rowf - half, 0.0)
        wins.append(total / cnt)
    pooled = jnp.where(lane < 64, wins[0],
                       jnp.where(lane < 128, wins[1], jnp.where(lane < 192, wins[2], wins[3])))
    d = (pooled - x).astype(BF16)
    y_ref[...] = (_dot(d, w_ref[...]) * s_ref[...]).astype(BF16)


def _pool(u_pool, w_bd, s_pool):
    B, S, C = u_pool.shape
    return pl.pallas_call(
        _pool_kernel,
        grid=(B,),
        in_specs=[pl.BlockSpec((None, S, C), lambda b: (b, 0, 0)),
                  pl.BlockSpec((C, C), lambda b: (0, 0)),
                  pl.BlockSpec((1, C), lambda b: (0, 0))],
        out_specs=pl.BlockSpec((None, S, C), lambda b: (b, 0, 0)),
        out_shape=jax.ShapeDtypeStruct((B, S, C), BF16),
        compiler_params=_cp(("parallel",), VMEM_LIMIT),
        name="pool",
    )(u_pool, w_bd, s_pool)


def _conv_kernel(u_ref, w_ref, b_ref, scale_ref, y_ref):
    x = u_ref[...]
    S, C = x.shape
    row = lax.broadcasted_iota(I32, (S, C), 0)
    y = (_shift_dn(x, 2, row) * w_ref[0:1, :] + _shift_dn(x, 1, row) * w_ref[1:2, :]
         + x * w_ref[2:3, :] + _shift_up(x, 1, row) * w_ref[3:4, :]
         + _shift_up(x, 2, row) * w_ref[4:5, :] + b_ref[...])
    y_ref[...] = (_silu(y) * scale_ref[...]).astype(BF16)


def _conv(u_qk, conv_w, conv_b, scale):
    B, S, C = u_qk.shape
    return pl.pallas_call(
        _conv_kernel,
        grid=(B,),
        in_specs=[pl.BlockSpec((None, S, C), lambda b: (b, 0, 0)),
                  pl.BlockSpec((ML_CONV, C), lambda b: (0, 0)),
                  pl.BlockSpec((1, C), lambda b: (0, 0)),
                  pl.BlockSpec((1, C), lambda b: (0, 0))],
        out_specs=pl.BlockSpec((None, S, C), lambda b: (b, 0, 0)),
        out_shape=jax.ShapeDtypeStruct((B, S, C), BF16),
        compiler_params=_cp(("parallel",), VMEM_LIMIT),
        name="mlstm_conv",
    )(u_qk, conv_w, conv_b, scale)


def _log_sigmoid(x):
    return jnp.minimum(x, 0.0) - jnp.log(1.0 + jnp.exp(-jnp.abs(x)))


def _split3(x):
    a = x.astype(BF16)
    r = x - a.astype(F32)
    b = r.astype(BF16)
    c = (r - b.astype(F32)).astype(BF16)
    return a, b, c


def _mlstm_chunk(s, k, qt, vt, c_col, b_row, i_row, b_last, allowed, ct, m_prev):
    logw = jnp.where(allowed, c_col + b_row, NEG_INF)
    m_inter = b_row + m_prev
    m_t = jnp.maximum(m_inter, jnp.max(logw, axis=0, keepdims=True))
    w_intra = jnp.exp(logw - m_t)
    w_inter = jnp.exp(m_inter - m_t)
    qk = (s * w_intra).astype(BF16)
    nd = _dot(vt, qk) + w_inter * _dot(ct.astype(BF16), qt)
    den = jnp.maximum(jnp.abs(nd[ML_DH:ML_DH + 1, :]), jnp.exp(-m_t))
    h = nd[:ML_DH, :] / den
    g = b_last - b_row + i_row
    m_new = jnp.maximum(b_last + m_prev, jnp.max(g, axis=1, keepdims=True))
    w_k = jnp.exp(g - m_new)
    decay = jnp.exp(b_last + m_prev - m_new)
    ct_new = decay * ct + _dot((vt.astype(F32) * w_k).astype(BF16), k)
    return h, ct_new, m_new


def _mlstm_kernel(k_ref, qt_ref, vt_ref, uo_ref, gc_ref, gr_ref, bc_ref, br_ref, gn_ref, y_ref,
                  hf_ref, hb_ref, *, chunk):
    nh, S = k_ref.shape[0], k_ref.shape[1]
    L = chunk
    nc = S // L
    ri = lax.broadcasted_iota(I32, (L, L), 0)
    ci = lax.broadcasted_iota(I32, (L, L), 1)
    lower = ci <= ri
    upper = ci >= ri
    tril = jnp.where(lower, 1.0, 0.0).astype(BF16)
    triu = jnp.where(upper, 1.0, 0.0).astype(BF16)

    def prep(hh, r0, forward):
        gc = gc_ref[hh, pl.ds(r0, L), :] + bc_ref[hh]
        gr = gr_ref[hh, :, pl.ds(r0, L)] + br_ref[hh]
        c1, c2, c3 = _split3(_log_sigmoid(gc))
        r1, r2, r3 = _split3(_log_sigmoid(gr))
        k = k_ref[hh, pl.ds(r0, L), :]
        qt = qt_ref[hh, :, pl.ds(r0, L)]
        s = _dot(k, qt)
        if forward:
            cum_c = (_dot(tril, c1) + _dot(tril, c2) + _dot(tril, c3))[:, 1:2]
            cum_r = (_dot(r1, triu) + _dot(r2, triu) + _dot(r3, triu))[1:2, :]
            return s, k, qt, gc[:, 0:1] - cum_c, cum_r, gr[0:1, :], cum_r[:, L - 1:L]
        cum_c = (_dot(triu, c1) + _dot(triu, c2) + _dot(triu, c3))[:, 3:4]
        cum_r = (_dot(r1, tril) + _dot(r2, tril) + _dot(r3, tril))[3:4, :]
        return s, k, qt, gc[:, 2:3] - cum_c, cum_r, gr[2:3, :], cum_r[:, 0:1]

    def step(i, carry):
        rf = pl.multiple_of(i * L, L)
        rb = pl.multiple_of((nc - 1 - i) * L, L)
        preps = [(prep(hh, rf, True), prep(hh, rb, False)) for hh in range(nh)]
        out = []
        for hh, (pf, pb) in enumerate(preps):
            cf, mf, cb, mb = carry[4 * hh:4 * hh + 4]
            h, cf, mf = _mlstm_chunk(*pf[:3], vt_ref[hh, :, pl.ds(rf, L)], *pf[3:], upper, cf, mf)
            hf_ref[hh, :, pl.ds(rf, L)] = h
            h, cb, mb = _mlstm_chunk(*pb[:3], vt_ref[hh, :, pl.ds(rb, L)], *pb[3:], lower, cb, mb)
            hb_ref[hh, :, pl.ds(rb, L)] = h
            out += [cf, mf, cb, mb]
        return tuple(out)

    c0 = jnp.zeros((ML_VT_ROWS, ML_DH), F32)
    m0 = jnp.full((1, 1), NEG_BIG, F32)
    lax.fori_loop(0, nc, step, (c0, m0, c0, m0) * nh)
    for hh in range(nh):
        h = hf_ref[hh] + hb_ref[hh]
        mu = jnp.mean(h, axis=0, keepdims=True)
        hc = h - mu
        var = jnp.mean(hc * hc, axis=0, keepdims=True)
        hn = hc * lax.rsqrt(var + LN_EPS)
        y_ref[hh] = (jax.nn.sigmoid(uo_ref[hh]) * (hn * gn_ref[hh])).astype(BF16)


def _mlstm(k_heads, qt_heads, vt_ext, uot_heads, g_col, g_row, b_col, b_row, gn_w):
    B, H, S, dh = k_heads.shape
    L = min(256, S)
    nh = ML_HEADS_PER_STEP
    seq = lambda r, c: pl.BlockSpec((None, nh, r, c), lambda b, h: (b, h, 0, 0))
    par = lambda r, c: pl.BlockSpec((nh, r, c), lambda b, h: (h, 0, 0))
    return pl.pallas_call(
        functools.partial(_mlstm_kernel, chunk=L),
        grid=(B, H // nh),
        in_specs=[seq(S, dh), seq(dh, S), seq(ML_VT_ROWS, S), seq(dh, S), seq(S, 4), seq(4, S),
                  par(1, 4), par(4, 1), par(dh, 1)],
        out_specs=seq(dh, S),
        out_shape=jax.ShapeDtypeStruct((B, H, dh, S), BF16),
        scratch_shapes=[pltpu.VMEM((nh, dh, S), F32), pltpu.VMEM((nh, dh, S), F32)],
        compiler_params=_cp(("parallel", "parallel"), VMEM_LIMIT),
        name="mlstm_scan",
    )(k_heads, qt_heads, vt_ext, uot_heads, g_col, g_row, b_col, b_row, gn_w)


def _rope_kernel(pos_ref, frq_ref, sgn_ref, cos_ref, sin_ref):
    ang = pos_ref[...].astype(F32) * frq_ref[...]
    cos_ref[...] = jnp.cos(ang)
    sin_ref[...] = jnp.sin(ang) * sgn_ref[...]


def _rope_tables(pos, frq, sgn):
    B, S, _ = pos.shape
    tm = min(512, S)
    row = lambda n: pl.BlockSpec((None, tm, n), lambda b, i: (b, i, 0))
    full = lambda a: pl.BlockSpec(a.shape, lambda b, i: (0,) * a.ndim)
    return pl.pallas_call(
        _rope_kernel,
        grid=(B, S // tm),
        in_specs=[row(1), full(frq), full(sgn)],
        out_specs=[row(MLA_HEAD_PAD), row(MLA_HEAD_PAD)],
        out_shape=[jax.ShapeDtypeStruct((B, S, MLA_HEAD_PAD), F32)] * 2,
        compiler_params=_cp(("parallel", "parallel"), VMEM_LIMIT),
        name="rope_tables",
    )(pos, frq, sgn)


def _mla_proj_kernel(dq_ref, kv_ref, cos_ref, sin_ref, gq_ref, gkv_ref, wq_ref, wqs_ref,
                     wk_ref, wks_ref, wv_ref, q_ref, k_ref, v_ref, *, scale):
    cos = cos_ref[...]
    sin = sin_ref[...]
    qn = _rms(dq_ref[...], gq_ref[...]).astype(BF16)
    a = _dot(qn, wq_ref[...])
    a_sw = _dot(qn, wqs_ref[...])
    kv = kv_ref[...]
    ckv = _rms(kv[:, :MLA_KV_LORA], gkv_ref[...])
    kin = jnp.concatenate([ckv, kv[:, MLA_KV_LORA:]], axis=1).astype(BF16)
    ak = _dot(kin, wk_ref[...])
    ak_sw = _dot(kin, wks_ref[...])
    for h in range(MLA_HEADS):
        sl = slice(h * MLA_HEAD_PAD, (h + 1) * MLA_HEAD_PAD)
        q_ref[:, sl] = ((a[:, sl] * cos + a_sw[:, sl] * sin) * scale).astype(BF16)
        k_ref[:, sl] = (ak[:, sl] * cos + ak_sw[:, sl] * sin).astype(BF16)
    v_ref[...] = _dot(ckv.astype(BF16), wv_ref[...]).astype(BF16)


def _mla_proj(u_dq, u_kv, cos, sin, g_q, g_kv, wq, wqs, wk, wks, wv):
    B, S, _ = u_dq.shape
    tm = min(512, S)
    hw = MLA_HEADS * MLA_HEAD_PAD
    vw = MLA_HEADS * MLA_V
    row = lambda n: pl.BlockSpec((None, tm, n), lambda b, i: (b, i, 0))
    full = lambda a: pl.BlockSpec(a.shape, lambda b, i: (0,) * a.ndim)
    scale = (MLA_NOPE + MLA_ROPE) ** -0.5 * float(np.log2(np.e))
    return pl.pallas_call(
        functools.partial(_mla_proj_kernel, scale=scale),
        grid=(B, S // tm),
        in_specs=[row(MLA_Q_LORA), row(256), row(MLA_HEAD_PAD), row(MLA_HEAD_PAD), full(g_q), full(g_kv),
                  full(wq), full(wqs), full(wk), full(wks), full(wv)],
        out_specs=[row(hw), row(hw), row(vw)],
        out_shape=[jax.ShapeDtypeStruct((B, S, hw), BF16),
                   jax.ShapeDtypeStruct((B, S, hw), BF16),
                   jax.ShapeDtypeStruct((B, S, vw), BF16)],
        compiler_params=_cp(("parallel", "parallel"), VMEM_LIMIT),
        name="mla_proj",
    )(u_dq, u_kv, cos, sin, g_q, g_kv, wq, wqs, wk, wks, wv)


def _attn_kernel(q_ref, k_ref, vt_ref, o_ref, *, tk):
    tq = q_ref.shape[0]
    S = k_ref.shape[0]
    nk = S // tk
    def scores(h, j):
        k = k_ref[j * tk:(j + 1) * tk, h * MLA_HEAD_PAD:(h + 1) * MLA_HEAD_PAD]
        return _dot_nt(k, q_ref[:, h * MLA_HEAD_PAD:(h + 1) * MLA_HEAD_PAD])

    hp = q_ref.shape[1] // MLA_HEAD_PAD
    m = [jnp.full((1, tq), NEG_INF, F32)] * hp
    l = [jnp.zeros((1, tq), F32)] * hp
    acc = [jnp.zeros((MLA_V, tq), F32)] * hp
    s = [scores(h, 0) for h in range(hp)]
    for j in range(nk):
        for h in range(hp):
            s_cur = s[h]
            if j + 1 < nk:
                s[h] = scores(h, j + 1)
            m_new = jnp.maximum(m[h], jnp.max(s_cur, axis=0, keepdims=True))
            alpha = jnp.exp2(m[h] - m_new)
            p = jnp.exp2(s_cur - m_new)
            l[h] = alpha * l[h] + jnp.sum(p, axis=0, keepdims=True)
            vt = vt_ref[h * MLA_V:(h + 1) * MLA_V, j * tk:(j + 1) * tk]
            acc[h] = alpha * acc[h] + _dot(vt, p.astype(BF16))
            m[h] = m_new
    for h in range(hp):
        o_ref[h * MLA_V:(h + 1) * MLA_V, :] = (acc[h] / l[h]).astype(BF16)


def _attn(qh, kh, vt):
    B, S, _ = qh.shape
    tq = min(512, S)
    tk = min(256, S)
    hp = ATTN_HEADS_PER_STEP
    pw = hp * MLA_HEAD_PAD
    return pl.pallas_call(
        functools.partial(_attn_kernel, tk=tk),
        grid=(B, MLA_HEADS // hp, S // tq),
        in_specs=[pl.BlockSpec((None, tq, pw), lambda b, p, i: (b, i, p)),
                  pl.BlockSpec((None, S, pw), lambda b, p, i: (b, 0, p)),
                  pl.BlockSpec((None, hp * MLA_V, S), lambda b, p, i: (b, p, 0))],
        out_specs=pl.BlockSpec((None, hp * MLA_V, tq), lambda b, p, i: (b, p, i)),
        out_shape=jax.ShapeDtypeStruct((B, MLA_HEADS * MLA_V, S), BF16),
        compiler_params=_cp(("parallel", "parallel", "parallel"), VMEM_LIMIT),
        name="mla_attn",
    )(qh, kh, vt)


def _outproj_kernel(yp_ref, ym_ref, ya_ref, x_ref, g1_ref, lg_ref, lb_ref, sc_ref, sh_ref, wo_ref,
                    wr_ref, x1_ref, h2_ref, lt_ref, *, alpha):
    mix = (_dot(yp_ref[...], wo_ref[0:256, :]) + _dot(ym_ref[...], wo_ref[256:512, :])
           + _dot(ya_ref[...], wo_ref[512:1024, :]))
    x1 = _ln(alpha * x_ref[...] + g1_ref[...] * mix) * lg_ref[...] + lb_ref[...]
    x1_ref[...] = x1
    h2 = _ln(x1) * (1.0 + sc_ref[...]) + sh_ref[...]
    half = h2.shape[1] // 2
    h2_ref[...] = _pack_pair(h2[:, :half], h2[:, half:])
    lt_ref[...] = _dot_nt(wr_ref[...], h2.astype(BF16))


def _outproj(y_pool, y_ml, y_mla, x, g1, ln_g, ln_b, sc2, sh2, w_out, w_router_t, alpha):
    B, S, D = x.shape
    tm = min(512, S)
    row = lambda n: pl.BlockSpec((None, tm, n), lambda b, i: (b, i, 0))
    per_b = pl.BlockSpec((None, 1, D), lambda b, i: (b, 0, 0))
    vec = pl.BlockSpec((1, D), lambda b, i: (0, 0))
    return pl.pallas_call(
        functools.partial(_outproj_kernel, alpha=alpha),
        grid=(B, S // tm),
        in_specs=[row(256), row(256), row(512), row(D), per_b, vec, vec, per_b, per_b,
                  pl.BlockSpec((D, D), lambda b, i: (0, 0)),
                  pl.BlockSpec((N_EXPERTS, D), lambda b, i: (0, 0))],
        out_specs=[row(D), row(D // 2), pl.BlockSpec((None, N_EXPERTS, tm), lambda b, i: (b, 0, i))],
        out_shape=[jax.ShapeDtypeStruct((B, S, D), F32), jax.ShapeDtypeStruct((B, S, D // 2), I32),
                   jax.ShapeDtypeStruct((B, N_EXPERTS, S), F32)],
        compiler_params=_cp(("parallel", "parallel"), VMEM_LIMIT),
        name="outproj",
    )(y_pool, y_ml, y_mla, x, g1, ln_g, ln_b, sc2, sh2, w_out, w_router_t)


def _first_max(v, iota, n):
    m = jnp.max(v, axis=0, keepdims=True)
    first = jnp.min(jnp.where(v == m, iota, n), axis=0, keepdims=True)
    return m, first


def _route_kernel(lt_ref, eb_ref, idx_ref, w_ref):
    scores = jax.nn.sigmoid(lt_ref[...])
    sel = scores + eb_ref[...]
    E, tt = sel.shape
    gi = lax.broadcasted_iota(I32, (GROUP_SIZE, tt), 0)
    groups = [sel[g * GROUP_SIZE:(g + 1) * GROUP_SIZE, :] for g in range(N_GROUPS)]
    gs = []
    for grp in groups:
        m1, f1 = _first_max(grp, gi, GROUP_SIZE)
        m2 = jnp.max(jnp.where(gi == f1, NEG_INF, grp), axis=0, keepdims=True)
        gs.append(m1 + m2)
    chosen = [jnp.zeros((1, tt), F32) for _ in range(N_GROUPS)]
    for _ in range(TOPK_GROUPS):
        m = functools.reduce(jnp.maximum, gs)
        first = functools.reduce(
            jnp.minimum, [jnp.where(gs[g] == m, g, N_GROUPS) for g in range(N_GROUPS)])
        for g in range(N_GROUPS):
            hit = first == g
            chosen[g] = jnp.where(hit, 1.0, chosen[g])
            gs[g] = jnp.where(hit, NEG_INF, gs[g])
    cand = jnp.concatenate(
        [jnp.where(chosen[g] > 0.5, groups[g], NEG_INF) for g in range(N_GROUPS)], axis=0)
    ei = lax.broadcasted_iota(I32, (E, tt), 0)
    ws = []
    for k in range(TOP_K):
        _, f = _first_max(cand, ei, E)
        hit = ei == f
        idx_ref[k:k + 1, :] = f
        ws.append(jnp.sum(jnp.where(hit, scores, 0.0), axis=0, keepdims=True))
        cand = jnp.where(hit, NEG_INF, cand)
    norm = ROUTED_SCALE / functools.reduce(jnp.add, ws)
    for k in range(TOP_K):
        w_ref[k:k + 1, :] = ws[k] * norm


def _route(logits_t, e_bias):
    B, E, S = logits_t.shape
    tt = min(512, S)
    return pl.pallas_call(
        _route_kernel,
        grid=(B, S // tt),
        in_specs=[pl.BlockSpec((None, E, tt), lambda b, i: (b, 0, i)),
                  pl.BlockSpec((E, 1), lambda b, i: (0, 0))],
        out_specs=[pl.BlockSpec((None, TOP_K, tt), lambda b, i: (b, 0, i)),
                   pl.BlockSpec((None, TOP_K, tt), lambda b, i: (b, 0, i))],
        out_shape=[jax.ShapeDtypeStruct((B, TOP_K, S), I32),
                   jax.ShapeDtypeStruct((B, TOP_K, S), F32)],
        compiler_params=_cp(("parallel", "parallel"), VMEM_LIMIT),
        name="route",
    )(logits_t, e_bias)


def _rank_kernel(idx_ref, rank_ref, cnt_ref, carry_ref):
    first = jnp.logical_and(pl.program_id(0) == 0, pl.program_id(1) == 0)

    @pl.when(first)
    def _():
        carry_ref[...] = jnp.zeros_like(carry_ref)

    K, tt = idx_ref.shape
    E = carry_ref.shape[0]
    ei = lax.broadcasted_iota(I32, (E, tt), 0)
    ri = lax.broadcasted_iota(I32, (tt, tt), 0)
    ci = lax.broadcasted_iota(I32, (tt, tt), 1)
    before = jnp.where(ri < ci, 1.0, 0.0).astype(BF16)
    base = carry_ref[...]
    for k in range(K):
        hit = ei == idx_ref[k:k + 1, :]
        onehot = jnp.where(hit, 1.0, 0.0)
        prefix = _dot(onehot.astype(BF16), before)
        rank = jnp.sum(jnp.where(hit, base + prefix, 0.0), axis=0, keepdims=True)
        rank_ref[k:k + 1, :] = rank.astype(I32)
        base = base + jnp.sum(onehot, axis=1, keepdims=True)
    carry_ref[...] = base
    cnt_ref[...] = base


def _rank(idx_t):
    B, K, S = idx_t.shape
    tt = min(256, S)
    return pl.pallas_call(
        _rank_kernel,
        grid=(B, S // tt),
        in_specs=[pl.BlockSpec((None, K, tt), lambda b, i: (b, 0, i))],
        out_specs=[pl.BlockSpec((None, K, tt), lambda b, i: (b, 0, i)),
                   pl.BlockSpec((N_EXPERTS, 1), lambda b, i: (0, 0))],
        out_shape=[jax.ShapeDtypeStruct((B, K, S), I32),
                   jax.ShapeDtypeStruct((N_EXPERTS, 1), F32)],
        scratch_shapes=[pltpu.VMEM((N_EXPERTS, 1), F32)],
        compiler_params=_cp(("arbitrary", "arbitrary"), VMEM_LIMIT),
        name="rank",
    )(idx_t)


def _pos_kernel(idx_ref, rank_ref, start_ref, pos_ref):
    K, tt = idx_ref.shape
    ei = lax.broadcasted_iota(I32, (N_EXPERTS, tt), 0)
    start = start_ref[...]
    for k in range(K):
        hit = ei == idx_ref[k:k + 1, :]
        off = jnp.sum(jnp.where(hit, start, 0.0), axis=0, keepdims=True)
        pos_ref[k:k + 1, :] = rank_ref[k:k + 1, :] + off.astype(I32)


def _pos(idx_t, rank_t, pstart):
    B, K, S = idx_t.shape
    tt = min(512, S)
    spec = pl.BlockSpec((None, K, tt), lambda b, i: (b, 0, i))
    return pl.pallas_call(
        _pos_kernel,
        grid=(B, S // tt),
        in_specs=[spec, spec, pl.BlockSpec((N_EXPERTS, 1), lambda b, i: (0, 0))],
        out_specs=spec,
        out_shape=jax.ShapeDtypeStruct((B, K, S), I32),
        compiler_params=_cp(("parallel", "parallel"), VMEM_LIMIT),
        name="slot",
    )(idx_t, rank_t, pstart)


def _dispatch_kernel(pos_ref, h_ref, xs_ref, sem, *, tt):
    def copy(t, k):
        return pltpu.make_async_copy(h_ref.at[pl.ds(t, 1), :],
                                     xs_ref.at[pos_ref[k, t]], sem)

    for t in range(tt):
        for k in range(TOP_K):
            copy(t, k).start(priority=k % 2)
    for t in range(tt):
        for k in range(TOP_K):
            copy(t, k).wait()


def _dispatch(pos_kt, h2_flat, n_rows):
    T, W = h2_flat.shape
    tt = 256
    return pl.pallas_call(
        functools.partial(_dispatch_kernel, tt=tt),
        grid=(T // tt,),
        in_specs=[pl.BlockSpec((TOP_K, tt), lambda i: (0, i), memory_space=pltpu.SMEM),
                  pl.BlockSpec((tt, W), lambda i: (i, 0))],
        out_specs=pl.BlockSpec(memory_space=pl.ANY),
        out_shape=jax.ShapeDtypeStruct((n_rows, 1, W), I32),
        scratch_shapes=[pltpu.SemaphoreType.DMA(())],
        compiler_params=pltpu.CompilerParams(dimension_semantics=("arbitrary",),
                                             has_side_effects=True),
        name="dispatch",
    )(pos_kt, h2_flat)


def _expert_kernel(be_ref, nu_ref, xs_ref, w1_ref, w3_ref, w2_ref, ys_ref, w1b, w3b, w2b):
    b = pl.program_id(0)
    prev = be_ref[jnp.maximum(b - 1, 0)]
    fresh = jnp.logical_or(b == 0, be_ref[b] != prev)

    @pl.when(jnp.logical_and(fresh, b < nu_ref[0]))
    def _():
        w1b[...] = w1_ref[...].astype(BF16)
        w3b[...] = w3_ref[...].astype(BF16)
        w2b[...] = w2_ref[...].astype(BF16)

    @pl.when(b < nu_ref[0])
    def _():
        half = ROW_PIECES * 128
        sub = EXPERT_BLOCK // EXPERT_SUB
        xs, hs = [], []
        for i in range(EXPERT_SUB):
            pieces = [xs_ref[pl.ds(i * sub * ROW_PIECES + c, sub, stride=ROW_PIECES), :]
                      for c in range(ROW_PIECES)]
            lo, hi = _unpack_pair(jnp.concatenate(pieces, axis=1))
            xs.append((lo.astype(BF16), hi.astype(BF16)))
        for lo, hi in xs:
            h1 = _dot(lo, w1b[:half, :]) + _dot(hi, w1b[half:, :])
            h3 = _dot(lo, w3b[:half, :]) + _dot(hi, w3b[half:, :])
            hs.append((h1, h3))
        for i, (h1, h3) in enumerate(hs):
            y = _dot((_silu(h1) * h3).astype(BF16), w2b[...])
            words = _pack_pair(y[:, :half], y[:, half:])
            for c in range(ROW_PIECES):
                ys_ref[pl.ds(i * sub * ROW_PIECES + c, sub, stride=ROW_PIECES), :] = (
                    words[:, c * 128:(c + 1) * 128])


def _experts(blk_expert, n_used, xs, w1, w3, w2, layer):
    P, _, W = xs.shape
    D = 2 * W
    nb = P // EXPERT_BLOCK
    xs = xs.reshape(P * ROW_PIECES, 128)
    last = lambda b, be, nu: jnp.minimum(b, nu[0] - 1)
    wspec = lambda r, c: pl.BlockSpec((None, None, r, c),
                                      lambda b, be, nu: (layer, be[last(b, be, nu)], 0, 0))
    grid_spec = pltpu.PrefetchScalarGridSpec(
        num_scalar_prefetch=2,
        grid=(nb,),
        in_specs=[pl.BlockSpec((EXPERT_BLOCK * ROW_PIECES, 128), lambda b, be, nu: (last(b, be, nu), 0)),
                  wspec(D, D_EXPERT), wspec(D, D_EXPERT), wspec(D_EXPERT, D)],
        out_specs=pl.BlockSpec((EXPERT_BLOCK * ROW_PIECES, 128), lambda b, be, nu: (last(b, be, nu), 0)),
        scratch_shapes=[pltpu.VMEM((D, D_EXPERT), BF16), pltpu.VMEM((D, D_EXPERT), BF16),
                        pltpu.VMEM((D_EXPERT, D), BF16)],
    )
    ys = pl.pallas_call(
        _expert_kernel,
        grid_spec=grid_spec,
        out_shape=jax.ShapeDtypeStruct((P * ROW_PIECES, 128), I32),
        compiler_params=_cp(("arbitrary",), VMEM_LIMIT),
        name="experts",
    )(blk_expert, n_used, xs, w1, w3, w2)
    return ys.reshape(P, 1, W)


def _combine_kernel(pos_ref, ys_ref, wt_ref, h2_ref, x1_ref, g2_ref, lg_ref, lb_ref, ws1_ref,
                    ws3_ref, ws2_ref, o_ref, buf, sem, *, tt, alpha):
    def copy(t, k):
        return pltpu.make_async_copy(ys_ref.at[pos_ref[k, t]], buf.at[k, pl.ds(t, 1), :], sem)

    for t in range(tt):
        for k in range(TOP_K):
            copy(t, k).start(priority=k % 2)
    half = h2_ref.shape[1]
    lo, hi = _unpack_pair(h2_ref[...])
    lo, hi = lo.astype(BF16), hi.astype(BF16)
    h1 = _dot(lo, ws1_ref[:half, :]) + _dot(hi, ws1_ref[half:, :])
    h3 = _dot(lo, ws3_ref[:half, :]) + _dot(hi, ws3_ref[half:, :])
    shared = _dot((_silu(h1) * h3).astype(BF16), ws2_ref[...])
    for t in range(tt):
        for k in range(TOP_K):
            copy(t, k).wait()
    wt = wt_ref[...]
    f_lo, f_hi = shared[:, :half], shared[:, half:]
    for k in range(TOP_K):
        y_lo, y_hi = _unpack_pair(buf[k])
        f_lo = f_lo + wt[:, k:k + 1] * y_lo
        f_hi = f_hi + wt[:, k:k + 1] * y_hi
    ffn = jnp.concatenate([f_lo, f_hi], axis=1)
    o_ref[...] = _ln(alpha * x1_ref[...] + g2_ref[...] * ffn) * lg_ref[...] + lb_ref[...]


def _combine(pos_kt, ys, w_tk, h2, x1, g2, ln_g, ln_b, ws1, ws3, ws2, alpha):
    B, S, D = x1.shape
    tt = min(256, S)
    nt = S // tt
    row = lambda n: pl.BlockSpec((None, tt, n), lambda b, i: (b, i, 0))
    full = lambda a: pl.BlockSpec(a.shape, lambda b, i: (0,) * a.ndim)
    return pl.pallas_call(
        functools.partial(_combine_kernel, tt=tt, alpha=alpha),
        grid=(B, nt),
        in_specs=[pl.BlockSpec((TOP_K, tt), lambda b, i: (0, b * nt + i), memory_space=pltpu.SMEM),
                  pl.BlockSpec(memory_space=pl.ANY),
                  row(TOP_K), row(D // 2), row(D),
                  pl.BlockSpec((None, 1, D), lambda b, i: (b, 0, 0)),
                  full(ln_g), full(ln_b), full(ws1), full(ws3), full(ws2)],
        out_specs=row(D),
        out_shape=jax.ShapeDtypeStruct((B, S, D), F32),
        scratch_shapes=[pltpu.VMEM((TOP_K, tt, D // 2), I32), pltpu.SemaphoreType.DMA(())],
        compiler_params=_cp(("arbitrary", "arbitrary"), VMEM_LIMIT),
        name="combine",
    )(pos_kt, ys, w_tk, h2, x1, g2, ln_g, ln_b, ws1, ws3, ws2)


def _arrange_w_in(w):
    z = lambda n: jnp.zeros((w.shape[0], n), w.dtype)
    return jnp.concatenate([w[:, 0:1280], w[:, 1296:1552], w[:, 1552:1680], w[:, 1680:1712],
                            z(96), w[:, 1280:1296], z(112)], axis=1).astype(BF16)


def _head_layout(nope, rope_a, rope_b):
    r = nope.shape[0]
    z = jnp.zeros((r, MLA_HEADS, MLA_HEAD_PAD - MLA_NOPE - MLA_ROPE), nope.dtype)
    return jnp.concatenate([nope, rope_a, rope_b, z], axis=2).reshape(r, MLA_HEADS * MLA_HEAD_PAD)


def _arrange_mla(w_uq, w_uk):
    half = MLA_ROPE // 2
    q = w_uq.reshape(MLA_Q_LORA, MLA_HEADS, MLA_NOPE + MLA_ROPE)
    qn, q1, q2 = q[..., :MLA_NOPE], q[..., MLA_NOPE:MLA_NOPE + half], q[..., MLA_NOPE + half:]
    wq = _head_layout(qn, q1, q2)
    wq_sw = _head_layout(jnp.zeros_like(qn), q2, q1)
    kn = w_uk.reshape(MLA_KV_LORA, MLA_HEADS, MLA_NOPE)
    zk = jnp.zeros((MLA_KV_LORA, MLA_HEADS, half), w_uk.dtype)
    eye = jnp.eye(MLA_ROPE, dtype=w_uk.dtype)
    e1 = jnp.broadcast_to(eye[:, None, :half], (MLA_ROPE, MLA_HEADS, half))
    e2 = jnp.broadcast_to(eye[:, None, half:], (MLA_ROPE, MLA_HEADS, half))
    zn = jnp.zeros((MLA_ROPE, MLA_HEADS, MLA_NOPE), w_uk.dtype)
    pad = jnp.zeros((128 - MLA_ROPE, MLA_HEADS * MLA_HEAD_PAD), w_uk.dtype)
    wk = jnp.concatenate([_head_layout(kn, zk, zk), _head_layout(zn, e1, e2), pad], axis=0)
    wk_sw = jnp.concatenate([_head_layout(jnp.zeros_like(kn), zk, zk), _head_layout(zn, e2, e1), pad],
                            axis=0)
    return wq.astype(BF16), wq_sw.astype(BF16), wk.astype(BF16), wk_sw.astype(BF16)


def _rope_rows():
    half = MLA_ROPE // 2
    inv_freq = ROPE_THETA ** (-jnp.arange(0, MLA_ROPE, 2, dtype=F32) / MLA_ROPE)
    z = lambda n: jnp.zeros((n,), F32)
    frq = jnp.concatenate([z(MLA_NOPE), inv_freq, inv_freq, z(MLA_HEAD_PAD - MLA_NOPE - MLA_ROPE)])
    sgn = jnp.concatenate([z(MLA_NOPE), -jnp.ones((half,), F32), jnp.ones((half,), F32),
                           z(MLA_HEAD_PAD - MLA_NOPE - MLA_ROPE)])
    return frq[None, :], sgn[None, :]


def _block_diag_pool(w_pool):
    G, C, _ = w_pool.shape
    out = jnp.zeros((G * C, G * C), w_pool.dtype)
    for g in range(G):
        out = out.at[g * C:(g + 1) * C, g * C:(g + 1) * C].set(w_pool[g])
    return out.astype(BF16)


def kernel(x, c, positions, w_ada, b_ada, w_in, w_pool, s_pool, conv_w, conv_b, gate_b, gn_w, g_q,
           g_kv, w_uq, w_uk, w_uv, w_out, ln1_g, ln1_b, w_router, e_bias, w1, w3, w2, ws1, ws3, ws2,
           ln2_g, ln2_b):
    B, S, D = x.shape
    depth = w_in.shape[0]
    T = B * S
    H = ML_HEADS
    alpha = float((2 * depth) ** 0.25)
    n_assign = T * TOP_K
    n_blocks = n_assign // EXPERT_BLOCK + N_EXPERTS
    n_rows = n_blocks * EXPERT_BLOCK

    ada = _ada(c, w_ada, b_ada)
    rope_cos, rope_sin = _rope_tables(positions.reshape(B, S, 1), *_rope_rows())
    qk_scale = jnp.concatenate([jnp.ones((1, ML_WIDTH), F32),
                                jnp.full((1, ML_WIDTH), ML_DH ** -0.5, F32)], axis=1)
    ones_row = lambda n: jnp.ones((B, n, 1, S), BF16)
    zero_rows = lambda n, r: jnp.zeros((B, n, r, S), BF16)

    for l in range(depth):
        sh1, sc1, g1, sh2, sc2, g2 = [a.reshape(B, 1, D) for a in jnp.split(ada[l], 6, axis=-1)]
        u_pool, u_qk, u_v, u_o, u_dq, u_kv, u_gate = _inproj(x, sc1, sh1, _arrange_w_in(w_in[l]))
        y_pool = _pool(u_pool, _block_diag_pool(w_pool[l]), s_pool[l][None, :])

        qk_act = _conv(u_qk, conv_w[l], conv_b[l][None, :], qk_scale)
        heads_t = lambda a: a.reshape(B, S, H, ML_DH).transpose(0, 2, 3, 1)
        gates = u_gate[:, :, :4 * H].reshape(B, S, 4, H)
        vt_ml = jnp.concatenate([heads_t(u_v), ones_row(H), zero_rows(H, ML_VT_ROWS - ML_DH - 1)],
                                axis=2)
        y_ml = _mlstm(qk_act[:, :, ML_WIDTH:].reshape(B, S, H, ML_DH).transpose(0, 2, 1, 3),
                      heads_t(qk_act[:, :, :ML_WIDTH]), vt_ml, heads_t(u_o),
                      gates.transpose(0, 3, 1, 2),
                      gates.transpose(0, 3, 2, 1),
                      gate_b[l].reshape(4, H).T.reshape(H, 1, 4),
                      gate_b[l].reshape(4, H).T.reshape(H, 4, 1),
                      gn_w[l].reshape(H, ML_DH, 1))
        y_ml = y_ml.transpose(0, 3, 1, 2).reshape(B, S, ML_WIDTH)

        wq, wq_sw, wk, wk_sw = _arrange_mla(w_uq[l], w_uk[l])
        qh, kh, vh = _mla_proj(u_dq, u_kv, rope_cos, rope_sin, g_q[l][None, :], g_kv[l][None, :],
                               wq, wq_sw, wk, wk_sw, w_uv[l].astype(BF16))
        y_mla = _attn(qh, kh, vh.transpose(0, 2, 1)).transpose(0, 2, 1)

        x1, h2, logits_t = _outproj(y_pool, y_ml, y_mla, x, g1, ln1_g[l][None, :], ln1_b[l][None, :],
                                    sc2, sh2, w_out[l].astype(BF16), w_router[l].T.astype(BF16), alpha)
        idx_t, w_t = _route(logits_t, e_bias[l][:, None])
        rank_t, counts = _rank(idx_t)
        counts = counts[:, 0].astype(I32)
        padded = (counts + EXPERT_BLOCK - 1) // EXPERT_BLOCK * EXPERT_BLOCK
        pend = jnp.cumsum(padded)
        pstart = pend - padded
        n_used = (pend[-1:] // EXPERT_BLOCK).astype(I32)
        blk_start = jnp.arange(n_blocks, dtype=I32) * EXPERT_BLOCK
        blk_expert = jnp.minimum(jnp.sum((pend[None, :] <= blk_start[:, None]).astype(I32), axis=1),
                                 N_EXPERTS - 1)
        pos_kt = _pos(idx_t, rank_t, pstart.astype(F32)[:, None]).transpose(1, 0, 2).reshape(TOP_K, T)
        xs = _dispatch(pos_kt, h2.reshape(T, D // 2), n_rows)
        ys = _experts(blk_expert, n_used, xs, w1, w3, w2, l)
        x = _combine(pos_kt, ys, w_t.transpose(0, 2, 1), h2, x1, g2, ln2_g[l][None, :],
                     ln2_b[l][None, :], ws1[l].astype(BF16), ws3[l].astype(BF16),
                     ws2[l].astype(BF16), alpha)
    return x
```

```python
import functools

import jax
import jax.numpy as jnp
import numpy as np
from jax import lax
from jax.experimental import pallas as pl
from jax.experimental.pallas import tpu as pltpu

F32 = jnp.float32
BF16 = jnp.bfloat16
I32 = jnp.int32

D_MODEL = 1024
POOL_WINDOWS = (2, 4, 8, 16)
POOL_GROUP_DIM = 64
POOL_WIDTH = 256
ML_HEADS = 4
ML_DH = 64
ML_WIDTH = 256
ML_CONV = 5
ML_VT_ROWS = 80
ML_HEADS_PER_STEP = 2
MLA_HEADS = 8
MLA_NOPE = 64
MLA_ROPE = 32
MLA_V = 64
MLA_Q_LORA = 256
MLA_KV_LORA = 128
MLA_HEAD_PAD = 128
ATTN_HEADS_PER_STEP = 2
ATTN_Q_SPLIT = 2
ROPE_THETA = 10000.0
N_EXPERTS = 256
TOP_K = 8
N_GROUPS = 8
TOPK_GROUPS = 4
GROUP_SIZE = N_EXPERTS // N_GROUPS
D_EXPERT = 256
ROUTED_SCALE = 2.5
LN_EPS = 1e-5
RMS_EPS = 1e-6
NEG_BIG = -1e30
NEG_INF = float("-inf")

EXPERT_BLOCK = 512
EXPERT_SUB = 2
ROW_PIECES = D_MODEL // 2 // 128
VMEM_LIMIT = 56 * 1024 * 1024


def _cp(sem, vmem=None):
    return pltpu.CompilerParams(dimension_semantics=sem, vmem_limit_bytes=vmem)


def _ln(x):
    mu = jnp.mean(x, axis=-1, keepdims=True)
    xc = x - mu
    var = jnp.mean(xc * xc, axis=-1, keepdims=True)
    return xc * lax.rsqrt(var + LN_EPS)


def _silu(x):
    return x * jax.nn.sigmoid(x)


def _rms(x, g):
    return x * lax.rsqrt(jnp.mean(x * x, axis=-1, keepdims=True) + RMS_EPS) * g


def _dot(a, b):
    return jnp.dot(a, b, preferred_element_type=F32)


def _dot_nt(a, b):
    return lax.dot_general(a, b, (((1,), (1,)), ((), ())), preferred_element_type=F32)


def _dot_tn(a, b):
    return lax.dot_general(a, b, (((0,), (0,)), ((), ())), preferred_element_type=F32)


def _pack_pair(lo, hi):
    lo_b = lax.bitcast_convert_type(lo.astype(BF16).astype(F32), I32)
    hi_b = lax.bitcast_convert_type(hi.astype(BF16).astype(F32), I32)
    return jnp.bitwise_or(hi_b, lax.shift_right_logical(lo_b, 16))


def _unpack_pair(w):
    lo = lax.bitcast_convert_type(lax.shift_left(w, 16), F32)
    hi = lax.bitcast_convert_type(jnp.bitwise_and(w, -65536), F32)
    return lo, hi


def _ada_kernel(c_ref, w_ref, b_ref, o_ref):
    ca = _silu(c_ref[...]).astype(BF16)
    o_ref[...] = _dot(ca, w_ref[...].astype(BF16)) + b_ref[...]


def _ada(c, w_ada, b_ada):
    L, D, N = w_ada.shape
    B = c.shape[0]
    tn = 1536
    return pl.pallas_call(
        _ada_kernel,
        grid=(L, N // tn),
        in_specs=[pl.BlockSpec((B, D), lambda l, j: (0, 0)),
                  pl.BlockSpec((None, D, tn), lambda l, j: (l, 0, j)),
                  pl.BlockSpec((None, 1, tn), lambda l, j: (l, 0, j))],
        out_specs=pl.BlockSpec((None, B, tn), lambda l, j: (l, 0, j)),
        out_shape=jax.ShapeDtypeStruct((L, B, N), F32),
        compiler_params=_cp(("parallel", "parallel"), VMEM_LIMIT),
        name="ada",
    )(c, w_ada, b_ada.reshape(L, 1, N))


def _inproj_kernel(x_ref, sc_ref, sh_ref, w_ref, pool_ref, qk_ref, v_ref, o_ref, dq_ref,
                   kv_ref, gate_ref):
    h = (_ln(x_ref[...]) * (1.0 + sc_ref[...]) + sh_ref[...]).astype(BF16)
    pool_ref[...] = _dot(h, w_ref[:, 0:256])
    qk_ref[...] = _dot(h, w_ref[:, 256:768])
    v_ref[...] = _dot(h, w_ref[:, 768:1024]).astype(BF16)
    o_ref[...] = _dot(h, w_ref[:, 1024:1280])
    dq_ref[...] = _dot(h, w_ref[:, 1280:1536])
    kv_ref[...] = _dot(h, w_ref[:, 1536:1792])
    gate_ref[...] = _dot(h, w_ref[:, 1792:1920])


def _inproj(x, sc, sh, w):
    B, S, D = x.shape
    tm = min(512, S)
    widths = (256, 512, 256, 256, 256, 256, 128)
    dtypes = (F32, F32, BF16, F32, F32, F32, F32)
    row = lambda n: pl.BlockSpec((None, tm, n), lambda b, i: (b, i, 0))
    return pl.pallas_call(
        _inproj_kernel,
        grid=(B, S // tm),
        in_specs=[row(D),
                  pl.BlockSpec((None, 1, D), lambda b, i: (b, 0, 0)),
                  pl.BlockSpec((None, 1, D), lambda b, i: (b, 0, 0)),
                  pl.BlockSpec(w.shape, lambda b, i: (0, 0))],
        out_specs=[row(n) for n in widths],
        out_shape=[jax.ShapeDtypeStruct((B, S, n), dt) for n, dt in zip(widths, dtypes)],
        compiler_params=_cp(("parallel", "parallel"), VMEM_LIMIT),
        name="inproj",
    )(x, sc, sh, w)


def _shift_dn(a, k, row):
    return jnp.where(row >= k, pltpu.roll(a, k, 0), 0.0)


def _shift_up(a, k, row):
    n = a.shape[0]
    return jnp.where(row < n - k, pltpu.roll(a, n - k, 0), 0.0)


def _pool_kernel(u_ref, w_ref, s_ref, y_ref):
    x = u_ref[...]
    S, C = x.shape
    row = lax.broadcasted_iota(I32, (S, C), 0)
    lane = lax.broadcasted_iota(I32, (S, C), 1)
    rowf = row.astype(F32)
    p1, f1 = x, x
    p2 = p1 + _shift_dn(p1, 1, row)
    f2 = f1 + _shift_up(f1, 1, row)
    p4 = p2 + _shift_dn(p2, 2, row)
    f4 = f2 + _shift_up(f2, 2, row)
    p8 = p4 + _shift_dn(p4, 4, row)
    f8 = f4 + _shift_up(f4, 4, row)
    wins = []
    for half, p, f in ((1, p1, f1), (2, p2, f2), (4, p4, f4), (8, p8, f8)):
        total = _shift_dn(p, 1, row) + f
        cnt = jnp.minimum(rowf + half, float(S)) - jnp.maximum(rowf - half, 0.0)
        wins.append(total / cnt)
    pooled = jnp.where(lane < 64, wins[0],
                       jnp.where(lane < 128, wins[1], jnp.where(lane < 192, wins[2], wins[3])))
    d = (pooled - x).astype(BF16)
    y_ref[...] = (_dot(d, w_ref[...]) * s_ref[...]).astype(BF16)


def _pool(u_pool, w_bd, s_pool):
    B, S, C = u_pool.shape
    return pl.pallas_call(
        _pool_kernel,
        grid=(B,),
        in_specs=[pl.BlockSpec((None, S, C), lambda b: (b, 0, 0)),
                  pl.BlockSpec((C, C), lambda b: (0, 0)),
                  pl.BlockSpec((1, C), lambda b: (0, 0))],
        out_specs=pl.BlockSpec((None, S, C), lambda b: (b, 0, 0)),
        out_shape=jax.ShapeDtypeStruct((B, S, C), BF16),
        compiler_params=_cp(("parallel",), VMEM_LIMIT),
        name="pool",
    )(u_pool, w_bd, s_pool)


def _conv_kernel(u_ref, w_ref, b_ref, scale_ref, y_ref):
    x = u_ref[...]
    S, C = x.shape
    row = lax.broadcasted_iota(I32, (S, C), 0)
    y = (_shift_dn(x, 2, row) * w_ref[0:1, :] + _shift_dn(x, 1, row) * w_ref[1:2, :]
         + x * w_ref[2:3, :] + _shift_up(x, 1, row) * w_ref[3:4, :]
         + _shift_up(x, 2, row) * w_ref[4:5, :] + b_ref[...])
    y_ref[...] = (_silu(y) * scale_ref[...]).astype(BF16)


def _conv(u_qk, conv_w, conv_b, scale):
    B, S, C = u_qk.shape
    return pl.pallas_call(
        _conv_kernel,
        grid=(B,),
        in_specs=[pl.BlockSpec((None, S, C), lambda b: (b, 0, 0)),
                  pl.BlockSpec((ML_CONV, C), lambda b: (0, 0)),
                  pl.BlockSpec((1, C), lambda b: (0, 0)),
                  pl.BlockSpec((1, C), lambda b: (0, 0))],
        out_specs=pl.BlockSpec((None, S, C), lambda b: (b, 0, 0)),
        out_shape=jax.ShapeDtypeStruct((B, S, C), BF16),
        compiler_params=_cp(("parallel",), VMEM_LIMIT),
        name="mlstm_conv",
    )(u_qk, conv_w, conv_b, scale)


def _log_sigmoid(x):
    return jnp.minimum(x, 0.0) - jnp.log(1.0 + jnp.exp(-jnp.abs(x)))


def _split3(x):
    a = x.astype(BF16)
    r = x - a.astype(F32)
    b = r.astype(BF16)
    c = (r - b.astype(F32)).astype(BF16)
    return a, b, c


def _mlstm_chunk(s, k, qt, vt, c_col, b_row, i_row, b_last, allowed, ct, m_prev):
    logw = jnp.where(allowed, c_col + b_row, NEG_INF)
    m_inter = b_row + m_prev
    m_t = jnp.maximum(m_inter, jnp.max(logw, axis=0, keepdims=True))
    w_intra = jnp.exp(logw - m_t)
    w_inter = jnp.exp(m_inter - m_t)
    qk = (s * w_intra).astype(BF16)
    nd = _dot(vt, qk) + w_inter * _dot(ct.astype(BF16), qt)
    den = jnp.maximum(jnp.abs(nd[ML_DH:ML_DH + 1, :]), jnp.exp(-m_t))
    h = nd[:ML_DH, :] / den
    g = b_last - b_row + i_row
    m_new = jnp.maximum(b_last + m_prev, jnp.max(g, axis=1, keepdims=True))
    w_k = jnp.exp(g - m_new)
    decay = jnp.exp(b_last + m_prev - m_new)
    ct_new = decay * ct + _dot((vt.astype(F32) * w_k).astype(BF16), k)
    return h, ct_new, m_new


def _mlstm_kernel(k_ref, qt_ref, vt_ref, uo_ref, gc_ref, gr_ref, bc_ref, br_ref, gn_ref, y_ref,
                  hf_ref, hb_ref, *, chunk):
    nh, S = k_ref.shape[0], k_ref.shape[1]
    L = chunk
    nc = S // L
    ri = lax.broadcasted_iota(I32, (L, L), 0)
    ci = lax.broadcasted_iota(I32, (L, L), 1)
    lower = ci <= ri
    upper = ci >= ri
    tril = jnp.where(lower, 1.0, 0.0).astype(BF16)
    triu = jnp.where(upper, 1.0, 0.0).astype(BF16)

    def prep(hh, r0, forward):
        gc = gc_ref[hh, pl.ds(r0, L), :] + bc_ref[hh]
        gr = gr_ref[hh, :, pl.ds(r0, L)] + br_ref[hh]
        c1, c2, c3 = _split3(_log_sigmoid(gc))
        r1, r2, r3 = _split3(_log_sigmoid(gr))
        k = k_ref[hh, pl.ds(r0, L), :]
        qt = qt_ref[hh, :, pl.ds(r0, L)]
        s = _dot(k, qt)
        if forward:
            cum_c = (_dot(tril, c1) + _dot(tril, c2) + _dot(tril, c3))[:, 1:2]
            cum_r = (_dot(r1, triu) + _dot(r2, triu) + _dot(r3, triu))[1:2, :]
            return s, k, qt, gc[:, 0:1] - cum_c, cum_r, gr[0:1, :], cum_r[:, L - 1:L]
        cum_c = (_dot(triu, c1) + _dot(triu, c2) + _dot(triu, c3))[:, 3:4]
        cum_r = (_dot(r1, tril) + _dot(r2, tril) + _dot(r3, tril))[3:4, :]
        return s, k, qt, gc[:, 2:3] - cum_c, cum_r, gr[2:3, :], cum_r[:, 0:1]

    def step(i, carry):
        rf = pl.multiple_of(i * L, L)
        rb = pl.multiple_of((nc - 1 - i) * L, L)
        preps = [(prep(hh, rf, True), prep(hh, rb, False)) for hh in range(nh)]
        out = []
        for hh, (pf, pb) in enumerate(preps):
            cf, mf, cb, mb = carry[4 * hh:4 * hh + 4]
            h, cf, mf = _mlstm_chunk(*pf[:3], vt_ref[hh, :, pl.ds(rf, L)], *pf[3:], upper, cf, mf)
            hf_ref[hh, :, pl.ds(rf, L)] = h
            h, cb, mb = _mlstm_chunk(*pb[:3], vt_ref[hh, :, pl.ds(rb, L)], *pb[3:], lower, cb, mb)
            hb_ref[hh, :, pl.ds(rb, L)] = h
            out += [cf, mf, cb, mb]
        return tuple(out)

    c0 = jnp.zeros((ML_VT_ROWS, ML_DH), F32)
    m0 = jnp.full((1, 1), NEG_BIG, F32)
    lax.fori_loop(0, nc, step, (c0, m0, c0, m0) * nh)
    for hh in range(nh):
        h = hf_ref[hh] + hb_ref[hh]
        mu = jnp.mean(h, axis=0, keepdims=True)
        hc = h - mu
        var = jnp.mean(hc * hc, axis=0, keepdims=True)
        hn = hc * lax.rsqrt(var + LN_EPS)
        y_ref[hh] = (jax.nn.sigmoid(uo_ref[hh]) * (hn * gn_ref[hh])).astype(BF16)


def _mlstm(k_heads, qt_heads, vt_ext, uot_heads, g_col, g_row, b_col, b_row, gn_w):
    B, H, S, dh = k_heads.shape
    L = min(256, S)
    nh = ML_HEADS_PER_STEP
    seq = lambda r, c: pl.BlockSpec((None, nh, r, c), lambda b, h: (b, h, 0, 0))
    par = lambda r, c: pl.BlockSpec((nh, r, c), lambda b, h: (h, 0, 0))
    return pl.pallas_call(
        functools.partial(_mlstm_kernel, chunk=L),
        grid=(B, H // nh),
        in_specs=[seq(S, dh), seq(dh, S), seq(ML_VT_ROWS, S), seq(dh, S), seq(S, 4), seq(4, S),
                  par(1, 4), par(4, 1), par(dh, 1)],
        out_specs=seq(dh, S),
        out_shape=jax.ShapeDtypeStruct((B, H, dh, S), BF16),
        scratch_shapes=[pltpu.VMEM((nh, dh, S), F32), pltpu.VMEM((nh, dh, S), F32)],
        compiler_params=_cp(("parallel", "parallel"), VMEM_LIMIT),
        name="mlstm_scan",
    )(k_heads, qt_heads, vt_ext, uot_heads, g_col, g_row, b_col, b_row, gn_w)


def _rope_kernel(pos_ref, frq_ref, sgn_ref, cos_ref, sin_ref):
    ang = pos_ref[...].astype(F32) * frq_ref[...]
    cos_ref[...] = jnp.cos(ang)
    sin_ref[...] = jnp.sin(ang) * sgn_ref[...]


def _rope_tables(pos, frq, sgn):
    B, S, _ = pos.shape
    tm = min(512, S)
    row = lambda n: pl.BlockSpec((None, tm, n), lambda b, i: (b, i, 0))
    full = lambda a: pl.BlockSpec(a.shape, lambda b, i: (0,) * a.ndim)
    return pl.pallas_call(
        _rope_kernel,
        grid=(B, S // tm),
        in_specs=[row(1), full(frq), full(sgn)],
        out_specs=[row(MLA_HEAD_PAD), row(MLA_HEAD_PAD)],
        out_shape=[jax.ShapeDtypeStruct((B, S, MLA_HEAD_PAD), F32)] * 2,
        compiler_params=_cp(("parallel", "parallel"), VMEM_LIMIT),
        name="rope_tables",
    )(pos, frq, sgn)


def _mla_proj_kernel(dq_ref, kv_ref, cos_ref, sin_ref, gq_ref, gkv_ref, wq_ref, wqs_ref,
                     wk_ref, wks_ref, wv_ref, q_ref, k_ref, v_ref, *, scale):
    cos = cos_ref[...]
    sin = sin_ref[...]
    qn = _rms(dq_ref[...], gq_ref[...]).astype(BF16)
    a = _dot(qn, wq_ref[...])
    a_sw = _dot(qn, wqs_ref[...])
    kv = kv_ref[...]
    ckv = _rms(kv[:, :MLA_KV_LORA], gkv_ref[...])
    kin = jnp.concatenate([ckv, kv[:, MLA_KV_LORA:]], axis=1).astype(BF16)
    ak = _dot(kin, wk_ref[...])
    ak_sw = _dot(kin, wks_ref[...])
    for h in range(MLA_HEADS):
        sl = slice(h * MLA_HEAD_PAD, (h + 1) * MLA_HEAD_PAD)
        q_ref[:, sl] = ((a[:, sl] * cos + a_sw[:, sl] * sin) * scale).astype(BF16)
        k_ref[:, sl] = (ak[:, sl] * cos + ak_sw[:, sl] * sin).astype(BF16)
    v_ref[...] = _dot(ckv.astype(BF16), wv_ref[...]).astype(BF16)


def _mla_proj(u_dq, u_kv, cos, sin, g_q, g_kv, wq, wqs, wk, wks, wv):
    B, S, _ = u_dq.shape
    tm = min(512, S)
    hw = MLA_HEADS * MLA_HEAD_PAD
    vw = MLA_HEADS * MLA_V
    row = lambda n: pl.BlockSpec((None, tm, n), lambda b, i: (b, i, 0))
    full = lambda a: pl.BlockSpec(a.shape, lambda b, i: (0,) * a.ndim)
    scale = (MLA_NOPE + MLA_ROPE) ** -0.5 * float(np.log2(np.e))
    return pl.pallas_call(
        functools.partial(_mla_proj_kernel, scale=scale),
        grid=(B, S // tm),
        in_specs=[row(MLA_Q_LORA), row(256), row(MLA_HEAD_PAD), row(MLA_HEAD_PAD), full(g_q), full(g_kv),
                  full(wq), full(wqs), full(wk), full(wks), full(wv)],
        out_specs=[row(hw), row(hw), row(vw)],
        out_shape=[jax.ShapeDtypeStruct((B, S, hw), BF16),
                   jax.ShapeDtypeStruct((B, S, hw), BF16),
                   jax.ShapeDtypeStruct((B, S, vw), BF16)],
        compiler_params=_cp(("parallel", "parallel"), VMEM_LIMIT),
        name="mla_proj",
    )(u_dq, u_kv, cos, sin, g_q, g_kv, wq, wqs, wk, wks, wv)


def _attn_kernel(q_ref, k_ref, vt_ref, o_ref, *, tk):
    S = k_ref.shape[0]
    nk = S // tk
    nq = ATTN_Q_SPLIT
    tq = q_ref.shape[0] // nq
    heads = q_ref.shape[1] // MLA_HEAD_PAD

    def scores(c, j):
        h, qi = c // nq, c % nq
        k = k_ref[j * tk:(j + 1) * tk, h * MLA_HEAD_PAD:(h + 1) * MLA_HEAD_PAD]
        return _dot_nt(k, q_ref[qi * tq:(qi + 1) * tq, h * MLA_HEAD_PAD:(h + 1) * MLA_HEAD_PAD])

    hp = heads * nq
    m = [jnp.full((1, tq), NEG_INF, F32)] * hp
    l = [jnp.zeros((1, tq), F32)] * hp
    acc = [jnp.zeros((MLA_V, tq), F32)] * hp
    s = [scores(h, 0) for h in range(hp)]
    for j in range(nk):
        for h in range(hp):
            s_cur = s[h]
            if j + 1 < nk:
                s[h] = scores(h, j + 1)
            m_new = jnp.maximum(m[h], jnp.max(s_cur, axis=0, keepdims=True))
            alpha = jnp.exp2(m[h] - m_new)
            p = jnp.exp2(s_cur - m_new)
            l[h] = alpha * l[h] + jnp.sum(p, axis=0, keepdims=True)
            hd = h // nq
            vt = vt_ref[hd * MLA_V:(hd + 1) * MLA_V, j * tk:(j + 1) * tk]
            acc[h] = alpha * acc[h] + _dot(vt, p.astype(BF16))
            m[h] = m_new
    for h in range(hp):
        hd, qi = h // nq, h % nq
        o_ref[hd * MLA_V:(hd + 1) * MLA_V, qi * tq:(qi + 1) * tq] = (acc[h] / l[h]).astype(BF16)


def _attn(qh, kh, vt):
    B, S, _ = qh.shape
    tq = min(512, S)
    tk = min(256, S)
    hp = ATTN_HEADS_PER_STEP
    pw = hp * MLA_HEAD_PAD
    return pl.pallas_call(
        functools.partial(_attn_kernel, tk=tk),
        grid=(B, MLA_HEADS // hp, S // tq),
        in_specs=[pl.BlockSpec((None, tq, pw), lambda b, p, i: (b, i, p)),
                  pl.BlockSpec((None, S, pw), lambda b, p, i: (b, 0, p)),
                  pl.BlockSpec((None, hp * MLA_V, S), lambda b, p, i: (b, p, 0))],
        out_specs=pl.BlockSpec((None, hp * MLA_V, tq), lambda b, p, i: (b, p, i)),
        out_shape=jax.ShapeDtypeStruct((B, MLA_HEADS * MLA_V, S), BF16),
        compiler_params=_cp(("parallel", "parallel", "parallel"), VMEM_LIMIT),
        name="mla_attn",
    )(qh, kh, vt)


def _outproj_kernel(yp_ref, ym_ref, ya_ref, x_ref, g1_ref, lg_ref, lb_ref, sc_ref, sh_ref, wo_ref,
                    wr_ref, x1_ref, h2_ref, lt_ref, *, alpha):
    mix = (_dot(yp_ref[...], wo_ref[0:256, :]) + _dot(ym_ref[...], wo_ref[256:512, :])
           + _dot(ya_ref[...], wo_ref[512:1024, :]))
    x1 = _ln(alpha * x_ref[...] + g1_ref[...] * mix) * lg_ref[...] + lb_ref[...]
    x1_ref[...] = x1
    h2 = _ln(x1) * (1.0 + sc_ref[...]) + sh_ref[...]
    half = h2.shape[1] // 2
    h2_ref[...] = _pack_pair(h2[:, :half], h2[:, half:])
    lt_ref[...] = _dot_nt(wr_ref[...], h2.astype(BF16))


def _outproj(y_pool, y_ml, y_mla, x, g1, ln_g, ln_b, sc2, sh2, w_out, w_router_t, alpha):
    B, S, D = x.shape
    tm = min(512, S)
    row = lambda n: pl.BlockSpec((None, tm, n), lambda b, i: (b, i, 0))
    per_b = pl.BlockSpec((None, 1, D), lambda b, i: (b, 0, 0))
    vec = pl.BlockSpec((1, D), lambda b, i: (0, 0))
    return pl.pallas_call(
        functools.partial(_outproj_kernel, alpha=alpha),
        grid=(B, S // tm),
        in_specs=[row(256), row(256), row(512), row(D), per_b, vec, vec, per_b, per_b,
                  pl.BlockSpec((D, D), lambda b, i: (0, 0)),
                  pl.BlockSpec((N_EXPERTS, D), lambda b, i: (0, 0))],
        out_specs=[row(D), row(D // 2), pl.BlockSpec((None, N_EXPERTS, tm), lambda b, i: (b, 0, i))],
        out_shape=[jax.ShapeDtypeStruct((B, S, D), F32), jax.ShapeDtypeStruct((B, S, D // 2), I32),
                   jax.ShapeDtypeStruct((B, N_EXPERTS, S), F32)],
        compiler_params=_cp(("parallel", "parallel"), VMEM_LIMIT),
        name="outproj",
    )(y_pool, y_ml, y_mla, x, g1, ln_g, ln_b, sc2, sh2, w_out, w_router_t)


def _first_max(v, iota, n):
    m = jnp.max(v, axis=0, keepdims=True)
    first = jnp.min(jnp.where(v == m, iota, n), axis=0, keepdims=True)
    return m, first


def _route_kernel(lt_ref, eb_ref, idx_ref, w_ref):
    scores = jax.nn.sigmoid(lt_ref[...])
    sel = scores + eb_ref[...]
    E, tt = sel.shape
    gi = lax.broadcasted_iota(I32, (GROUP_SIZE, tt), 0)
    groups = [sel[g * GROUP_SIZE:(g + 1) * GROUP_SIZE, :] for g in range(N_GROUPS)]
    gs = []
    for grp in groups:
        m1, f1 = _first_max(grp, gi, GROUP_SIZE)
        m2 = jnp.max(jnp.where(gi == f1, NEG_INF, grp), axis=0, keepdims=True)
        gs.append(m1 + m2)
    chosen = [jnp.zeros((1, tt), F32) for _ in range(N_GROUPS)]
    for _ in range(TOPK_GROUPS):
        m = functools.reduce(jnp.maximum, gs)
        first = functools.reduce(
            jnp.minimum, [jnp.where(gs[g] == m, g, N_GROUPS) for g in range(N_GROUPS)])
        for g in range(N_GROUPS):
            hit = first == g
            chosen[g] = jnp.where(hit, 1.0, chosen[g])
            gs[g] = jnp.where(hit, NEG_INF, gs[g])
    cand = jnp.concatenate(
        [jnp.where(chosen[g] > 0.5, groups[g], NEG_INF) for g in range(N_GROUPS)], axis=0)
    ei = lax.broadcasted_iota(I32, (E, tt), 0)
    ws = []
    for k in range(TOP_K):
        _, f = _first_max(cand, ei, E)
        hit = ei == f
        idx_ref[k:k + 1, :] = f
        ws.append(jnp.sum(jnp.where(hit, scores, 0.0), axis=0, keepdims=True))
        cand = jnp.where(hit, NEG_INF, cand)
    norm = ROUTED_SCALE / functools.reduce(jnp.add, ws)
    for k in range(TOP_K):
        w_ref[k:k + 1, :] = ws[k] * norm


def _route(logits_t, e_bias):
    B, E, S = logits_t.shape
    tt = min(512, S)
    return pl.pallas_call(
        _route_kernel,
        grid=(B, S // tt),
        in_specs=[pl.BlockSpec((None, E, tt), lambda b, i: (b, 0, i)),
                  pl.BlockSpec((E, 1), lambda b, i: (0, 0))],
        out_specs=[pl.BlockSpec((None, TOP_K, tt), lambda b, i: (b, 0, i)),
                   pl.BlockSpec((None, TOP_K, tt), lambda b, i: (b, 0, i))],
        out_shape=[jax.ShapeDtypeStruct((B, TOP_K, S), I32),
                   jax.ShapeDtypeStruct((B, TOP_K, S), F32)],
        compiler_params=_cp(("parallel", "parallel"), VMEM_LIMIT),
        name="route",
    )(logits_t, e_bias)


def _rank_kernel(idx_ref, rank_ref, cnt_ref, carry_ref):
    first = jnp.logical_and(pl.program_id(0) == 0, pl.program_id(1) == 0)

    @pl.when(first)
    def _():
        carry_ref[...] = jnp.zeros_like(carry_ref)

    K, tt = idx_ref.shape
    E = carry_ref.shape[0]
    ei = lax.broadcasted_iota(I32, (E, tt), 0)
    ri = lax.broadcasted_iota(I32, (tt, tt), 0)
    ci = lax.broadcasted_iota(I32, (tt, tt), 1)
    before = jnp.where(ri < ci, 1.0, 0.0).astype(BF16)
    base = carry_ref[...]
    for k in range(K):
        hit = ei == idx_ref[k:k + 1, :]
        onehot = jnp.where(hit, 1.0, 0.0)
        prefix = _dot(onehot.astype(BF16), before)
        rank = jnp.sum(jnp.where(hit, base + prefix, 0.0), axis=0, keepdims=True)
        rank_ref[k:k + 1, :] = rank.astype(I32)
        base = base + jnp.sum(onehot, axis=1, keepdims=True)
    carry_ref[...] = base
    cnt_ref[...] = base


def _rank(idx_t):
    B, K, S = idx_t.shape
    tt = min(256, S)
    return pl.pallas_call(
        _rank_kernel,
        grid=(B, S // tt),
        in_specs=[pl.BlockSpec((None, K, tt), lambda b, i: (b, 0, i))],
        out_specs=[pl.BlockSpec((None, K, tt), lambda b, i: (b, 0, i)),
                   pl.BlockSpec((N_EXPERTS, 1), lambda b, i: (0, 0))],
        out_shape=[jax.ShapeDtypeStruct((B, K, S), I32),
                   jax.ShapeDtypeStruct((N_EXPERTS, 1), F32)],
        scratch_shapes=[pltpu.VMEM((N_EXPERTS, 1), F32)],
        compiler_params=_cp(("arbitrary", "arbitrary"), VMEM_LIMIT),
        name="rank",
    )(idx_t)


def _pos_kernel(idx_ref, rank_ref, start_ref, pos_ref):
    K, tt = idx_ref.shape
    ei = lax.broadcasted_iota(I32, (N_EXPERTS, tt), 0)
    start = start_ref[...]
    for k in range(K):
        hit = ei == idx_ref[k:k + 1, :]
        off = jnp.sum(jnp.where(hit, start, 0.0), axis=0, keepdims=True)
        pos_ref[k:k + 1, :] = rank_ref[k:k + 1, :] + off.astype(I32)


def _pos(idx_t, rank_t, pstart):
    B, K, S = idx_t.shape
    tt = min(512, S)
    spec = pl.BlockSpec((None, K, tt), lambda b, i: (b, 0, i))
    return pl.pallas_call(
        _pos_kernel,
        grid=(B, S // tt),
        in_specs=[spec, spec, pl.BlockSpec((N_EXPERTS, 1), lambda b, i: (0, 0))],
        out_specs=spec,
        out_shape=jax.ShapeDtypeStruct((B, K, S), I32),
        compiler_params=_cp(("parallel", "parallel"), VMEM_LIMIT),
        name="slot",
    )(idx_t, rank_t, pstart)


def _dispatch_kernel(pos_ref, h_ref, xs_ref, sem, *, tt):
    def copy(t, k):
        return pltpu.make_async_copy(h_ref.at[pl.ds(t, 1), :],
                                     xs_ref.at[pos_ref[k, t]], sem)

    for t in range(tt):
        for k in range(TOP_K):
            copy(t, k).start(priority=k % 2)
    for t in range(tt):
        for k in range(TOP_K):
            copy(t, k).wait()


def _dispatch(pos_kt, h2_flat, n_rows):
    T, W = h2_flat.shape
    tt = 256
    return pl.pallas_call(
        functools.partial(_dispatch_kernel, tt=tt),
        grid=(T // tt,),
        in_specs=[pl.BlockSpec((TOP_K, tt), lambda i: (0, i), memory_space=pltpu.SMEM),
                  pl.BlockSpec((tt, W), lambda i: (i, 0))],
        out_specs=pl.BlockSpec(memory_space=pl.ANY),
        out_shape=jax.ShapeDtypeStruct((n_rows, 1, W), I32),
        scratch_shapes=[pltpu.SemaphoreType.DMA(())],
        compiler_params=pltpu.CompilerParams(dimension_semantics=("arbitrary",),
                                             has_side_effects=True),
        name="dispatch",
    )(pos_kt, h2_flat)


def _expert_kernel(be_ref, nu_ref, xs_ref, w1_ref, w3_ref, w2_ref, ys_ref, w1b, w3b, w2b):
    b = pl.program_id(0)
    prev = be_ref[jnp.maximum(b - 1, 0)]
    fresh = jnp.logical_or(b == 0, be_ref[b] != prev)

    @pl.when(jnp.logical_and(fresh, b < nu_ref[0]))
    def _():
        w1b[...] = w1_ref[...].astype(BF16)
        w3b[...] = w3_ref[...].astype(BF16)
        w2b[...] = w2_ref[...].astype(BF16)

    @pl.when(b < nu_ref[0])
    def _():
        half = ROW_PIECES * 128
        sub = EXPERT_BLOCK // EXPERT_SUB
        xs, hs = [], []
        for i in range(EXPERT_SUB):
            pieces = [xs_ref[pl.ds(i * sub * ROW_PIECES + c, sub, stride=ROW_PIECES), :]
                      for c in range(ROW_PIECES)]
            lo, hi = _unpack_pair(jnp.concatenate(pieces, axis=1))
            xs.append((lo.astype(BF16), hi.astype(BF16)))
        for lo, hi in xs:
            h1 = _dot(lo, w1b[:half, :]) + _dot(hi, w1b[half:, :])
            h3 = _dot(lo, w3b[:half, :]) + _dot(hi, w3b[half:, :])
            hs.append((h1, h3))
        for i, (h1, h3) in enumerate(hs):
            y = _dot((_silu(h1) * h3).astype(BF16), w2b[...])
            words = _pack_pair(y[:, :half], y[:, half:])
            for c in range(ROW_PIECES):
                ys_ref[pl.ds(i * sub * ROW_PIECES + c, sub, stride=ROW_PIECES), :] = (
                    words[:, c * 128:(c + 1) * 128])


def _experts(blk_expert, n_used, xs, w1, w3, w2, layer):
    P, _, W = xs.shape
    D = 2 * W
    nb = P // EXPERT_BLOCK
    xs = xs.reshape(P * ROW_PIECES, 128)
    last = lambda b, be, nu: jnp.minimum(b, nu[0] - 1)
    wspec = lambda r, c: pl.BlockSpec((None, None, r, c),
                                      lambda b, be, nu: (layer, be[last(b, be, nu)], 0, 0))
    grid_spec = pltpu.PrefetchScalarGridSpec(
        num_scalar_prefetch=2,
        grid=(nb,),
        in_specs=[pl.BlockSpec((EXPERT_BLOCK * ROW_PIECES, 128), lambda b, be, nu: (last(b, be, nu), 0)),
                  wspec(D, D_EXPERT), wspec(D, D_EXPERT), wspec(D_EXPERT, D)],
        out_specs=pl.BlockSpec((EXPERT_BLOCK * ROW_PIECES, 128), lambda b, be, nu: (last(b, be, nu), 0)),
        scratch_shapes=[pltpu.VMEM((D, D_EXPERT), BF16), pltpu.VMEM((D, D_EXPERT), BF16),
                        pltpu.VMEM((D_EXPERT, D), BF16)],
    )
    ys = pl.pallas_call(
        _expert_kernel,
        grid_spec=grid_spec,
        out_shape=jax.ShapeDtypeStruct((P * ROW_PIECES, 128), I32),
        compiler_params=_cp(("arbitrary",), VMEM_LIMIT),
        name="experts",
    )(blk_expert, n_used, xs, w1, w3, w2)
    return ys.reshape(P, 1, W)


def _combine_kernel(pos_ref, ys_ref, wt_ref, h2_ref, x1_ref, g2_ref, lg_ref, lb_ref, ws1_ref,
                    ws3_ref, ws2_ref, o_ref, buf, sem, *, tt, alpha):
    def copy(t, k):
        return pltpu.make_async_copy(ys_ref.at[pos_ref[k, t]], buf.at[k, pl.ds(t, 1), :], sem)

    for t in range(tt):
        for k in range(TOP_K):
            copy(t, k).start(priority=k % 2)
    half = h2_ref.shape[1]
    lo, hi = _unpack_pair(h2_ref[...])
    lo, hi = lo.astype(BF16), hi.astype(BF16)
    h1 = _dot(lo, ws1_ref[:half, :]) + _dot(hi, ws1_ref[half:, :])
    h3 = _dot(lo, ws3_ref[:half, :]) + _dot(hi, ws3_ref[half:, :])
    shared = _dot((_silu(h1) * h3).astype(BF16), ws2_ref[...])
    for t in range(tt):
        for k in range(TOP_K):
            copy(t, k).wait()
    wt = wt_ref[...]
    f_lo, f_hi = shared[:, :half], shared[:, half:]
    for k in range(TOP_K):
        y_lo, y_hi = _unpack_pair(buf[k])
        f_lo = f_lo + wt[:, k:k + 1] * y_lo
        f_hi = f_hi + wt[:, k:k + 1] * y_hi
    ffn = jnp.concatenate([f_lo, f_hi], axis=1)
    o_ref[...] = _ln(alpha * x1_ref[...] + g2_ref[...] * ffn) * lg_ref[...] + lb_ref[...]


def _combine(pos_kt, ys, w_tk, h2, x1, g2, ln_g, ln_b, ws1, ws3, ws2, alpha):
    B, S, D = x1.shape
    tt = min(256, S)
    nt = S // tt
    row = lambda n: pl.BlockSpec((None, tt, n), lambda b, i: (b, i, 0))
    full = lambda a: pl.BlockSpec(a.shape, lambda b, i: (0,) * a.ndim)
    return pl.pallas_call(
        functools.partial(_combine_kernel, tt=tt, alpha=alpha),
        grid=(B, nt),
        in_specs=[pl.BlockSpec((TOP_K, tt), lambda b, i: (0, b * nt + i), memory_space=pltpu.SMEM),
                  pl.BlockSpec(memory_space=pl.ANY),
                  row(TOP_K), row(D // 2), row(D),
                  pl.BlockSpec((None, 1, D), lambda b, i: (b, 0, 0)),
                  full(ln_g), full(ln_b), full(ws1), full(ws3), full(ws2)],
        out_specs=row(D),
        out_shape=jax.ShapeDtypeStruct((B, S, D), F32),
        scratch_shapes=[pltpu.VMEM((TOP_K, tt, D // 2), I32), pltpu.SemaphoreType.DMA(())],
        compiler_params=_cp(("arbitrary", "arbitrary"), VMEM_LIMIT),
        name="combine",
    )(pos_kt, ys, w_tk, h2, x1, g2, ln_g, ln_b, ws1, ws3, ws2)


def _arrange_w_in(w):
    z = lambda n: jnp.zeros((w.shape[0], n), w.dtype)
    return jnp.concatenate([w[:, 0:1280], w[:, 1296:1552], w[:, 1552:1680], w[:, 1680:1712],
                            z(96), w[:, 1280:1296], z(112)], axis=1).astype(BF16)


def _head_layout(nope, rope_a, rope_b):
    r = nope.shape[0]
    z = jnp.zeros((r, MLA_HEADS, MLA_HEAD_PAD - MLA_NOPE - MLA_ROPE), nope.dtype)
    return jnp.concatenate([nope, rope_a, rope_b, z], axis=2).reshape(r, MLA_HEADS * MLA_HEAD_PAD)


def _arrange_mla(w_uq, w_uk):
    half = MLA_ROPE // 2
    q = w_uq.reshape(MLA_Q_LORA, MLA_HEADS, MLA_NOPE + MLA_ROPE)
    qn, q1, q2 = q[..., :MLA_NOPE], q[..., MLA_NOPE:MLA_NOPE + half], q[..., MLA_NOPE + half:]
    wq = _head_layout(qn, q1, q2)
    wq_sw = _head_layout(jnp.zeros_like(qn), q2, q1)
    kn = w_uk.reshape(MLA_KV_LORA, MLA_HEADS, MLA_NOPE)
    zk = jnp.zeros((MLA_KV_LORA, MLA_HEADS, half), w_uk.dtype)
    eye = jnp.eye(MLA_ROPE, dtype=w_uk.dtype)
    e1 = jnp.broadcast_to(eye[:, None, :half], (MLA_ROPE, MLA_HEADS, half))
    e2 = jnp.broadcast_to(eye[:, None, half:], (MLA_ROPE, MLA_HEADS, half))
    zn = jnp.zeros((MLA_ROPE, MLA_HEADS, MLA_NOPE), w_uk.dtype)
    pad = jnp.zeros((128 - MLA_ROPE, MLA_HEADS * MLA_HEAD_PAD), w_uk.dtype)
    wk = jnp.concatenate([_head_layout(kn, zk, zk), _head_layout(zn, e1, e2), pad], axis=0)
    wk_sw = jnp.concatenate([_head_layout(jnp.zeros_like(kn), zk, zk), _head_layout(zn, e2, e1), pad],
                            axis=0)
    return wq.astype(BF16), wq_sw.astype(BF16), wk.astype(BF16), wk_sw.astype(BF16)


def _rope_rows():
    half = MLA_ROPE // 2
    inv_freq = ROPE_THETA ** (-jnp.arange(0, MLA_ROPE, 2, dtype=F32) / MLA_ROPE)
    z = lambda n: jnp.zeros((n,), F32)
    frq = jnp.concatenate([z(MLA_NOPE), inv_freq, inv_freq, z(MLA_HEAD_PAD - MLA_NOPE - MLA_ROPE)])
    sgn = jnp.concatenate([z(MLA_NOPE), -jnp.ones((half,), F32), jnp.ones((half,), F32),
                           z(MLA_HEAD_PAD - MLA_NOPE - MLA_ROPE)])
    return frq[None, :], sgn[None, :]


def _block_diag_pool(w_pool):
    G, C, _ = w_pool.shape
    out = jnp.zeros((G * C, G * C), w_pool.dtype)
    for g in range(G):
        out = out.at[g * C:(g + 1) * C, g * C:(g + 1) * C].set(w_pool[g])
    return out.astype(BF16)


def kernel(x, c, positions, w_ada, b_ada, w_in, w_pool, s_pool, conv_w, conv_b, gate_b, gn_w, g_q,
           g_kv, w_uq, w_uk, w_uv, w_out, ln1_g, ln1_b, w_router, e_bias, w1, w3, w2, ws1, ws3, ws2,
           ln2_g, ln2_b):
    B, S, D = x.shape
    depth = w_in.shape[0]
    T = B * S
    H = ML_HEADS
    alpha = float((2 * depth) ** 0.25)
    n_assign = T * TOP_K
    n_blocks = n_assign // EXPERT_BLOCK + N_EXPERTS
    n_rows = n_blocks * EXPERT_BLOCK

    ada = _ada(c, w_ada, b_ada)
    rope_cos, rope_sin = _rope_tables(positions.reshape(B, S, 1), *_rope_rows())
    qk_scale = jnp.concatenate([jnp.ones((1, ML_WIDTH), F32),
                                jnp.full((1, ML_WIDTH), ML_DH ** -0.5, F32)], axis=1)
    ones_row = lambda n: jnp.ones((B, n, 1, S), BF16)
    zero_rows = lambda n, r: jnp.zeros((B, n, r, S), BF16)

    for l in range(depth):
        sh1, sc1, g1, sh2, sc2, g2 = [a.reshape(B, 1, D) for a in jnp.split(ada[l], 6, axis=-1)]
        u_pool, u_qk, u_v, u_o, u_dq, u_kv, u_gate = _inproj(x, sc1, sh1, _arrange_w_in(w_in[l]))
        y_pool = _pool(u_pool, _block_diag_pool(w_pool[l]), s_pool[l][None, :])

        qk_act = _conv(u_qk, conv_w[l], conv_b[l][None, :], qk_scale)
        heads_t = lambda a: a.reshape(B, S, H, ML_DH).transpose(0, 2, 3, 1)
        gates = u_gate[:, :, :4 * H].reshape(B, S, 4, H)
        vt_ml = jnp.concatenate([heads_t(u_v), ones_row(H), zero_rows(H, ML_VT_ROWS - ML_DH - 1)],
                                axis=2)
        y_ml = _mlstm(qk_act[:, :, ML_WIDTH:].reshape(B, S, H, ML_DH).transpose(0, 2, 1, 3),
                      heads_t(qk_act[:, :, :ML_WIDTH]), vt_ml, heads_t(u_o),
                      gates.transpose(0, 3, 1, 2),
                      gates.transpose(0, 3, 2, 1),
                      gate_b[l].reshape(4, H).T.reshape(H, 1, 4),
                      gate_b[l].reshape(4, H).T.reshape(H, 4, 1),
                      gn_w[l].reshape(H, ML_DH, 1))
        y_ml = y_ml.transpose(0, 3, 1, 2).reshape(B, S, ML_WIDTH)

        wq, wq_sw, wk, wk_sw = _arrange_mla(w_uq[l], w_uk[l])
        qh, kh, vh = _mla_proj(u_dq, u_kv, rope_cos, rope_sin, g_q[l][None, :], g_kv[l][None, :],
                               wq, wq_sw, wk, wk_sw, w_uv[l].astype(BF16))
        y_mla = _attn(qh, kh, vh.transpose(0, 2, 1)).transpose(0, 2, 1)

        x1, h2, logits_t = _outproj(y_pool, y_ml, y_mla, x, g1, ln1_g[l][None, :], ln1_b[l][None, :],
                                    sc2, sh2, w_out[l].astype(BF16), w_router[l].T.astype(BF16), alpha)
        idx_t, w_t = _route(logits_t, e_bias[l][:, None])
        rank_t, counts = _rank(idx_t)
        counts = counts[:, 0].astype(I32)
        padded = (counts + EXPERT_BLOCK - 1) // EXPERT_BLOCK * EXPERT_BLOCK
        pend = jnp.cumsum(padded)
        pstart = pend - padded
        n_used = (pend[-1:] // EXPERT_BLOCK).astype(I32)
        blk_start = jnp.arange(n_blocks, dtype=I32) * EXPERT_BLOCK
        blk_expert = jnp.minimum(jnp.sum((pend[None, :] <= blk_start[:, None]).astype(I32), axis=1),
                                 N_EXPERTS - 1)
        pos_kt = _pos(idx_t, rank_t, pstart.astype(F32)[:, None]).transpose(1, 0, 2).reshape(TOP_K, T)
        xs = _dispatch(pos_kt, h2.reshape(T, D // 2), n_rows)
        ys = _experts(blk_expert, n_used, xs, w1, w3, w2, l)
        x = _combine(pos_kt, ys, w_t.transpose(0, 2, 1), h2, x1, g2, ln2_g[l][None, :],
                     ln2_b[l][None, :], ws1[l].astype(BF16), ws3[l].astype(BF16),
                     ws2[l].astype(BF16), alpha)
    return x
```
